```python
import math
import jax, jax.numpy as jnp
from jax import lax
import numpy as np

D_MODEL = 2048
BATCH = 8
SEQ = 2048
DEPTH = 2

CHUNK = 64
Q_BLOCK = 128
N_MIXERS = 2
ROPE_THETA = 500000.0
ROT_FRACTION = 4
EPS = 1e-6
BRANCH_WIDTH = D_MODEL

DIFF_HEAD_DIM = 128
DIFF_HEADS = BRANCH_WIDTH // (2 * DIFF_HEAD_DIM)
FOX_HEAD_DIM = 128
FOX_HEADS = BRANCH_WIDTH // FOX_HEAD_DIM

N_LAYERS_A = (DEPTH + 1) // 2
N_LAYERS_B = DEPTH // 2

kernel_name = "hybrid_diffattn_fox_gated_trunk"


def rms_norm(x, gain):
    xf = x.astype(jnp.float32)
    y = xf * lax.rsqrt(jnp.mean(xf * xf, axis=-1, keepdims=True) + EPS)
    return (y * gain.astype(jnp.float32)).astype(x.dtype)


def partial_rope(x, positions):
    rot = x.shape[-1] // ROT_FRACTION
    half = rot // 2
    inv_freq = ROPE_THETA ** (-jnp.arange(half, dtype=jnp.float32) / half)
    ang = positions.astype(jnp.float32)[:, :, None, None] * inv_freq
    cos, sin = jnp.cos(ang), jnp.sin(ang)
    xr = x[..., :rot].astype(jnp.float32)
    x1, x2 = xr[..., :half], xr[..., half:]
    xr = jnp.concatenate([x1 * cos - x2 * sin, x2 * cos + x1 * sin], axis=-1)
    return jnp.concatenate([xr.astype(x.dtype), x[..., rot:]], axis=-1)


def chunk_causal_diff_attention(q1, q2, k1, k2, v, lam):
    S = q1.shape[1]
    scale = q1.shape[-1] ** -0.5
    outs = []
    for start in range(0, S, Q_BLOCK):
        end = start + Q_BLOCK
        mask = (jnp.arange(end) // CHUNK)[None, :] <= (jnp.arange(start, end) // CHUNK)[:, None]

        def attn_map(q, k):
            s = jnp.einsum("bqhd,bkhd->bhqk", q[:, start:end].astype(jnp.float32),
                           k[:, :end].astype(jnp.float32)) * scale
            return jax.nn.softmax(jnp.where(mask, s, -jnp.inf), axis=-1)

        p = attn_map(q1, k1) - lam * attn_map(q2, k2)
        outs.append(jnp.einsum("bhqk,bkhe->bqhe", p.astype(v.dtype), v[:, :end]))
    return jnp.concatenate(outs, axis=1)


def forgetting_attention(q, k, v, log_f):
    S = q.shape[1]
    scale = q.shape[-1] ** -0.5
    cum = jnp.transpose(jnp.cumsum(log_f, axis=1), (0, 2, 1))
    outs = []
    for start in range(0, S, Q_BLOCK):
        end = start + Q_BLOCK
        mask = jnp.arange(end)[None, :] <= jnp.arange(start, end)[:, None]
        s = jnp.einsum("bqhd,bkhd->bhqk", q[:, start:end].astype(jnp.float32),
                       k[:, :end].astype(jnp.float32)) * scale
        s = s + (cum[:, :, start:end, None] - cum[:, :, None, :end])
        p = jax.nn.softmax(jnp.where(mask, s, -jnp.inf), axis=-1)
        outs.append(jnp.einsum("bhqk,bkhd->bqhd", p.astype(v.dtype), v[:, :end]))
    return jnp.concatenate(outs, axis=1)


def diff_attn_layer(x, positions, norm_g, w_in, q_norm_g, k_norm_g,
                    lq1, lk1, lq2, lk2, sub_norm_g, w_out, layer_idx):
    B, S, _ = x.shape
    lambda_init = 0.8 - 0.6 * math.exp(-0.3 * layer_idx)
    h = rms_norm(x, norm_g)
    q, k, v, gate = jnp.split(h @ w_in, 4, axis=-1)
    q = partial_rope(rms_norm(q.reshape(B, S, 2 * DIFF_HEADS, DIFF_HEAD_DIM), q_norm_g), positions)
    k = partial_rope(rms_norm(k.reshape(B, S, 2 * DIFF_HEADS, DIFF_HEAD_DIM), k_norm_g), positions)
    q = q.reshape(B, S, DIFF_HEADS, 2, DIFF_HEAD_DIM)
    k = k.reshape(B, S, DIFF_HEADS, 2, DIFF_HEAD_DIM)
    v = v.reshape(B, S, DIFF_HEADS, 2 * DIFF_HEAD_DIM)
    lam = (jnp.exp(jnp.sum(lq1.astype(jnp.float32) * lk1.astype(jnp.float32)))
           - jnp.exp(jnp.sum(lq2.astype(jnp.float32) * lk2.astype(jnp.float32)))
           + lambda_init)
    o = chunk_causal_diff_attention(q[:, :, :, 0], q[:, :, :, 1], k[:, :, :, 0], k[:, :, :, 1], v, lam)
    o = rms_norm(o, sub_norm_g) * (1.0 - lambda_init)
    o = o.reshape(B, S, BRANCH_WIDTH) * jax.nn.silu(gate)
    return x + o @ w_out


def forgetting_attn_layer(x, norm_g, w_in, f_bias, q_norm_g, k_norm_g, w_out):
    B, S, _ = x.shape
    h = rms_norm(x, norm_g)
    proj = h @ w_in
    q, k, v, gate = jnp.split(proj[..., :4 * BRANCH_WIDTH], 4, axis=-1)
    log_f = jax.nn.log_sigmoid(proj[..., 4 * BRANCH_WIDTH:].astype(jnp.float32)
                               + f_bias.astype(jnp.float32))
    q = rms_norm(q.reshape(B, S, FOX_HEADS, FOX_HEAD_DIM), q_norm_g)
    k = rms_norm(k.reshape(B, S, FOX_HEADS, FOX_HEAD_DIM), k_norm_g)
    v = v.reshape(B, S, FOX_HEADS, FOX_HEAD_DIM)
    o = forgetting_attention(q, k, v, log_f).reshape(B, S, BRANCH_WIDTH) * jax.nn.silu(gate)
    return x + o @ w_out


def setup_inputs(seed: int = 0) -> dict:
    key = jax.random.key(seed)
    ks = jax.random.split(key, 20)
    f32 = jnp.float32
    NA, NB = N_LAYERS_A, N_LAYERS_B

    def gain(k, shape):
        return 1.0 + 0.02 * jax.random.normal(k, shape, f32)

    x = jax.random.normal(ks[0], (BATCH, SEQ, D_MODEL), f32)
    offsets = jax.random.randint(ks[1], (BATCH, 1), 0, 64, dtype=jnp.int32) * CHUNK
    positions = (offsets + jnp.arange(SEQ, dtype=jnp.int32)[None, :]).astype(jnp.int32)
    return {
        "x": x,
        "positions": positions,
        "a_norm": gain(ks[2], (NA, D_MODEL)),
        "a_w_in": jax.random.normal(ks[3], (NA, D_MODEL, 4 * BRANCH_WIDTH), f32) * D_MODEL ** -0.5,
        "a_q_norm": gain(ks[4], (NA, DIFF_HEAD_DIM)),
        "a_k_norm": gain(ks[5], (NA, DIFF_HEAD_DIM)),
        "a_lambda_q1": 0.1 * jax.random.normal(ks[6], (NA, DIFF_HEAD_DIM), f32),
        "a_lambda_k1": 0.1 * jax.random.normal(ks[7], (NA, DIFF_HEAD_DIM), f32),
        "a_lambda_q2": 0.1 * jax.random.normal(ks[8], (NA, DIFF_HEAD_DIM), f32),
        "a_lambda_k2": 0.1 * jax.random.normal(ks[9], (NA, DIFF_HEAD_DIM), f32),
        "a_sub_norm": gain(ks[10], (NA, 2 * DIFF_HEAD_DIM)),
        "a_w_out": jax.random.normal(ks[11], (NA, BRANCH_WIDTH, D_MODEL), f32) * BRANCH_WIDTH ** -0.5,
        "b_norm": gain(ks[12], (NB, D_MODEL)),
        "b_w_in": jax.random.normal(ks[13], (NB, D_MODEL, 4 * BRANCH_WIDTH + FOX_HEADS), f32) * D_MODEL ** -0.5,
        "b_f_bias": jax.random.uniform(ks[14], (NB, FOX_HEADS), f32, minval=1.0, maxval=5.0),
        "b_q_norm": gain(ks[15], (NB, FOX_HEAD_DIM)),
        "b_k_norm": gain(ks[16], (NB, FOX_HEAD_DIM)),
        "b_w_out": jax.random.normal(ks[17], (NB, BRANCH_WIDTH, D_MODEL), f32) * BRANCH_WIDTH ** -0.5,
    }


def reference(x, positions, a_norm, a_w_in, a_q_norm, a_k_norm, a_lambda_q1, a_lambda_k1,
              a_lambda_q2, a_lambda_k2, a_sub_norm, a_w_out, b_norm, b_w_in, b_f_bias,
              b_q_norm, b_k_norm, b_w_out):
    for i in range(DEPTH):
        j = i // N_MIXERS
        if i % N_MIXERS == 0:
            x = diff_attn_layer(x, positions, a_norm[j], a_w_in[j], a_q_norm[j], a_k_norm[j],
                                a_lambda_q1[j], a_lambda_k1[j], a_lambda_q2[j], a_lambda_k2[j],
                                a_sub_norm[j], a_w_out[j], layer_idx=i)
        else:
            x = forgetting_attn_layer(x, b_norm[j], b_w_in[j], b_f_bias[j], b_q_norm[j],
                                      b_k_norm[j], b_w_out[j])
    return x
```

```python
import functools
import math

import numpy as np
import jax
import jax.numpy as jnp
from jax import lax
from jax.experimental import pallas as pl
from jax.experimental.pallas import tpu as pltpu

F32 = jnp.float32
BF16 = jnp.bfloat16

EPS = 1e-6
CHUNK = 64
ROPE_THETA = 500000.0
HEAD_DIM = 128
ROT_DIM = HEAD_DIM // 4
ROT_HALF = ROT_DIM // 2
LOG2E = math.log2(math.e)
MASKED = -1e30

V7X_VMEM_LIMIT_BYTES = 56 * 1024 * 1024

PROJ_TM = 1024
PROJ_TN = 1024
ATTN_TQ = 256
CUM_BLK = 256


def _compiler_params(semantics):
    return pltpu.CompilerParams(dimension_semantics=semantics,
                                vmem_limit_bytes=V7X_VMEM_LIMIT_BYTES)


def _dot(a, b):
    return jnp.dot(a, b, preferred_element_type=F32)


def _dot_nt(a, b):
    return lax.dot_general(a, b, (((1,), (1,)), ((), ())), preferred_element_type=F32)


def _rope_tables_kernel(pos_ref, invf_ref, c_ref, s1_ref, s2_ref):
    ang = invf_ref[...] * pos_ref[...]
    c = jnp.cos(ang)
    s = jnp.sin(ang)
    tb = ang.shape[1]
    ones = jnp.ones((HEAD_DIM - ROT_DIM, tb), F32)
    c_ref[...] = jnp.concatenate([c, c, ones], axis=0).T
    s1_ref[...] = jnp.concatenate([-s, jnp.zeros((HEAD_DIM - ROT_HALF, tb), F32)], axis=0).T
    s2_ref[...] = jnp.concatenate([jnp.zeros((ROT_HALF, tb), F32), s,
                                   jnp.zeros((HEAD_DIM - ROT_DIM, tb), F32)], axis=0).T


def _rope_tables(positions):
    T = positions.size
    tb = 2048
    pos = positions.reshape(1, T).astype(F32)
    invf = (ROPE_THETA ** (-np.arange(ROT_HALF, dtype=np.float32) / ROT_HALF)).astype(np.float32)
    invf = jnp.asarray(invf.reshape(ROT_HALF, 1))
    tab = jax.ShapeDtypeStruct((T, HEAD_DIM), F32)
    return pl.pallas_call(
        _rope_tables_kernel,
        grid=(T // tb,),
        in_specs=[pl.BlockSpec((1, tb), lambda i: (0, i)),
                  pl.BlockSpec((ROT_HALF, 1), lambda i: (0, 0))],
        out_specs=[pl.BlockSpec((tb, HEAD_DIM), lambda i: (i, 0))] * 3,
        out_shape=[tab, tab, tab],
        compiler_params=_compiler_params(("parallel",)),
        name="rope_tables",
    )(pos, invf)


def _inproj_kernel(*refs, n_normed_tiles, rope, with_f):
    refs = list(refs)
    x_ref, ng_ref, w_ref, cg_ref = refs[:4]
    pos = 4
    if rope:
        c_ref, s1_ref, s2_ref = refs[pos:pos + 3]
        pos += 3
    if with_f:
        wf_ref, fb_ref = refs[pos:pos + 2]
        pos += 2
    out_ref = refs[pos]
    pos += 1
    if with_f:
        logf_ref = refs[pos]
        pos += 1
    h_scr = refs[pos]

    j = pl.program_id(1)

    @pl.when(j == 0)
    def _():
        xf = x_ref[...]
        ms = jnp.mean(xf * xf, axis=-1, keepdims=True)
        h = xf * lax.rsqrt(ms + EPS) * ng_ref[...]
        h_scr[...] = h.astype(BF16)
        if with_f:
            f = _dot(h_scr[...], wf_ref[...]) + fb_ref[...]
            logf_ref[...] = jnp.minimum(f, 0.0) - jnp.log(1.0 + jnp.exp(-jnp.abs(f)))

    acc = _dot(h_scr[...], w_ref[...])
    tn = acc.shape[1]

    @pl.when(j < n_normed_tiles)
    def _():
        for hh in range(tn // HEAD_DIM):
            sl = slice(hh * HEAD_DIM, (hh + 1) * HEAD_DIM)
            a = acc[:, sl]
            r = lax.rsqrt(jnp.mean(a * a, axis=-1, keepdims=True) + EPS)
            y = a * r * cg_ref[:, sl]
            if rope:
                y = (y * c_ref[...]
                     + pltpu.roll(y, HEAD_DIM - ROT_HALF, 1) * s1_ref[...]
                     + pltpu.roll(y, ROT_HALF, 1) * s2_ref[...])
            out_ref[:, sl] = y.astype(BF16)

    @pl.when(j >= n_normed_tiles)
    def _():
        out_ref[...] = acc.astype(BF16)


def _inproj(x2, norm_g, w_bf, col_gain, n_normed_cols, rope_tabs=None, wf_bf=None, f_bias=None):
    T, D = x2.shape
    N = w_bf.shape[1]
    tm, tn = PROJ_TM, PROJ_TN
    rope = rope_tabs is not None
    with_f = wf_bf is not None
    in_specs = [pl.BlockSpec((tm, D), lambda i, j: (i, 0)),
                pl.BlockSpec((1, D), lambda i, j: (0, 0)),
                pl.BlockSpec((D, tn), lambda i, j: (0, j)),
                pl.BlockSpec((1, tn), lambda i, j: (0, j))]
    args = [x2, norm_g.reshape(1, D), w_bf, col_gain]
    if rope:
        in_specs += [pl.BlockSpec((tm, HEAD_DIM), lambda i, j: (i, 0))] * 3
        args += list(rope_tabs)
    if with_f:
        in_specs += [pl.BlockSpec((D, HEAD_DIM), lambda i, j: (0, 0)),
                     pl.BlockSpec((1, HEAD_DIM), lambda i, j: (0, 0))]
        args += [wf_bf, f_bias]
    out_specs = [pl.BlockSpec((tm, tn), lambda i, j: (i, j))]
    out_shape = [jax.ShapeDtypeStruct((T, N), BF16)]
    if with_f:
        out_specs.append(pl.BlockSpec((tm, HEAD_DIM), lambda i, j: (i, 0)))
        out_shape.append(jax.ShapeDtypeStruct((T, HEAD_DIM), F32))
    res = pl.pallas_call(
        functools.partial(_inproj_kernel, n_normed_tiles=n_normed_cols // tn,
                          rope=rope, with_f=with_f),
        grid=(T // tm, N // tn),
        in_specs=in_specs,
        out_specs=out_specs,
        out_shape=out_shape,
        scratch_shapes=[pltpu.VMEM((tm, D), BF16)],
        compiler_params=_compiler_params(("parallel", "arbitrary")),
        name="inproj_rope" if rope else "inproj_fgate",
    )(*args)
    return res


def _outproj_kernel(o_ref, w_ref, x_ref, out_ref):
    out_ref[...] = x_ref[...] + _dot(o_ref[...], w_ref[...])


def _outproj(o_bf, w_bf, x2):
    T, D = x2.shape
    K = o_bf.shape[1]
    tm, tn = PROJ_TM, PROJ_TN
    return pl.pallas_call(
        _outproj_kernel,
        grid=(T // tm, D // tn),
        in_specs=[pl.BlockSpec((tm, K), lambda i, j: (i, 0)),
                  pl.BlockSpec((K, tn), lambda i, j: (0, j)),
                  pl.BlockSpec((tm, tn), lambda i, j: (i, j))],
        out_specs=pl.BlockSpec((tm, tn), lambda i, j: (i, j)),
        out_shape=jax.ShapeDtypeStruct((T, D), F32),
        compiler_params=_compiler_params(("parallel", "arbitrary")),
        name="outproj_residual",
    )(o_bf, w_bf, x2)


def _softmax_parts(s, r0, mask):
    right = jnp.where(mask, s[:, r0:], MASKED)
    m = jnp.max(right, axis=-1, keepdims=True)
    if r0 > 0:
        left = s[:, :r0]
        m = jnp.maximum(m, jnp.max(left, axis=-1, keepdims=True))
        p_left = jnp.exp2(left - m)
    else:
        p_left = None
    p_right = jnp.exp2(right - m)
    l = jnp.sum(p_right, axis=-1, keepdims=True)
    if r0 > 0:
        l = l + jnp.sum(p_left, axis=-1, keepdims=True)
    return p_left, p_right, l


def _silu(g):
    return g * (1.0 / (1.0 + jnp.exp(-g)))


def _diff_attn_kernel(q_ref, k_ref, v_ref, g_ref, lq1_ref, lk1_ref, lq2_ref, lk2_ref, sg_ref,
                      out_ref, *, lambda_init):
    S = q_ref.shape[0]
    tq = ATTN_TQ
    lam = (jnp.exp(jnp.sum(lq1_ref[...] * lk1_ref[...], axis=-1, keepdims=True))
           - jnp.exp(jnp.sum(lq2_ref[...] * lk2_ref[...], axis=-1, keepdims=True))
           + lambda_init)
    row_chunk = lax.broadcasted_iota(jnp.int32, (tq, tq), 0) // CHUNK
    col_chunk = lax.broadcasted_iota(jnp.int32, (tq, tq), 1) // CHUNK
    mask = col_chunk <= row_chunk
    sub_gain = sg_ref[...] * (1.0 - lambda_init)

    for qi in range(S // tq):
        r0 = qi * tq
        kv = r0 + tq
        q1 = q_ref[r0:kv, :HEAD_DIM]
        q2 = q_ref[r0:kv, HEAD_DIM:]
        s1 = _dot_nt(q1, k_ref[:kv, :HEAD_DIM])
        s2 = _dot_nt(q2, k_ref[:kv, HEAD_DIM:])
        p1l, p1r, l1 = _softmax_parts(s1, r0, mask)
        p2l, p2r, l2 = _softmax_parts(s2, r0, mask)
        w1 = 1.0 / l1
        w2 = lam * (1.0 / l2)
        p = p1r * w1 - p2r * w2
        if r0 > 0:
            p = jnp.concatenate([p1l * w1 - p2l * w2, p], axis=1)
        o = _dot(p.astype(BF16), v_ref[:kv, :])
        o = o * lax.rsqrt(jnp.mean(o * o, axis=-1, keepdims=True) + EPS) * sub_gain
        o = o * _silu(g_ref[r0:kv, :].astype(F32))
        out_ref[r0:kv, :] = o.astype(BF16)


def _diff_attention(qkvg, B, S, lq1, lk1, lq2, lk2, sub_g, lambda_init):
    T = B * S
    width = 2 * HEAD_DIM
    heads = (qkvg.shape[1] // 4) // width
    vec = lambda a: a.reshape(1, -1)
    small = lambda n: pl.BlockSpec((1, n), lambda b, h: (0, 0))
    return pl.pallas_call(
        functools.partial(_diff_attn_kernel, lambda_init=lambda_init),
        grid=(B, heads),
        in_specs=[pl.BlockSpec((S, width), lambda b, h: (b, h)),
                  pl.BlockSpec((S, width), lambda b, h: (b, heads + h)),
                  pl.BlockSpec((S, width), lambda b, h: (b, 2 * heads + h)),
                  pl.BlockSpec((S, width), lambda b, h: (b, 3 * heads + h)),
                  small(HEAD_DIM), small(HEAD_DIM), small(HEAD_DIM), small(HEAD_DIM),
                  small(width)],
        out_specs=pl.BlockSpec((S, width), lambda b, h: (b, h)),
        out_shape=jax.ShapeDtypeStruct((T, heads * width), BF16),
        compiler_params=_compiler_params(("parallel", "parallel")),
        name="diff_attention",
    )(qkvg, qkvg, qkvg, qkvg, vec(lq1), vec(lk1), vec(lq2), vec(lk2), vec(sub_g))


def _cumsum_kernel(logf_ref, out_ref, *, n_heads):
    S = logf_ref.shape[1]
    row = lax.broadcasted_iota(jnp.int32, (CUM_BLK, CUM_BLK), 0)
    col = lax.broadcasted_iota(jnp.int32, (CUM_BLK, CUM_BLK), 1)
    tri = (col <= row).astype(F32)
    off = jnp.zeros((1, HEAD_DIM), F32)
    for kb in range(S // CUM_BLK):
        blk = logf_ref[0, kb * CUM_BLK:(kb + 1) * CUM_BLK, :]
        c = jnp.dot(tri, blk, preferred_element_type=F32,
                    precision=lax.Precision.HIGHEST) + off
        off = c[CUM_BLK - 1:CUM_BLK, :]
        ct = c.T
        out_ref[0, :, kb * CUM_BLK:(kb + 1) * CUM_BLK] = ct[:n_heads, :] * (-LOG2E)


def _forget_bias(logf, B, S, n_heads):
    return pl.pallas_call(
        functools.partial(_cumsum_kernel, n_heads=n_heads),
        grid=(B,),
        in_specs=[pl.BlockSpec((1, S, HEAD_DIM), lambda b: (b, 0, 0))],
        out_specs=pl.BlockSpec((1, n_heads, S), lambda b: (b, 0, 0)),
        out_shape=jax.ShapeDtypeStruct((B, n_heads, S), F32),
        compiler_params=_compiler_params(("parallel",)),
        name="forget_prefix_sum",
    )(logf.reshape(B, S, HEAD_DIM))


def _fox_attn_kernel(q_ref, k_ref, v_ref, g_ref, bias_ref, out_ref):
    S = q_ref.shape[0]
    tq = ATTN_TQ
    heads_per_step = q_ref.shape[1] // HEAD_DIM
    row = lax.broadcasted_iota(jnp.int32, (tq, tq), 0)
    col = lax.broadcasted_iota(jnp.int32, (tq, tq), 1)
    mask = col <= row
    for hh in range(heads_per_step):
        sl = slice(hh * HEAD_DIM, (hh + 1) * HEAD_DIM)
        for qi in range(S // tq):
            r0 = qi * tq
            kv = r0 + tq
            s = _dot_nt(q_ref[r0:kv, sl], k_ref[:kv, sl]) + bias_ref[hh, :, :kv]
            pl_, pr, l = _softmax_parts(s, r0, mask)
            p = pr if pl_ is None else jnp.concatenate([pl_, pr], axis=1)
            o = _dot(p.astype(BF16), v_ref[:kv, sl]) * (1.0 / l)
            o = o * _silu(g_ref[r0:kv, sl].astype(F32))
            out_ref[r0:kv, sl] = o.astype(BF16)


def _fox_attention(qkvg, bias, B, S):
    T = B * S
    hps = 2
    width = hps * HEAD_DIM
    groups = (qkvg.shape[1] // 4) // width
    return pl.pallas_call(
        _fox_attn_kernel,
        grid=(B, groups),
        in_specs=[pl.BlockSpec((S, width), lambda b, h: (b, h)),
                  pl.BlockSpec((S, width), lambda b, h: (b, groups + h)),
                  pl.BlockSpec((S, width), lambda b, h: (b, 2 * groups + h)),
                  pl.BlockSpec((S, width), lambda b, h: (b, 3 * groups + h)),
                  pl.BlockSpec((hps, 1, S), lambda b, h: (b * groups + h, 0, 0))],
        out_specs=pl.BlockSpec((S, width), lambda b, h: (b, h)),
        out_shape=jax.ShapeDtypeStruct((T, groups * width), BF16),
        compiler_params=_compiler_params(("parallel", "parallel")),
        name="fox_attention",
    )(qkvg, qkvg, qkvg, qkvg, bias)


def kernel(x, positions, a_norm, a_w_in, a_q_norm, a_k_norm, a_lambda_q1, a_lambda_k1,
           a_lambda_q2, a_lambda_k2, a_sub_norm, a_w_out, b_norm, b_w_in, b_f_bias,
           b_q_norm, b_k_norm, b_w_out):
    B, S, D = x.shape
    T = B * S
    width = a_w_out.shape[1]
    heads_per_width = width // HEAD_DIM
    q_scale = HEAD_DIM ** -0.5 * LOG2E
    x2 = x.reshape(T, D)

    def col_gain(qg, kg):
        return jnp.concatenate([jnp.tile(qg * q_scale, heads_per_width),
                                jnp.tile(kg, heads_per_width)]).reshape(1, 2 * width)

    lambda_init0 = 0.8 - 0.6 * math.exp(-0.3 * 0)
    tabs = _rope_tables(positions)
    (qkvg,) = _inproj(x2, a_norm[0], a_w_in[0].astype(BF16),
                      jnp.pad(col_gain(a_q_norm[0], a_k_norm[0]), ((0, 0), (0, 2 * width)),
                              constant_values=1.0),
                      2 * width, rope_tabs=tabs)
    o = _diff_attention(qkvg, B, S, a_lambda_q1[0], a_lambda_k1[0], a_lambda_q2[0],
                        a_lambda_k2[0], a_sub_norm[0], lambda_init0)
    x2 = _outproj(o, a_w_out[0].astype(BF16), x2)

    w1 = b_w_in[0]
    n_f = w1.shape[1] - 4 * width
    wf = jnp.pad(w1[:, 4 * width:], ((0, 0), (0, HEAD_DIM - n_f))).astype(BF16)
    fb = jnp.pad(b_f_bias[0], (0, HEAD_DIM - n_f)).reshape(1, HEAD_DIM)
    qkvg, logf = _inproj(x2, b_norm[0], w1[:, :4 * width].astype(BF16),
                         jnp.pad(col_gain(b_q_norm[0], b_k_norm[0]), ((0, 0), (0, 2 * width)),
                                 constant_values=1.0),
                         2 * width, wf_bf=wf, f_bias=fb)
    bias = _forget_bias(logf, B, S, n_f).reshape(B * n_f, 1, S)
    o = _fox_attention(qkvg, bias, B, S)
    x2 = _outproj(o, b_w_out[0].astype(BF16), x2)
    return x2.reshape(B, S, D)
```

```python
import functools
import math

import numpy as np
import jax
import jax.numpy as jnp
from jax import lax
from jax.experimental import pallas as pl
from jax.experimental.pallas import tpu as pltpu

F32 = jnp.float32
BF16 = jnp.bfloat16

EPS = 1e-6
CHUNK = 64
ROPE_THETA = 500000.0
HEAD_DIM = 128
BF16_ROWS = 16
ROT_DIM = HEAD_DIM // 4
ROT_HALF = ROT_DIM // 2
LOG2E = math.log2(math.e)
MASKED = -1e30

V7X_VMEM_LIMIT_BYTES = 56 * 1024 * 1024

PROJ_TM = 1024
PROJ_TN = 1024
ATTN_TQ = 512
CUM_BLK = 256


def _compiler_params(semantics):
    return pltpu.CompilerParams(dimension_semantics=semantics,
                                vmem_limit_bytes=V7X_VMEM_LIMIT_BYTES)


def _dot(a, b):
    return jnp.dot(a, b, preferred_element_type=F32)


def _dot_nt(a, b):
    return lax.dot_general(a, b, (((1,), (1,)), ((), ())), preferred_element_type=F32)


def _rope_tables_kernel(pos_ref, invf_ref, c_ref, s1_ref, s2_ref):
    ang = invf_ref[...] * pos_ref[...]
    c = jnp.cos(ang)
    s = jnp.sin(ang)
    tb = ang.shape[1]
    ones = jnp.ones((HEAD_DIM - ROT_DIM, tb), F32)
    c_ref[...] = jnp.concatenate([c, c, ones], axis=0).T
    s1_ref[...] = jnp.concatenate([-s, jnp.zeros((HEAD_DIM - ROT_HALF, tb), F32)], axis=0).T
    s2_ref[...] = jnp.concatenate([jnp.zeros((ROT_HALF, tb), F32), s,
                                   jnp.zeros((HEAD_DIM - ROT_DIM, tb), F32)], axis=0).T


def _rope_tables(positions):
    T = positions.size
    tb = 2048
    pos = positions.reshape(1, T).astype(F32)
    invf = (ROPE_THETA ** (-np.arange(ROT_HALF, dtype=np.float32) / ROT_HALF)).astype(np.float32)
    invf = jnp.asarray(invf.reshape(ROT_HALF, 1))
    tab = jax.ShapeDtypeStruct((T, HEAD_DIM), F32)
    return pl.pallas_call(
        _rope_tables_kernel,
        grid=(T // tb,),
        in_specs=[pl.BlockSpec((1, tb), lambda i: (0, i)),
                  pl.BlockSpec((ROT_HALF, 1), lambda i: (0, 0))],
        out_specs=[pl.BlockSpec((tb, HEAD_DIM), lambda i: (i, 0))] * 3,
        out_shape=[tab, tab, tab],
        compiler_params=_compiler_params(("parallel",)),
        name="rope_tables",
    )(pos, invf)


def _inproj_kernel(*refs, n_normed_tiles, rope, with_f):
    refs = list(refs)
    x_ref, ng_ref, w_ref, cg_ref = refs[:4]
    pos = 4
    if rope:
        c_ref, s1_ref, s2_ref = refs[pos:pos + 3]
        pos += 3
    if with_f:
        wf_ref, fb_ref = refs[pos:pos + 2]
        pos += 2
    out_ref = refs[pos]
    pos += 1
    if with_f:
        logf_ref = refs[pos]
        pos += 1
    h_scr = refs[pos]

    j = pl.program_id(1)

    @pl.when(j == 0)
    def _():
        xf = x_ref[...]
        ms = jnp.mean(xf * xf, axis=-1, keepdims=True)
        h = xf * lax.rsqrt(ms + EPS) * ng_ref[...]
        h_scr[...] = h.astype(BF16)
        if with_f:
            f = _dot(h_scr[...], wf_ref[...]) + fb_ref[...]
            logf_ref[...] = jnp.minimum(f, 0.0) - jnp.log(1.0 + jnp.exp(-jnp.abs(f)))

    acc = _dot(h_scr[...], w_ref[...])
    tn = acc.shape[1]

    @pl.when(j < n_normed_tiles)
    def _():
        for hh in range(tn // HEAD_DIM):
            sl = slice(hh * HEAD_DIM, (hh + 1) * HEAD_DIM)
            a = acc[:, sl]
            r = lax.rsqrt(jnp.mean(a * a, axis=-1, keepdims=True) + EPS)
            y = a * r * cg_ref[:, sl]
            if rope:
                y = (y * c_ref[...]
                     + pltpu.roll(y, HEAD_DIM - ROT_HALF, 1) * s1_ref[...]
                     + pltpu.roll(y, ROT_HALF, 1) * s2_ref[...])
            out_ref[:, sl] = y.astype(BF16)

    @pl.when(j >= n_normed_tiles)
    def _():
        out_ref[...] = acc.astype(BF16)


def _inproj(x2, norm_g, w_bf, col_gain, n_normed_cols, rope_tabs=None, wf_bf=None, f_bias=None):
    T, D = x2.shape
    N = w_bf.shape[1]
    tm, tn = PROJ_TM, PROJ_TN
    rope = rope_tabs is not None
    with_f = wf_bf is not None
    in_specs = [pl.BlockSpec((tm, D), lambda i, j: (i, 0)),
                pl.BlockSpec((1, D), lambda i, j: (0, 0)),
                pl.BlockSpec((D, tn), lambda i, j: (0, j)),
                pl.BlockSpec((1, tn), lambda i, j: (0, j))]
    args = [x2, norm_g.reshape(1, D), w_bf, col_gain]
    if rope:
        in_specs += [pl.BlockSpec((tm, HEAD_DIM), lambda i, j: (i, 0))] * 3
        args += list(rope_tabs)
    if with_f:
        in_specs += [pl.BlockSpec((D, HEAD_DIM), lambda i, j: (0, 0)),
                     pl.BlockSpec((1, HEAD_DIM), lambda i, j: (0, 0))]
        args += [wf_bf, f_bias]
    out_specs = [pl.BlockSpec((tm, tn), lambda i, j: (i, j))]
    out_shape = [jax.ShapeDtypeStruct((T, N), BF16)]
    if with_f:
        out_specs.append(pl.BlockSpec((tm, HEAD_DIM), lambda i, j: (i, 0)))
        out_shape.append(jax.ShapeDtypeStruct((T, HEAD_DIM), F32))
    res = pl.pallas_call(
        functools.partial(_inproj_kernel, n_normed_tiles=n_normed_cols // tn,
                          rope=rope, with_f=with_f),
        grid=(T // tm, N // tn),
        in_specs=in_specs,
        out_specs=out_specs,
        out_shape=out_shape,
        scratch_shapes=[pltpu.VMEM((tm, D), BF16)],
        compiler_params=_compiler_params(("parallel", "arbitrary")),
        name="inproj_rope" if rope else "inproj_fgate",
    )(*args)
    return res


def _outproj_kernel(o_ref, w_ref, x_ref, out_ref):
    out_ref[...] = x_ref[...] + _dot(o_ref[...], w_ref[...])


def _outproj(o_bf, w_bf, x2):
    T, D = x2.shape
    K = o_bf.shape[1]
    tm, tn = PROJ_TM, PROJ_TN
    return pl.pallas_call(
        _outproj_kernel,
        grid=(T // tm, D // tn),
        in_specs=[pl.BlockSpec((tm, K), lambda i, j: (i, 0)),
                  pl.BlockSpec((K, tn), lambda i, j: (0, j)),
                  pl.BlockSpec((tm, tn), lambda i, j: (i, j))],
        out_specs=pl.BlockSpec((tm, tn), lambda i, j: (i, j)),
        out_shape=jax.ShapeDtypeStruct((T, D), F32),
        compiler_params=_compiler_params(("parallel", "arbitrary")),
        name="outproj_residual",
    )(o_bf, w_bf, x2)


def _softmax_t(s_t, r0, mask_t):
    last = jnp.where(mask_t, s_t[r0:, :], MASKED)
    m = jnp.max(last, axis=0, keepdims=True)
    if r0 == 0:
        return jnp.exp2(last - m).astype(BF16)
    first = s_t[:r0, :]
    m = jnp.maximum(m, jnp.max(first, axis=0, keepdims=True))
    return jnp.concatenate([jnp.exp2(first - m), jnp.exp2(last - m)], axis=0).astype(BF16)


def _fill_vt(vt_scr, v):
    d = v.shape[1]
    vt_scr[:d, :] = v.T
    vt_scr[d:, :] = jnp.ones((vt_scr.shape[0] - d, vt_scr.shape[1]), BF16)


def _silu(g):
    return g * (1.0 / (1.0 + jnp.exp(-g)))


def _diff_attn_kernel(q_ref, k_ref, v_ref, g_ref, lq1_ref, lk1_ref, lq2_ref, lk2_ref, sg_ref,
                      out_ref, vt_scr, *, lambda_init):
    S = q_ref.shape[0]
    tq = ATTN_TQ
    dv = v_ref.shape[1]
    lam = (jnp.exp(jnp.sum(lq1_ref[...] * lk1_ref[...], axis=-1, keepdims=True))
           - jnp.exp(jnp.sum(lq2_ref[...] * lk2_ref[...], axis=-1, keepdims=True))
           + lambda_init)
    key_chunk = lax.broadcasted_iota(jnp.int32, (tq, tq), 0) // CHUNK
    qry_chunk = lax.broadcasted_iota(jnp.int32, (tq, tq), 1) // CHUNK
    mask_t = key_chunk <= qry_chunk
    sub_gain = sg_ref[...] * (1.0 - lambda_init)
    _fill_vt(vt_scr, v_ref[...])

    for qi in range(S // tq):
        r0 = qi * tq
        kv = r0 + tq
        s1 = _dot_nt(k_ref[:kv, :HEAD_DIM], q_ref[r0:kv, :HEAD_DIM])
        s2 = _dot_nt(k_ref[:kv, HEAD_DIM:], q_ref[r0:kv, HEAD_DIM:])
        o1 = _dot(vt_scr[:, :kv], _softmax_t(s1, r0, mask_t))
        o2 = _dot(vt_scr[:, :kv], _softmax_t(s2, r0, mask_t))
        w1 = 1.0 / o1[dv:dv + 1, :]
        w2 = lam * (1.0 / o2[dv:dv + 1, :])
        o = (o1[:dv, :] * w1 - o2[:dv, :] * w2).T
        o = o * lax.rsqrt(jnp.mean(o * o, axis=-1, keepdims=True) + EPS) * sub_gain
        o = o * _silu(g_ref[r0:kv, :].astype(F32))
        out_ref[r0:kv, :] = o.astype(BF16)


def _diff_attention(qkvg, B, S, lq1, lk1, lq2, lk2, sub_g, lambda_init):
    T = B * S
    width = 2 * HEAD_DIM
    heads = (qkvg.shape[1] // 4) // width
    vec = lambda a: a.reshape(1, -1)
    small = lambda n: pl.BlockSpec((1, n), lambda b, h: (0, 0))
    return pl.pallas_call(
        functools.partial(_diff_attn_kernel, lambda_init=lambda_init),
        grid=(B, heads),
        in_specs=[pl.BlockSpec((S, width), lambda b, h: (b, h)),
                  pl.BlockSpec((S, width), lambda b, h: (b, heads + h)),
                  pl.BlockSpec((S, width), lambda b, h: (b, 2 * heads + h)),
                  pl.BlockSpec((S, width), lambda b, h: (b, 3 * heads + h)),
                  small(HEAD_DIM), small(HEAD_DIM), small(HEAD_DIM), small(HEAD_DIM),
                  small(width)],
        out_specs=pl.BlockSpec((S, width), lambda b, h: (b, h)),
        out_shape=jax.ShapeDtypeStruct((T, heads * width), BF16),
        scratch_shapes=[pltpu.VMEM((width + BF16_ROWS, S), BF16)],
        compiler_params=_compiler_params(("parallel", "parallel")),
        name="diff_attention",
    )(qkvg, qkvg, qkvg, qkvg, vec(lq1), vec(lk1), vec(lq2), vec(lk2), vec(sub_g))


BIAS_PARTS = 3


def _cumsum_kernel(logf_ref, out_ref):
    S = logf_ref.shape[0]
    row = lax.broadcasted_iota(jnp.int32, (CUM_BLK, CUM_BLK), 0)
    col = lax.broadcasted_iota(jnp.int32, (CUM_BLK, CUM_BLK), 1)
    tri = (col <= row).astype(F32)
    src = lax.broadcasted_iota(jnp.int32, (HEAD_DIM, HEAD_DIM), 0)
    dst = lax.broadcasted_iota(jnp.int32, (HEAD_DIM, HEAD_DIM), 1)
    spread = [(dst == BIAS_PARTS * src + part).astype(BF16) for part in range(BIAS_PARTS)]
    off = jnp.zeros((1, HEAD_DIM), F32)
    for kb in range(S // CUM_BLK):
        rows = slice(kb * CUM_BLK, (kb + 1) * CUM_BLK)
        c = jnp.dot(tri, logf_ref[rows, :], preferred_element_type=F32,
                    precision=lax.Precision.HIGHEST) + off
        off = c[CUM_BLK - 1:CUM_BLK, :]
        rest = c * (-LOG2E)
        packed = jnp.zeros((CUM_BLK, HEAD_DIM), F32)
        for part in range(BIAS_PARTS):
            piece = rest.astype(BF16)
            rest = rest - piece.astype(F32)
            packed = packed + _dot(piece, spread[part])
        out_ref[rows, :] = packed.astype(BF16)


def _forget_bias(logf, B, S):
    T = B * S
    return pl.pallas_call(
        _cumsum_kernel,
        grid=(B,),
        in_specs=[pl.BlockSpec((S, HEAD_DIM), lambda b: (b, 0))],
        out_specs=pl.BlockSpec((S, HEAD_DIM), lambda b: (b, 0)),
        out_shape=jax.ShapeDtypeStruct((T, HEAD_DIM), BF16),
        compiler_params=_compiler_params(("parallel",)),
        name="forget_prefix_sum",
    )(logf)


def _fox_attn_kernel(q_ref, k_ref, v_ref, g_ref, kb_ref, out_ref, vt_scr):
    S = q_ref.shape[0]
    tq = ATTN_TQ
    heads_per_step = q_ref.shape[1] // HEAD_DIM
    key = lax.broadcasted_iota(jnp.int32, (tq, tq), 0)
    qry = lax.broadcasted_iota(jnp.int32, (tq, tq), 1)
    mask_t = key <= qry
    lane_head = lax.broadcasted_iota(jnp.int32, (tq, HEAD_DIM), 1) // BIAS_PARTS
    for hh in range(heads_per_step):
        sl = slice(hh * HEAD_DIM, (hh + 1) * HEAD_DIM)
        head = pl.program_id(1) * heads_per_step + hh
        pick = jnp.where(lane_head == head, 1.0, 0.0).astype(BF16)
        vt = vt_scr.at[hh]
        _fill_vt(vt, v_ref[:, sl])
        for qi in range(S // tq):
            r0 = qi * tq
            kv = r0 + tq
            k_aug = jnp.concatenate([k_ref[:kv, sl], kb_ref[:kv, :]], axis=1)
            q_aug = jnp.concatenate([q_ref[r0:kv, sl], pick], axis=1)
            s_t = _dot_nt(k_aug, q_aug)
            o = _dot(vt[:, :kv], _softmax_t(s_t, r0, mask_t))
            o = (o[:HEAD_DIM, :] * (1.0 / o[HEAD_DIM:HEAD_DIM + 1, :])).T
            o = o * _silu(g_ref[r0:kv, sl].astype(F32))
            out_ref[r0:kv, sl] = o.astype(BF16)


def _fox_attention(qkvg, kbias, B, S):
    T = B * S
    hps = 2
    width = hps * HEAD_DIM
    groups = (qkvg.shape[1] // 4) // width
    return pl.pallas_call(
        _fox_attn_kernel,
        grid=(B, groups),
        in_specs=[pl.BlockSpec((S, width), lambda b, h: (b, h)),
                  pl.BlockSpec((S, width), lambda b, h: (b, groups + h)),
                  pl.BlockSpec((S, width), lambda b, h: (b, 2 * groups + h)),
                  pl.BlockSpec((S, width), lambda b, h: (b, 3 * groups + h)),
                  pl.BlockSpec((S, HEAD_DIM), lambda b, h: (b, 0))],
        out_specs=pl.BlockSpec((S, width), lambda b, h: (b, h)),
        out_shape=jax.ShapeDtypeStruct((T, groups * width), BF16),
        scratch_shapes=[pltpu.VMEM((hps, HEAD_DIM + BF16_ROWS, S), BF16)],
        compiler_params=_compiler_params(("parallel", "parallel")),
        name="fox_attention",
    )(qkvg, qkvg, qkvg, qkvg, kbias)


def kernel(x, positions, a_norm, a_w_in, a_q_norm, a_k_norm, a_lambda_q1, a_lambda_k1,
           a_lambda_q2, a_lambda_k2, a_sub_norm, a_w_out, b_norm, b_w_in, b_f_bias,
           b_q_norm, b_k_norm, b_w_out):
    B, S, D = x.shape
    T = B * S
    width = a_w_out.shape[1]
    heads_per_width = width // HEAD_DIM
    q_scale = HEAD_DIM ** -0.5 * LOG2E
    x2 = x.reshape(T, D)

    def col_gain(qg, kg):
        return jnp.concatenate([jnp.tile(qg * q_scale, heads_per_width),
                                jnp.tile(kg, heads_per_width)]).reshape(1, 2 * width)

    lambda_init0 = 0.8 - 0.6 * math.exp(-0.3 * 0)
    tabs = _rope_tables(positions)
    (qkvg,) = _inproj(x2, a_norm[0], a_w_in[0].astype(BF16),
                      jnp.pad(col_gain(a_q_norm[0], a_k_norm[0]), ((0, 0), (0, 2 * width)),
                              constant_values=1.0),
                      2 * width, rope_tabs=tabs)
    o = _diff_attention(qkvg, B, S, a_lambda_q1[0], a_lambda_k1[0], a_lambda_q2[0],
                        a_lambda_k2[0], a_sub_norm[0], lambda_init0)
    x2 = _outproj(o, a_w_out[0].astype(BF16), x2)

    w1 = b_w_in[0]
    n_f = w1.shape[1] - 4 * width
    wf = jnp.pad(w1[:, 4 * width:], ((0, 0), (0, HEAD_DIM - n_f))).astype(BF16)
    fb = jnp.pad(b_f_bias[0], (0, HEAD_DIM - n_f)).reshape(1, HEAD_DIM)
    qkvg, logf = _inproj(x2, b_norm[0], w1[:, :4 * width].astype(BF16),
                         jnp.pad(col_gain(b_q_norm[0], b_k_norm[0]), ((0, 0), (0, 2 * width)),
                                 constant_values=1.0),
                         2 * width, wf_bf=wf, f_bias=fb)
    assert n_f * BIAS_PARTS <= HEAD_DIM
    o = _fox_attention(qkvg, _forget_bias(logf, B, S), B, S)
    x2 = _outproj(o, b_w_out[0].astype(BF16), x2)
    return x2.reshape(B, S, D)
```

```python
import functools
import math

import numpy as np
import jax
import jax.numpy as jnp
from jax import lax
from jax.experimental import pallas as pl
from jax.experimental.pallas import tpu as pltpu

F32 = jnp.float32
BF16 = jnp.bfloat16

EPS = 1e-6
CHUNK = 64
ROPE_THETA = 500000.0
HEAD_DIM = 128
BF16_ROWS = 16
ROT_DIM = HEAD_DIM // 4
ROT_HALF = ROT_DIM // 2
LOG2E = math.log2(math.e)
MASKED = -1e30

V7X_VMEM_LIMIT_BYTES = 56 * 1024 * 1024

PROJ_TM = 1024
PROJ_TN = 1024
OUT_TM = 512
ATTN_TQ = 512
CUM_BLK = 256


def _compiler_params(semantics):
    return pltpu.CompilerParams(dimension_semantics=semantics,
                                vmem_limit_bytes=V7X_VMEM_LIMIT_BYTES)


def _dot(a, b):
    return jnp.dot(a, b, preferred_element_type=F32)


def _dot_nt(a, b):
    return lax.dot_general(a, b, (((1,), (1,)), ((), ())), preferred_element_type=F32)


def _rope_tables_kernel(pos_ref, invf_ref, cos_ref, sin_ref):
    ang = invf_ref[...] * pos_ref[...]
    cos_ref[...] = jnp.cos(ang)
    sin_ref[...] = jnp.sin(ang)


def _rope_tables(positions):
    T = positions.size
    tb = 2048
    pos = positions.reshape(1, T).astype(F32)
    invf = (ROPE_THETA ** (-np.arange(ROT_HALF, dtype=np.float32) / ROT_HALF)).astype(np.float32)
    invf = jnp.asarray(invf.reshape(ROT_HALF, 1))
    tab = jax.ShapeDtypeStruct((ROT_HALF, T), F32)
    return pl.pallas_call(
        _rope_tables_kernel,
        grid=(T // tb,),
        in_specs=[pl.BlockSpec((1, tb), lambda i: (0, i)),
                  pl.BlockSpec((ROT_HALF, 1), lambda i: (0, 0))],
        out_specs=[pl.BlockSpec((ROT_HALF, tb), lambda i: (0, i))] * 2,
        out_shape=[tab, tab],
        compiler_params=_compiler_params(("parallel",)),
        name="rope_tables",
    )(pos, invf)


def _rmsnorm_to(h_ref, x_ref, ng_ref):
    xf = x_ref[...]
    ms = jnp.mean(xf * xf, axis=-1, keepdims=True)
    h_ref[...] = (xf * lax.rsqrt(ms + EPS) * ng_ref[...]).astype(BF16)


def _qk_proj_kernel(x_ref, ng_ref, w_ref, cg_ref, out_ref, h_ref):
    @pl.when(pl.program_id(1) == 0)
    def _():
        _rmsnorm_to(h_ref, x_ref, ng_ref)

    acc = _dot(h_ref[...], w_ref[...])
    for hh in range(acc.shape[1] // HEAD_DIM):
        sl = slice(hh * HEAD_DIM, (hh + 1) * HEAD_DIM)
        a = acc[:, sl]
        r = lax.rsqrt(jnp.mean(a * a, axis=-1, keepdims=True) + EPS)
        out_ref[:, sl] = (a * r * cg_ref[:, sl]).astype(BF16)


def _qk_proj(x2, norm_g, w_bf, col_gain):
    T, D = x2.shape
    N = col_gain.shape[1]
    tm, tn = PROJ_TM, PROJ_TN
    return pl.pallas_call(
        _qk_proj_kernel,
        grid=(T // tm, N // tn),
        in_specs=[pl.BlockSpec((tm, D), lambda i, j: (i, 0)),
                  pl.BlockSpec((1, D), lambda i, j: (0, 0)),
                  pl.BlockSpec((D, tn), lambda i, j: (0, j)),
                  pl.BlockSpec((1, tn), lambda i, j: (0, j))],
        out_specs=[pl.BlockSpec((tm, tn), lambda i, j: (i, j)),
                   pl.BlockSpec((tm, D), lambda i, j: (i, 0))],
        out_shape=[jax.ShapeDtypeStruct((T, N), BF16), jax.ShapeDtypeStruct((T, D), BF16)],
        compiler_params=_compiler_params(("parallel", "arbitrary")),
        name="qk_proj",
    )(x2, norm_g.reshape(1, D), w_bf, col_gain)


def _qk_proj_t_kernel(x_ref, ng_ref, wt_ref, g_ref, cos_ref, sin_ref, out_ref, h_ref):
    @pl.when(pl.program_id(1) == 0)
    def _():
        _rmsnorm_to(h_ref, x_ref, ng_ref)

    acc = _dot_nt(wt_ref[...], h_ref[...])
    tm = acc.shape[1]
    cos = cos_ref[...]
    sin = sin_ref[...]
    for hh in range(acc.shape[0] // HEAD_DIM):
        rows = slice(hh * HEAD_DIM, (hh + 1) * HEAD_DIM)
        a = acc[rows, :]
        r = lax.rsqrt(jnp.mean(a * a, axis=0, keepdims=True) + EPS)
        gain = jnp.concatenate([g_ref[rows, :]] * (tm // HEAD_DIM), axis=1)
        y = a * r * gain
        y1 = y[:ROT_HALF, :]
        y2 = y[ROT_HALF:ROT_DIM, :]
        y = jnp.concatenate([y1 * cos - y2 * sin, y2 * cos + y1 * sin, y[ROT_DIM:, :]], axis=0)
        out_ref[rows, :] = y.astype(BF16)


def _qk_proj_t(x2, norm_g, wt_bf, row_gain, rope_tabs):
    T, D = x2.shape
    N = row_gain.shape[0]
    tm, tn = PROJ_TM, PROJ_TN
    return pl.pallas_call(
        _qk_proj_t_kernel,
        grid=(T // tm, N // tn),
        in_specs=[pl.BlockSpec((tm, D), lambda i, j: (i, 0)),
                  pl.BlockSpec((1, D), lambda i, j: (0, 0)),
                  pl.BlockSpec((tn, D), lambda i, j: (j, 0)),
                  pl.BlockSpec((tn, HEAD_DIM), lambda i, j: (j, 0)),
                  pl.BlockSpec((ROT_HALF, tm), lambda i, j: (0, i)),
                  pl.BlockSpec((ROT_HALF, tm), lambda i, j: (0, i))],
        out_specs=[pl.BlockSpec((tn, tm), lambda i, j: (j, i)),
                   pl.BlockSpec((tm, D), lambda i, j: (i, 0))],
        out_shape=[jax.ShapeDtypeStruct((N, T), BF16), jax.ShapeDtypeStruct((T, D), BF16)],
        compiler_params=_compiler_params(("parallel", "arbitrary")),
        name="qk_proj_rope",
    )(x2, norm_g.reshape(1, D), wt_bf, row_gain, *rope_tabs)


def _vg_proj_kernel(*refs, with_f):
    if with_f:
        h_ref, w_ref, wf_ref, fb_ref, out_ref, logf_ref = refs

        @pl.when(pl.program_id(1) == 0)
        def _():
            f = _dot(h_ref[...], wf_ref[...]) + fb_ref[...]
            logf_ref[...] = jnp.minimum(f, 0.0) - jnp.log(1.0 + jnp.exp(-jnp.abs(f)))
    else:
        h_ref, w_ref, out_ref = refs
    out_ref[...] = _dot(h_ref[...], w_ref[...]).astype(BF16)


def _vg_proj(h, w_bf, col0, n_cols, wf_bf=None, f_bias=None):
    T, D = h.shape
    tm, tn = PROJ_TM, PROJ_TN
    j0 = col0 // tn
    with_f = wf_bf is not None
    in_specs = [pl.BlockSpec((tm, D), lambda i, j: (i, 0)),
                pl.BlockSpec((D, tn), lambda i, j: (0, j0 + j))]
    args = [h, w_bf]
    out_specs = [pl.BlockSpec((tm, tn), lambda i, j: (i, j))]
    out_shape = [jax.ShapeDtypeStruct((T, n_cols), BF16)]
    if with_f:
        in_specs += [pl.BlockSpec((D, HEAD_DIM), lambda i, j: (0, 0)),
                     pl.BlockSpec((1, HEAD_DIM), lambda i, j: (0, 0))]
        args += [wf_bf, f_bias]
        out_specs.append(pl.BlockSpec((tm, HEAD_DIM), lambda i, j: (i, 0)))
        out_shape.append(jax.ShapeDtypeStruct((T, HEAD_DIM), F32))
    return pl.pallas_call(
        functools.partial(_vg_proj_kernel, with_f=with_f),
        grid=(T // tm, n_cols // tn),
        in_specs=in_specs,
        out_specs=out_specs,
        out_shape=out_shape,
        compiler_params=_compiler_params(("parallel", "arbitrary")),
        name="vg_proj_fgate" if with_f else "vg_proj",
    )(*args)


def _outproj_kernel(o_ref, w_ref, x_ref, out_ref):
    out_ref[...] = x_ref[...] + _dot(o_ref[...], w_ref[...])


def _outproj(o_bf, w_bf, x2):
    T, D = x2.shape
    K = o_bf.shape[1]
    tm = OUT_TM
    return pl.pallas_call(
        _outproj_kernel,
        grid=(T // tm,),
        in_specs=[pl.BlockSpec((tm, K), lambda i: (i, 0)),
                  pl.BlockSpec((K, D), lambda i: (0, 0)),
                  pl.BlockSpec((tm, D), lambda i: (i, 0))],
        out_specs=pl.BlockSpec((tm, D), lambda i: (i, 0)),
        out_shape=jax.ShapeDtypeStruct((T, D), F32),
        compiler_params=_compiler_params(("parallel",)),
        name="outproj_residual",
    )(o_bf, w_bf, x2)


def _softmax_t(s_t, r0, mask_t):
    last = jnp.where(mask_t, s_t[r0:, :], MASKED)
    m = jnp.max(last, axis=0, keepdims=True)
    if r0 == 0:
        return jnp.exp2(last - m).astype(BF16)
    first = s_t[:r0, :]
    m = jnp.maximum(m, jnp.max(first, axis=0, keepdims=True))
    return jnp.concatenate([jnp.exp2(first - m), jnp.exp2(last - m)], axis=0).astype(BF16)


def _fill_vt(vt_scr, v):
    d = v.shape[1]
    vt_scr[:d, :] = v.T
    vt_scr[d:, :] = jnp.ones((vt_scr.shape[0] - d, vt_scr.shape[1]), BF16)


def _silu(g):
    return g * (1.0 / (1.0 + jnp.exp(-g)))


def _diff_attn_kernel(qt_ref, kt_ref, v_ref, g_ref, lq1_ref, lk1_ref, lq2_ref, lk2_ref, sg_ref,
                      out_ref, vt_scr, k_scr, *, lambda_init):
    S = v_ref.shape[0]
    tq = ATTN_TQ
    dv = v_ref.shape[1]
    lam = (jnp.exp(jnp.sum(lq1_ref[...] * lk1_ref[...], axis=-1, keepdims=True))
           - jnp.exp(jnp.sum(lq2_ref[...] * lk2_ref[...], axis=-1, keepdims=True))
           + lambda_init)
    key_chunk = lax.broadcasted_iota(jnp.int32, (tq, tq), 0) // CHUNK
    qry_chunk = lax.broadcasted_iota(jnp.int32, (tq, tq), 1) // CHUNK
    mask_t = key_chunk <= qry_chunk
    sub_gain = sg_ref[...] * (1.0 - lambda_init)
    _fill_vt(vt_scr, v_ref[...])
    k_scr[...] = kt_ref[...].T

    for qi in range(S // tq):
        r0 = qi * tq
        kv = r0 + tq
        s1 = _dot(k_scr[:kv, :HEAD_DIM], qt_ref[:HEAD_DIM, r0:kv])
        s2 = _dot(k_scr[:kv, HEAD_DIM:], qt_ref[HEAD_DIM:, r0:kv])
        o1 = _dot(vt_scr[:, :kv], _softmax_t(s1, r0, mask_t))
        o2 = _dot(vt_scr[:, :kv], _softmax_t(s2, r0, mask_t))
        w1 = 1.0 / o1[dv:dv + 1, :]
        w2 = lam * (1.0 / o2[dv:dv + 1, :])
        o = (o1[:dv, :] * w1 - o2[:dv, :] * w2).T
        o = o * lax.rsqrt(jnp.mean(o * o, axis=-1, keepdims=True) + EPS) * sub_gain
        o = o * _silu(g_ref[r0:kv, :].astype(F32))
        out_ref[r0:kv, :] = o.astype(BF16)


def _diff_attention(qk_t, vg, B, S, lq1, lk1, lq2, lk2, sub_g, lambda_init):
    T = B * S
    width = 2 * HEAD_DIM
    heads = (qk_t.shape[0] // 2) // width
    vec = lambda a: a.reshape(1, -1)
    small = lambda n: pl.BlockSpec((1, n), lambda b, h: (0, 0))
    return pl.pallas_call(
        functools.partial(_diff_attn_kernel, lambda_init=lambda_init),
        grid=(B, heads),
        in_specs=[pl.BlockSpec((width, S), lambda b, h: (h, b)),
                  pl.BlockSpec((width, S), lambda b, h: (heads + h, b)),
                  pl.BlockSpec((S, width), lambda b, h: (b, h)),
                  pl.BlockSpec((S, width), lambda b, h: (b, heads + h)),
                  small(HEAD_DIM), small(HEAD_DIM), small(HEAD_DIM), small(HEAD_DIM),
                  small(width)],
        out_specs=pl.BlockSpec((S, width), lambda b, h: (b, h)),
        out_shape=jax.ShapeDtypeStruct((T, heads * width), BF16),
        scratch_shapes=[pltpu.VMEM((width + BF16_ROWS, S), BF16),
                        pltpu.VMEM((S, width), BF16)],
        compiler_params=_compiler_params(("parallel", "parallel")),
        name="diff_attention",
    )(qk_t, qk_t, vg, vg, vec(lq1), vec(lk1), vec(lq2), vec(lk2), vec(sub_g))


BIAS_PARTS = 3


def _cumsum_kernel(logf_ref, out_ref):
    S = logf_ref.shape[0]
    row = lax.broadcasted_iota(jnp.int32, (CUM_BLK, CUM_BLK), 0)
    col = lax.broadcasted_iota(jnp.int32, (CUM_BLK, CUM_BLK), 1)
    tri = (col <= row).astype(F32)
    src = lax.broadcasted_iota(jnp.int32, (HEAD_DIM, HEAD_DIM), 0)
    dst = lax.broadcasted_iota(jnp.int32, (HEAD_DIM, HEAD_DIM), 1)
    spread = [(dst == BIAS_PARTS * src + part).astype(BF16) for part in range(BIAS_PARTS)]
    off = jnp.zeros((1, HEAD_DIM), F32)
    for kb in range(S // CUM_BLK):
        rows = slice(kb * CUM_BLK, (kb + 1) * CUM_BLK)
        c = jnp.dot(tri, logf_ref[rows, :], preferred_element_type=F32,
                    precision=lax.Precision.HIGHEST) + off
        off = c[CUM_BLK - 1:CUM_BLK, :]
        rest = c * (-LOG2E)
        packed = jnp.zeros((CUM_BLK, HEAD_DIM), F32)
        for part in range(BIAS_PARTS):
            piece = rest.astype(BF16)
            rest = rest - piece.astype(F32)
            packed = packed + _dot(piece, spread[part])
        out_ref[rows, :] = packed.astype(BF16)


def _forget_bias(logf, B, S):
    T = B * S
    return pl.pallas_call(
        _cumsum_kernel,
        grid=(B,),
        in_specs=[pl.BlockSpec((S, HEAD_DIM), lambda b: (b, 0))],
        out_specs=pl.BlockSpec((S, HEAD_DIM), lambda b: (b, 0)),
        out_shape=jax.ShapeDtypeStruct((T, HEAD_DIM), BF16),
        compiler_params=_compiler_params(("parallel",)),
        name="forget_prefix_sum",
    )(logf)


def _fox_attn_kernel(q_ref, k_ref, v_ref, g_ref, kb_ref, out_ref, vt_scr):
    S = q_ref.shape[0]
    tq = ATTN_TQ
    heads_per_step = q_ref.shape[1] // HEAD_DIM
    key = lax.broadcasted_iota(jnp.int32, (tq, tq), 0)
    qry = lax.broadcasted_iota(jnp.int32, (tq, tq), 1)
    mask_t = key <= qry
    lane_head = lax.broadcasted_iota(jnp.int32, (tq, HEAD_DIM), 1) // BIAS_PARTS
    for hh in range(heads_per_step):
        sl = slice(hh * HEAD_DIM, (hh + 1) * HEAD_DIM)
        head = pl.program_id(1) * heads_per_step + hh
        pick = jnp.where(lane_head == head, 1.0, 0.0).astype(BF16)
        vt = vt_scr.at[hh]
        _fill_vt(vt, v_ref[:, sl])
        for qi in range(S // tq):
            r0 = qi * tq
            kv = r0 + tq
            k_aug = jnp.concatenate([k_ref[:kv, sl], kb_ref[:kv, :]], axis=1)
            q_aug = jnp.concatenate([q_ref[r0:kv, sl], pick], axis=1)
            s_t = _dot_nt(k_aug, q_aug)
            o = _dot(vt[:, :kv], _softmax_t(s_t, r0, mask_t))
            o = (o[:HEAD_DIM, :] * (1.0 / o[HEAD_DIM:HEAD_DIM + 1, :])).T
            o = o * _silu(g_ref[r0:kv, sl].astype(F32))
            out_ref[r0:kv, sl] = o.astype(BF16)


def _fox_attention(qk, vg, kbias, B, S):
    T = B * S
    hps = 2
    width = hps * HEAD_DIM
    groups = (qk.shape[1] // 2) // width
    return pl.pallas_call(
        _fox_attn_kernel,
        grid=(B, groups),
        in_specs=[pl.BlockSpec((S, width), lambda b, h: (b, h)),
                  pl.BlockSpec((S, width), lambda b, h: (b, groups + h)),
                  pl.BlockSpec((S, width), lambda b, h: (b, h)),
                  pl.BlockSpec((S, width), lambda b, h: (b, groups + h)),
                  pl.BlockSpec((S, HEAD_DIM), lambda b, h: (b, 0))],
        out_specs=pl.BlockSpec((S, width), lambda b, h: (b, h)),
        out_shape=jax.ShapeDtypeStruct((T, groups * width), BF16),
        scratch_shapes=[pltpu.VMEM((hps, HEAD_DIM + BF16_ROWS, S), BF16)],
        compiler_params=_compiler_params(("parallel", "parallel")),
        name="fox_attention",
    )(qk, qk, vg, vg, kbias)


def kernel(x, positions, a_norm, a_w_in, a_q_norm, a_k_norm, a_lambda_q1, a_lambda_k1,
           a_lambda_q2, a_lambda_k2, a_sub_norm, a_w_out, b_norm, b_w_in, b_f_bias,
           b_q_norm, b_k_norm, b_w_out):
    B, S, D = x.shape
    T = B * S
    width = a_w_out.shape[1]
    heads_per_width = width // HEAD_DIM
    q_scale = HEAD_DIM ** -0.5 * LOG2E
    x2 = x.reshape(T, D)

    def col_gain(qg, kg):
        return jnp.concatenate([jnp.tile(qg * q_scale, heads_per_width),
                                jnp.tile(kg, heads_per_width)]).reshape(1, 2 * width)

    lambda_init0 = 0.8 - 0.6 * math.exp(-0.3 * 0)
    w0 = a_w_in[0].astype(BF16)
    row_gain = jnp.broadcast_to(col_gain(a_q_norm[0], a_k_norm[0]).reshape(2 * width, 1),
                                (2 * width, HEAD_DIM))
    qk_t, h = _qk_proj_t(x2, a_norm[0], w0[:, :2 * width].T, row_gain, _rope_tables(positions))
    (vg,) = _vg_proj(h, w0, 2 * width, 2 * width)
    o = _diff_attention(qk_t, vg, B, S, a_lambda_q1[0], a_lambda_k1[0], a_lambda_q2[0],
                        a_lambda_k2[0], a_sub_norm[0], lambda_init0)
    x2 = _outproj(o, a_w_out[0].astype(BF16), x2)

    w1 = b_w_in[0].astype(BF16)
    n_f = w1.shape[1] - 4 * width
    assert n_f * BIAS_PARTS <= HEAD_DIM
    wf = jnp.pad(w1[:, 4 * width:], ((0, 0), (0, HEAD_DIM - n_f)))
    fb = jnp.pad(b_f_bias[0], (0, HEAD_DIM - n_f)).reshape(1, HEAD_DIM)
    qk, h = _qk_proj(x2, b_norm[0], w1, col_gain(b_q_norm[0], b_k_norm[0]))
    vg, logf = _vg_proj(h, w1, 2 * width, 2 * width, wf_bf=wf, f_bias=fb)
    o = _fox_attention(qk, vg, _forget_bias(logf, B, S), B, S)
    x2 = _outproj(o, b_w_out[0].astype(BF16), x2)
    return x2.reshape(B, S, D)
```

```python
import functools
import math

import numpy as np
import jax
import jax.numpy as jnp
from jax import lax
from jax.experimental import pallas as pl
from jax.experimental.pallas import tpu as pltpu

F32 = jnp.float32
BF16 = jnp.bfloat16

EPS = 1e-6
CHUNK = 64
ROPE_THETA = 500000.0
HEAD_DIM = 128
BF16_ROWS = 16
ROT_DIM = HEAD_DIM // 4
ROT_HALF = ROT_DIM // 2
LOG2E = math.log2(math.e)
MASKED = -1e30

V7X_VMEM_LIMIT_BYTES = 56 * 1024 * 1024

PROJ_TM = 1024
PROJ_TN = 1024
OUT_TM = 512
ATTN_TQ = 512
CUM_BLK = 256
DIFF_HEADS_PER_STEP = 2
FOX_HEADS_PER_STEP = 4


def _compiler_params(semantics):
    return pltpu.CompilerParams(dimension_semantics=semantics,
                                vmem_limit_bytes=V7X_VMEM_LIMIT_BYTES)


def _dot(a, b):
    return jnp.dot(a, b, preferred_element_type=F32)


def _dot_nt(a, b):
    return lax.dot_general(a, b, (((1,), (1,)), ((), ())), preferred_element_type=F32)


def _rope_tables_kernel(pos_ref, invf_ref, cos_ref, sin_ref):
    ang = invf_ref[...] * pos_ref[...]
    cos_ref[...] = jnp.cos(ang)
    sin_ref[...] = jnp.sin(ang)


def _rope_tables(positions):
    T = positions.size
    tb = 2048
    pos = positions.reshape(1, T).astype(F32)
    invf = (ROPE_THETA ** (-np.arange(ROT_HALF, dtype=np.float32) / ROT_HALF)).astype(np.float32)
    invf = jnp.asarray(invf.reshape(ROT_HALF, 1))
    tab = jax.ShapeDtypeStruct((ROT_HALF, T), F32)
    return pl.pallas_call(
        _rope_tables_kernel,
        grid=(T // tb,),
        in_specs=[pl.BlockSpec((1, tb), lambda i: (0, i)),
                  pl.BlockSpec((ROT_HALF, 1), lambda i: (0, 0))],
        out_specs=[pl.BlockSpec((ROT_HALF, tb), lambda i: (0, i))] * 2,
        out_shape=[tab, tab],
        compiler_params=_compiler_params(("parallel",)),
        name="rope_tables",
    )(pos, invf)


def _rmsnorm_to(h_ref, x_ref, ng_ref):
    xf = x_ref[...]
    ms = jnp.mean(xf * xf, axis=-1, keepdims=True)
    h_ref[...] = (xf * lax.rsqrt(ms + EPS) * ng_ref[...]).astype(BF16)


def _qk_proj_kernel(x_ref, ng_ref, w_ref, cg_ref, out_ref, h_ref):
    @pl.when(pl.program_id(1) == 0)
    def _():
        _rmsnorm_to(h_ref, x_ref, ng_ref)

    acc = _dot(h_ref[...], w_ref[...])
    for hh in range(acc.shape[1] // HEAD_DIM):
        sl = slice(hh * HEAD_DIM, (hh + 1) * HEAD_DIM)
        a = acc[:, sl]
        r = lax.rsqrt(jnp.mean(a * a, axis=-1, keepdims=True) + EPS)
        out_ref[:, sl] = (a * r * cg_ref[:, sl]).astype(BF16)


def _qk_proj(x2, norm_g, w_bf, col_gain):
    T, D = x2.shape
    N = col_gain.shape[1]
    tm, tn = PROJ_TM, PROJ_TN
    return pl.pallas_call(
        _qk_proj_kernel,
        grid=(T // tm, N // tn),
        in_specs=[pl.BlockSpec((tm, D), lambda i, j: (i, 0)),
                  pl.BlockSpec((1, D), lambda i, j: (0, 0)),
                  pl.BlockSpec((D, tn), lambda i, j: (0, j)),
                  pl.BlockSpec((1, tn), lambda i, j: (0, j))],
        out_specs=[pl.BlockSpec((tm, tn), lambda i, j: (i, j)),
                   pl.BlockSpec((tm, D), lambda i, j: (i, 0))],
        out_shape=[jax.ShapeDtypeStruct((T, N), BF16), jax.ShapeDtypeStruct((T, D), BF16)],
        compiler_params=_compiler_params(("parallel", "arbitrary")),
        name="qk_proj",
    )(x2, norm_g.reshape(1, D), w_bf, col_gain)


def _qk_proj_t_kernel(x_ref, ng_ref, wt_ref, g_ref, cos_ref, sin_ref, out_ref, h_ref):
    @pl.when(pl.program_id(1) == 0)
    def _():
        _rmsnorm_to(h_ref, x_ref, ng_ref)

    acc = _dot_nt(wt_ref[...], h_ref[...])
    tm = acc.shape[1]
    cos = cos_ref[...]
    sin = sin_ref[...]
    for hh in range(acc.shape[0] // HEAD_DIM):
        rows = slice(hh * HEAD_DIM, (hh + 1) * HEAD_DIM)
        a = acc[rows, :]
        r = lax.rsqrt(jnp.mean(a * a, axis=0, keepdims=True) + EPS)
        gain = jnp.concatenate([g_ref[rows, :]] * (tm // HEAD_DIM), axis=1)
        y = a * r * gain
        y1 = y[:ROT_HALF, :]
        y2 = y[ROT_HALF:ROT_DIM, :]
        y = jnp.concatenate([y1 * cos - y2 * sin, y2 * cos + y1 * sin, y[ROT_DIM:, :]], axis=0)
        out_ref[rows, :] = y.astype(BF16)


def _qk_proj_t(x2, norm_g, wt_bf, row_gain, rope_tabs):
    T, D = x2.shape
    N = row_gain.shape[0]
    tm, tn = PROJ_TM, PROJ_TN
    return pl.pallas_call(
        _qk_proj_t_kernel,
        grid=(T // tm, N // tn),
        in_specs=[pl.BlockSpec((tm, D), lambda i, j: (i, 0)),
                  pl.BlockSpec((1, D), lambda i, j: (0, 0)),
                  pl.BlockSpec((tn, D), lambda i, j: (j, 0)),
                  pl.BlockSpec((tn, HEAD_DIM), lambda i, j: (j, 0)),
                  pl.BlockSpec((ROT_HALF, tm), lambda i, j: (0, i)),
                  pl.BlockSpec((ROT_HALF, tm), lambda i, j: (0, i))],
        out_specs=[pl.BlockSpec((tn, tm), lambda i, j: (j, i)),
                   pl.BlockSpec((tm, D), lambda i, j: (i, 0))],
        out_shape=[jax.ShapeDtypeStruct((N, T), BF16), jax.ShapeDtypeStruct((T, D), BF16)],
        compiler_params=_compiler_params(("parallel", "arbitrary")),
        name="qk_proj_rope",
    )(x2, norm_g.reshape(1, D), wt_bf, row_gain, *rope_tabs)


def _vg_proj_kernel(*refs, with_f):
    if with_f:
        h_ref, w_ref, wf_ref, fb_ref, out_ref, logf_ref = refs

        @pl.when(pl.program_id(1) == 0)
        def _():
            f = _dot(h_ref[...], wf_ref[...]) + fb_ref[...]
            logf_ref[...] = jnp.minimum(f, 0.0) - jnp.log(1.0 + jnp.exp(-jnp.abs(f)))
    else:
        h_ref, w_ref, out_ref = refs
    out_ref[...] = _dot(h_ref[...], w_ref[...]).astype(BF16)


def _vg_proj(h, w_bf, col0, n_cols, wf_bf=None, f_bias=None):
    T, D = h.shape
    tm, tn = PROJ_TM, PROJ_TN
    j0 = col0 // tn
    with_f = wf_bf is not None
    in_specs = [pl.BlockSpec((tm, D), lambda i, j: (i, 0)),
                pl.BlockSpec((D, tn), lambda i, j: (0, j0 + j))]
    args = [h, w_bf]
    out_specs = [pl.BlockSpec((tm, tn), lambda i, j: (i, j))]
    out_shape = [jax.ShapeDtypeStruct((T, n_cols), BF16)]
    if with_f:
        in_specs += [pl.BlockSpec((D, HEAD_DIM), lambda i, j: (0, 0)),
                     pl.BlockSpec((1, HEAD_DIM), lambda i, j: (0, 0))]
        args += [wf_bf, f_bias]
        out_specs.append(pl.BlockSpec((tm, HEAD_DIM), lambda i, j: (i, 0)))
        out_shape.append(jax.ShapeDtypeStruct((T, HEAD_DIM), F32))
    return pl.pallas_call(
        functools.partial(_vg_proj_kernel, with_f=with_f),
        grid=(T // tm, n_cols // tn),
        in_specs=in_specs,
        out_specs=out_specs,
        out_shape=out_shape,
        compiler_params=_compiler_params(("parallel", "arbitrary")),
        name="vg_proj_fgate" if with_f else "vg_proj",
    )(*args)


def _outproj_kernel(o_ref, w_ref, x_ref, out_ref):
    out_ref[...] = x_ref[...] + _dot(o_ref[...], w_ref[...])


def _outproj(o_bf, w_bf, x2):
    T, D = x2.shape
    K = o_bf.shape[1]
    tm = OUT_TM
    return pl.pallas_call(
        _outproj_kernel,
        grid=(T // tm,),
        in_specs=[pl.BlockSpec((tm, K), lambda i: (i, 0)),
                  pl.BlockSpec((K, D), lambda i: (0, 0)),
                  pl.BlockSpec((tm, D), lambda i: (i, 0))],
        out_specs=pl.BlockSpec((tm, D), lambda i: (i, 0)),
        out_shape=jax.ShapeDtypeStruct((T, D), F32),
        compiler_params=_compiler_params(("parallel",)),
        name="outproj_residual",
    )(o_bf, w_bf, x2)


def _softmax_t(s_t, r0, mask_t):
    last = jnp.where(mask_t, s_t[r0:, :], MASKED)
    m = jnp.max(last, axis=0, keepdims=True)
    if r0 == 0:
        return jnp.exp2(last - m).astype(BF16)
    first = s_t[:r0, :]
    m = jnp.maximum(m, jnp.max(first, axis=0, keepdims=True))
    return jnp.concatenate([jnp.exp2(first - m), jnp.exp2(last - m)], axis=0).astype(BF16)


def _fill_vt(vt_scr, v):
    d = v.shape[1]
    vt_scr[:d, :] = v.T
    vt_scr[d:, :] = jnp.ones((vt_scr.shape[0] - d, vt_scr.shape[1]), BF16)


def _silu(g):
    return g * (1.0 / (1.0 + jnp.exp(-g)))


def _diff_attn_kernel(qt_ref, kt_ref, v_ref, g_ref, lq1_ref, lk1_ref, lq2_ref, lk2_ref, sg_ref,
                      out_ref, vt_scr, k_scr, *, lambda_init):
    S = v_ref.shape[0]
    tq = ATTN_TQ
    dv = 2 * HEAD_DIM
    n_heads = v_ref.shape[1] // dv
    lam = (jnp.exp(jnp.sum(lq1_ref[...] * lk1_ref[...], axis=-1, keepdims=True))
           - jnp.exp(jnp.sum(lq2_ref[...] * lk2_ref[...], axis=-1, keepdims=True))
           + lambda_init)
    key_chunk = lax.broadcasted_iota(jnp.int32, (tq, tq), 0) // CHUNK
    qry_chunk = lax.broadcasted_iota(jnp.int32, (tq, tq), 1) // CHUNK
    mask_t = key_chunk <= qry_chunk
    sub_gain = sg_ref[...] * (1.0 - lambda_init)
    for hh in range(n_heads):
        _fill_vt(vt_scr.at[hh], v_ref[:, hh * dv:(hh + 1) * dv])
    k_scr[...] = kt_ref[...].T

    maps = [slice(m * HEAD_DIM, (m + 1) * HEAD_DIM) for m in range(2 * n_heads)]
    for qi in range(S // tq):
        r0 = qi * tq
        kv = r0 + tq
        s_ts = [_dot(k_scr[:kv, sl], qt_ref[sl, r0:kv]) for sl in maps]
        p_ts = [_softmax_t(s_t, r0, mask_t) for s_t in s_ts]
        os_ = [_dot(vt_scr[m // 2, :, :kv], p_t) for m, p_t in enumerate(p_ts)]
        for hh in range(n_heads):
            o1, o2 = os_[2 * hh], os_[2 * hh + 1]
            w1 = 1.0 / o1[dv:dv + 1, :]
            w2 = lam * (1.0 / o2[dv:dv + 1, :])
            o = (o1[:dv, :] * w1 - o2[:dv, :] * w2).T
            o = o * lax.rsqrt(jnp.mean(o * o, axis=-1, keepdims=True) + EPS) * sub_gain
            o = o * _silu(g_ref[r0:kv, hh * dv:(hh + 1) * dv].astype(F32))
            out_ref[r0:kv, hh * dv:(hh + 1) * dv] = o.astype(BF16)


def _diff_attention(qk_t, vg, B, S, lq1, lk1, lq2, lk2, sub_g, lambda_init):
    T = B * S
    hps = DIFF_HEADS_PER_STEP
    dv = 2 * HEAD_DIM
    width = hps * dv
    heads = (qk_t.shape[0] // 2) // width
    vec = lambda a: a.reshape(1, -1)
    small = lambda n: pl.BlockSpec((1, n), lambda b, h: (0, 0))
    return pl.pallas_call(
        functools.partial(_diff_attn_kernel, lambda_init=lambda_init),
        grid=(B, heads),
        in_specs=[pl.BlockSpec((width, S), lambda b, h: (h, b)),
                  pl.BlockSpec((width, S), lambda b, h: (heads + h, b)),
                  pl.BlockSpec((S, width), lambda b, h: (b, h)),
                  pl.BlockSpec((S, width), lambda b, h: (b, heads + h)),
                  small(HEAD_DIM), small(HEAD_DIM), small(HEAD_DIM), small(HEAD_DIM),
                  small(dv)],
        out_specs=pl.BlockSpec((S, width), lambda b, h: (b, h)),
        out_shape=jax.ShapeDtypeStruct((T, heads * width), BF16),
        scratch_shapes=[pltpu.VMEM((hps, dv + BF16_ROWS, S), BF16),
                        pltpu.VMEM((S, width), BF16)],
        compiler_params=_compiler_params(("parallel", "parallel")),
        name="diff_attention",
    )(qk_t, qk_t, vg, vg, vec(lq1), vec(lk1), vec(lq2), vec(lk2), vec(sub_g))


BIAS_PARTS = 3


def _cumsum_kernel(logf_ref, out_ref):
    S = logf_ref.shape[0]
    row = lax.broadcasted_iota(jnp.int32, (CUM_BLK, CUM_BLK), 0)
    col = lax.broadcasted_iota(jnp.int32, (CUM_BLK, CUM_BLK), 1)
    tri = (col <= row).astype(F32)
    src = lax.broadcasted_iota(jnp.int32, (HEAD_DIM, HEAD_DIM), 0)
    dst = lax.broadcasted_iota(jnp.int32, (HEAD_DIM, HEAD_DIM), 1)
    spread = [(dst == BIAS_PARTS * src + part).astype(BF16) for part in range(BIAS_PARTS)]
    off = jnp.zeros((1, HEAD_DIM), F32)
    for kb in range(S // CUM_BLK):
        rows = slice(kb * CUM_BLK, (kb + 1) * CUM_BLK)
        c = jnp.dot(tri, logf_ref[rows, :], preferred_element_type=F32,
                    precision=lax.Precision.HIGHEST) + off
        off = c[CUM_BLK - 1:CUM_BLK, :]
        rest = c * (-LOG2E)
        packed = jnp.zeros((CUM_BLK, HEAD_DIM), F32)
        for part in range(BIAS_PARTS):
            piece = rest.astype(BF16)
            rest = rest - piece.astype(F32)
            packed = packed + _dot(piece, spread[part])
        out_ref[rows, :] = packed.astype(BF16)


def _forget_bias(logf, B, S):
    T = B * S
    return pl.pallas_call(
        _cumsum_kernel,
        grid=(B,),
        in_specs=[pl.BlockSpec((S, HEAD_DIM), lambda b: (b, 0))],
        out_specs=pl.BlockSpec((S, HEAD_DIM), lambda b: (b, 0)),
        out_shape=jax.ShapeDtypeStruct((T, HEAD_DIM), BF16),
        compiler_params=_compiler_params(("parallel",)),
        name="forget_prefix_sum",
    )(logf)


def _fox_attn_kernel(q_ref, k_ref, v_ref, g_ref, kb_ref, out_ref, vt_scr):
    S = q_ref.shape[0]
    tq = ATTN_TQ
    heads_per_step = q_ref.shape[1] // HEAD_DIM
    key = lax.broadcasted_iota(jnp.int32, (tq, tq), 0)
    qry = lax.broadcasted_iota(jnp.int32, (tq, tq), 1)
    mask_t = key <= qry
    lane_head = lax.broadcasted_iota(jnp.int32, (tq, HEAD_DIM), 1) // BIAS_PARTS
    lanes = [slice(hh * HEAD_DIM, (hh + 1) * HEAD_DIM) for hh in range(heads_per_step)]
    picks = []
    for hh, sl in enumerate(lanes):
        head = pl.program_id(1) * heads_per_step + hh
        picks.append(jnp.where(lane_head == head, 1.0, 0.0).astype(BF16))
        _fill_vt(vt_scr.at[hh], v_ref[:, sl])
    for qi in range(S // tq):
        r0 = qi * tq
        kv = r0 + tq
        s_ts = [_dot_nt(jnp.concatenate([k_ref[:kv, sl], kb_ref[:kv, :]], axis=1),
                        jnp.concatenate([q_ref[r0:kv, sl], pick], axis=1))
                for sl, pick in zip(lanes, picks)]
        p_ts = [_softmax_t(s_t, r0, mask_t) for s_t in s_ts]
        os_ = [_dot(vt_scr[hh, :, :kv], p_t) for hh, p_t in enumerate(p_ts)]
        for sl, o in zip(lanes, os_):
            o = (o[:HEAD_DIM, :] * (1.0 / o[HEAD_DIM:HEAD_DIM + 1, :])).T
            o = o * _silu(g_ref[r0:kv, sl].astype(F32))
            out_ref[r0:kv, sl] = o.astype(BF16)


def _fox_attention(qk, vg, kbias, B, S):
    T = B * S
    hps = FOX_HEADS_PER_STEP
    width = hps * HEAD_DIM
    groups = (qk.shape[1] // 2) // width
    return pl.pallas_call(
        _fox_attn_kernel,
        grid=(B, groups),
        in_specs=[pl.BlockSpec((S, width), lambda b, h: (b, h)),
                  pl.BlockSpec((S, width), lambda b, h: (b, groups + h)),
                  pl.BlockSpec((S, width), lambda b, h: (b, h)),
                  pl.BlockSpec((S, width), lambda b, h: (b, groups + h)),
                  pl.BlockSpec((S, HEAD_DIM), lambda b, h: (b, 0))],
        out_specs=pl.BlockSpec((S, width), lambda b, h: (b, h)),
        out_shape=jax.ShapeDtypeStruct((T, groups * width), BF16),
        scratch_shapes=[pltpu.VMEM((hps, HEAD_DIM + BF16_ROWS, S), BF16)],
        compiler_params=_compiler_params(("parallel", "parallel")),
        name="fox_attention",
    )(qk, qk, vg, vg, kbias)


def kernel(x, positions, a_norm, a_w_in, a_q_norm, a_k_norm, a_lambda_q1, a_lambda_k1,
           a_lambda_q2, a_lambda_k2, a_sub_norm, a_w_out, b_norm, b_w_in, b_f_bias,
           b_q_norm, b_k_norm, b_w_out):
    B, S, D = x.shape
    T = B * S
    width = a_w_out.shape[1]
    heads_per_width = width // HEAD_DIM
    q_scale = HEAD_DIM ** -0.5 * LOG2E
    x2 = x.reshape(T, D)

    def col_gain(qg, kg):
        return jnp.concatenate([jnp.tile(qg * q_scale, heads_per_width),
                                jnp.tile(kg, heads_per_width)]).reshape(1, 2 * width)

    lambda_init0 = 0.8 - 0.6 * math.exp(-0.3 * 0)
    w0 = a_w_in[0].astype(BF16)
    row_gain = jnp.broadcast_to(col_gain(a_q_norm[0], a_k_norm[0]).reshape(2 * width, 1),
                                (2 * width, HEAD_DIM))
    qk_t, h = _qk_proj_t(x2, a_norm[0], w0[:, :2 * width].T, row_gain, _rope_tables(positions))
    (vg,) = _vg_proj(h, w0, 2 * width, 2 * width)
    o = _diff_attention(qk_t, vg, B, S, a_lambda_q1[0], a_lambda_k1[0], a_lambda_q2[0],
                        a_lambda_k2[0], a_sub_norm[0], lambda_init0)
    x2 = _outproj(o, a_w_out[0].astype(BF16), x2)

    w1 = b_w_in[0].astype(BF16)
    n_f = w1.shape[1] - 4 * width
    assert n_f * BIAS_PARTS <= HEAD_DIM
    wf = jnp.pad(w1[:, 4 * width:], ((0, 0), (0, HEAD_DIM - n_f)))
    fb = jnp.pad(b_f_bias[0], (0, HEAD_DIM - n_f)).reshape(1, HEAD_DIM)
    qk, h = _qk_proj(x2, b_norm[0], w1, col_gain(b_q_norm[0], b_k_norm[0]))
    vg, logf = _vg_proj(h, w1, 2 * width, 2 * width, wf_bf=wf, f_bias=fb)
    o = _fox_attention(qk, vg, _forget_bias(logf, B, S), B, S)
    x2 = _outproj(o, b_w_out[0].astype(BF16), x2)
    return x2.reshape(B, S, D)
```

```python
import functools
import math

import numpy as np
import jax
import jax.numpy as jnp
from jax import lax
from jax.experimental import pallas as pl
from jax.experimental.pallas import tpu as pltpu

F32 = jnp.float32
BF16 = jnp.bfloat16

EPS = 1e-6
CHUNK = 64
ROPE_THETA = 500000.0
HEAD_DIM = 128
BF16_ROWS = 16
ROT_DIM = HEAD_DIM // 4
ROT_HALF = ROT_DIM // 2
LOG2E = math.log2(math.e)
MASKED = -1e30

V7X_VMEM_LIMIT_BYTES = 56 * 1024 * 1024

PROJ_TM = 1024
PROJ_TN = 1024
OUT_TM = 512
ATTN_TQ = 512
CUM_BLK = 256
DIFF_HEADS_PER_STEP = 2
FOX_HEADS_PER_STEP = 4


def _compiler_params(semantics):
    return pltpu.CompilerParams(dimension_semantics=semantics,
                                vmem_limit_bytes=V7X_VMEM_LIMIT_BYTES)


def _dot(a, b):
    return jnp.dot(a, b, preferred_element_type=F32)


def _dot_nt(a, b):
    return lax.dot_general(a, b, (((1,), (1,)), ((), ())), preferred_element_type=F32)


def _rope_tables_kernel(pos_ref, invf_ref, cos_ref, sin_ref):
    ang = invf_ref[...] * pos_ref[...]
    cos_ref[...] = jnp.cos(ang)
    sin_ref[...] = jnp.sin(ang)


def _rope_tables(positions):
    T = positions.size
    tb = 2048
    pos = positions.reshape(1, T).astype(F32)
    invf = (ROPE_THETA ** (-np.arange(ROT_HALF, dtype=np.float32) / ROT_HALF)).astype(np.float32)
    invf = jnp.asarray(invf.reshape(ROT_HALF, 1))
    tab = jax.ShapeDtypeStruct((ROT_HALF, T), F32)
    return pl.pallas_call(
        _rope_tables_kernel,
        grid=(T // tb,),
        in_specs=[pl.BlockSpec((1, tb), lambda i: (0, i)),
                  pl.BlockSpec((ROT_HALF, 1), lambda i: (0, 0))],
        out_specs=[pl.BlockSpec((ROT_HALF, tb), lambda i: (0, i))] * 2,
        out_shape=[tab, tab],
        compiler_params=_compiler_params(("parallel",)),
        name="rope_tables",
    )(pos, invf)


def _rmsnorm_to(h_ref, x_ref, ng_ref):
    xf = x_ref[...]
    ms = jnp.mean(xf * xf, axis=-1, keepdims=True)
    h_ref[...] = (xf * lax.rsqrt(ms + EPS) * ng_ref[...]).astype(BF16)


def _qk_proj_kernel(h_ref, w_ref, cg_ref, out_ref):
    acc = _dot(h_ref[...], w_ref[...])
    for hh in range(acc.shape[1] // HEAD_DIM):
        sl = slice(hh * HEAD_DIM, (hh + 1) * HEAD_DIM)
        a = acc[:, sl]
        r = lax.rsqrt(jnp.mean(a * a, axis=-1, keepdims=True) + EPS)
        out_ref[:, sl] = (a * r * cg_ref[:, sl]).astype(BF16)


def _qk_proj(h, w_bf, col_gain):
    T, D = h.shape
    N = col_gain.shape[1]
    tm, tn = PROJ_TM, PROJ_TN
    return pl.pallas_call(
        _qk_proj_kernel,
        grid=(T // tm, N // tn),
        in_specs=[pl.BlockSpec((tm, D), lambda i, j: (i, 0)),
                  pl.BlockSpec((D, tn), lambda i, j: (0, j)),
                  pl.BlockSpec((1, tn), lambda i, j: (0, j))],
        out_specs=pl.BlockSpec((tm, tn), lambda i, j: (i, j)),
        out_shape=jax.ShapeDtypeStruct((T, N), BF16),
        compiler_params=_compiler_params(("parallel", "arbitrary")),
        name="qk_proj",
    )(h, w_bf, col_gain)


def _qk_proj_t_kernel(x_ref, ng_ref, wt_ref, g_ref, cos_ref, sin_ref, out_ref, h_ref):
    @pl.when(pl.program_id(1) == 0)
    def _():
        _rmsnorm_to(h_ref, x_ref, ng_ref)

    acc = _dot_nt(wt_ref[...], h_ref[...])
    tm = acc.shape[1]
    cos = cos_ref[...]
    sin = sin_ref[...]
    for hh in range(acc.shape[0] // HEAD_DIM):
        rows = slice(hh * HEAD_DIM, (hh + 1) * HEAD_DIM)
        a = acc[rows, :]
        r = lax.rsqrt(jnp.mean(a * a, axis=0, keepdims=True) + EPS)
        gain = jnp.concatenate([g_ref[rows, :]] * (tm // HEAD_DIM), axis=1)
        y = a * r * gain
        y1 = y[:ROT_HALF, :]
        y2 = y[ROT_HALF:ROT_DIM, :]
        y = jnp.concatenate([y1 * cos - y2 * sin, y2 * cos + y1 * sin, y[ROT_DIM:, :]], axis=0)
        out_ref[rows, :] = y.astype(BF16)


def _qk_proj_t(x2, norm_g, wt_bf, row_gain, rope_tabs):
    T, D = x2.shape
    N = row_gain.shape[0]
    tm, tn = PROJ_TM, PROJ_TN
    n_row_steps, n_col_steps = T // tm, N // tn
    assert n_col_steps >= 2

    def x_block(i, j):
        return (jnp.minimum(i + jnp.minimum(j, 1), n_row_steps - 1), 0)

    return pl.pallas_call(
        _qk_proj_t_kernel,
        grid=(n_row_steps, n_col_steps),
        in_specs=[pl.BlockSpec((tm, D), x_block),
                  pl.BlockSpec((1, D), lambda i, j: (0, 0)),
                  pl.BlockSpec((tn, D), lambda i, j: (j, 0)),
                  pl.BlockSpec((tn, HEAD_DIM), lambda i, j: (j, 0)),
                  pl.BlockSpec((ROT_HALF, tm), lambda i, j: (0, i)),
                  pl.BlockSpec((ROT_HALF, tm), lambda i, j: (0, i))],
        out_specs=[pl.BlockSpec((tn, tm), lambda i, j: (j, i)),
                   pl.BlockSpec((tm, D), lambda i, j: (i, 0))],
        out_shape=[jax.ShapeDtypeStruct((N, T), BF16), jax.ShapeDtypeStruct((T, D), BF16)],
        compiler_params=_compiler_params(("arbitrary", "arbitrary")),
        name="qk_proj_rope",
    )(x2, norm_g.reshape(1, D), wt_bf, row_gain, *rope_tabs)


def _vg_proj_kernel(*refs, with_f):
    if with_f:
        h_ref, w_ref, wf_ref, fb_ref, out_ref, logf_ref = refs

        @pl.when(pl.program_id(1) == 0)
        def _():
            f = _dot(h_ref[...], wf_ref[...]) + fb_ref[...]
            logf_ref[...] = jnp.minimum(f, 0.0) - jnp.log(1.0 + jnp.exp(-jnp.abs(f)))
    else:
        h_ref, w_ref, out_ref = refs
    out_ref[...] = _dot(h_ref[...], w_ref[...]).astype(BF16)


def _vg_proj(h, w_bf, col0, n_cols, wf_bf=None, f_bias=None):
    T, D = h.shape
    tm, tn = PROJ_TM, PROJ_TN
    j0 = col0 // tn
    with_f = wf_bf is not None
    in_specs = [pl.BlockSpec((tm, D), lambda i, j: (i, 0)),
                pl.BlockSpec((D, tn), lambda i, j: (0, j0 + j))]
    args = [h, w_bf]
    out_specs = [pl.BlockSpec((tm, tn), lambda i, j: (i, j))]
    out_shape = [jax.ShapeDtypeStruct((T, n_cols), BF16)]
    if with_f:
        in_specs += [pl.BlockSpec((D, HEAD_DIM), lambda i, j: (0, 0)),
                     pl.BlockSpec((1, HEAD_DIM), lambda i, j: (0, 0))]
        args += [wf_bf, f_bias]
        out_specs.append(pl.BlockSpec((tm, HEAD_DIM), lambda i, j: (i, 0)))
        out_shape.append(jax.ShapeDtypeStruct((T, HEAD_DIM), F32))
    return pl.pallas_call(
        functools.partial(_vg_proj_kernel, with_f=with_f),
        grid=(T // tm, n_cols // tn),
        in_specs=in_specs,
        out_specs=out_specs,
        out_shape=out_shape,
        compiler_params=_compiler_params(("parallel", "arbitrary")),
        name="vg_proj_fgate" if with_f else "vg_proj",
    )(*args)


def _outproj_kernel(*refs, with_next_norm):
    if with_next_norm:
        o_ref, w_ref, x_ref, ng_ref, out_ref, h_ref = refs
    else:
        o_ref, w_ref, x_ref, out_ref = refs
    y = x_ref[...] + _dot(o_ref[...], w_ref[...])
    out_ref[...] = y
    if with_next_norm:
        ms = jnp.mean(y * y, axis=-1, keepdims=True)
        h_ref[...] = (y * lax.rsqrt(ms + EPS) * ng_ref[...]).astype(BF16)


def _outproj(o_bf, w_bf, x2, next_norm_g=None):
    T, D = x2.shape
    K = o_bf.shape[1]
    tm = OUT_TM
    with_next_norm = next_norm_g is not None
    row_block = pl.BlockSpec((tm, D), lambda i: (i, 0))
    in_specs = [pl.BlockSpec((tm, K), lambda i: (i, 0)),
                pl.BlockSpec((K, D), lambda i: (0, 0)),
                row_block]
    args = [o_bf, w_bf, x2]
    out_specs = [row_block]
    out_shape = [jax.ShapeDtypeStruct((T, D), F32)]
    if with_next_norm:
        in_specs.append(pl.BlockSpec((1, D), lambda i: (0, 0)))
        args.append(next_norm_g.reshape(1, D))
        out_specs.append(row_block)
        out_shape.append(jax.ShapeDtypeStruct((T, D), BF16))
    return pl.pallas_call(
        functools.partial(_outproj_kernel, with_next_norm=with_next_norm),
        grid=(T // tm,),
        in_specs=in_specs,
        out_specs=out_specs,
        out_shape=out_shape,
        compiler_params=_compiler_params(("parallel",)),
        name="outproj_residual_norm" if with_next_norm else "outproj_residual",
    )(*args)


def _softmax_t(s_t, r0, mask_t):
    last = jnp.where(mask_t, s_t[r0:, :], MASKED)
    m = jnp.max(last, axis=0, keepdims=True)
    if r0 == 0:
        return jnp.exp2(last - m).astype(BF16)
    first = s_t[:r0, :]
    m = jnp.maximum(m, jnp.max(first, axis=0, keepdims=True))
    return jnp.concatenate([jnp.exp2(first - m), jnp.exp2(last - m)], axis=0).astype(BF16)


def _fill_vt(vt_scr, v):
    d = v.shape[1]
    vt_scr[:d, :] = v.T
    vt_scr[d:, :] = jnp.ones((vt_scr.shape[0] - d, vt_scr.shape[1]), BF16)


def _silu(g):
    return g * (1.0 / (1.0 + jnp.exp(-g)))


def _diff_attn_kernel(qt_ref, kt_ref, v_ref, g_ref, lq1_ref, lk1_ref, lq2_ref, lk2_ref, sg_ref,
                      out_ref, vt_scr, k_scr, *, lambda_init):
    S = v_ref.shape[0]
    tq = ATTN_TQ
    dv = 2 * HEAD_DIM
    n_heads = v_ref.shape[1] // dv
    lam = (jnp.exp(jnp.sum(lq1_ref[...] * lk1_ref[...], axis=-1, keepdims=True))
           - jnp.exp(jnp.sum(lq2_ref[...] * lk2_ref[...], axis=-1, keepdims=True))
           + lambda_init)
    key_chunk = lax.broadcasted_iota(jnp.int32, (tq, tq), 0) // CHUNK
    qry_chunk = lax.broadcasted_iota(jnp.int32, (tq, tq), 1) // CHUNK
    mask_t = key_chunk <= qry_chunk
    sub_gain = sg_ref[...] * (1.0 - lambda_init)
    for hh in range(n_heads):
        _fill_vt(vt_scr.at[hh], v_ref[:, hh * dv:(hh + 1) * dv])
    k_scr[...] = kt_ref[...].T

    maps = [slice(m * HEAD_DIM, (m + 1) * HEAD_DIM) for m in range(2 * n_heads)]
    for qi in range(S // tq):
        r0 = qi * tq
        kv = r0 + tq
        s_ts = [_dot(k_scr[:kv, sl], qt_ref[sl, r0:kv]) for sl in maps]
        p_ts = [_softmax_t(s_t, r0, mask_t) for s_t in s_ts]
        os_ = [_dot(vt_scr[m // 2, :, :kv], p_t) for m, p_t in enumerate(p_ts)]
        for hh in range(n_heads):
            o1, o2 = os_[2 * hh], os_[2 * hh + 1]
            w1 = 1.0 / o1[dv:dv + 1, :]
            w2 = lam * (1.0 / o2[dv:dv + 1, :])
            o = (o1[:dv, :] * w1 - o2[:dv, :] * w2).T
            o = o * lax.rsqrt(jnp.mean(o * o, axis=-1, keepdims=True) + EPS) * sub_gain
            o = o * _silu(g_ref[r0:kv, hh * dv:(hh + 1) * dv].astype(F32))
            out_ref[r0:kv, hh * dv:(hh + 1) * dv] = o.astype(BF16)


def _diff_attention(qk_t, vg, B, S, lq1, lk1, lq2, lk2, sub_g, lambda_init):
    T = B * S
    hps = DIFF_HEADS_PER_STEP
    dv = 2 * HEAD_DIM
    width = hps * dv
    heads = (qk_t.shape[0] // 2) // width
    vec = lambda a: a.reshape(1, -1)
    small = lambda n: pl.BlockSpec((1, n), lambda b, h: (0, 0))
    return pl.pallas_call(
        functools.partial(_diff_attn_kernel, lambda_init=lambda_init),
        grid=(B, heads),
        in_specs=[pl.BlockSpec((width, S), lambda b, h: (h, b)),
                  pl.BlockSpec((width, S), lambda b, h: (heads + h, b)),
                  pl.BlockSpec((S, width), lambda b, h: (b, h)),
                  pl.BlockSpec((S, width), lambda b, h: (b, heads + h)),
                  small(HEAD_DIM), small(HEAD_DIM), small(HEAD_DIM), small(HEAD_DIM),
                  small(dv)],
        out_specs=pl.BlockSpec((S, width), lambda b, h: (b, h)),
        out_shape=jax.ShapeDtypeStruct((T, heads * width), BF16),
        scratch_shapes=[pltpu.VMEM((hps, dv + BF16_ROWS, S), BF16),
                        pltpu.VMEM((S, width), BF16)],
        compiler_params=_compiler_params(("parallel", "parallel")),
        name="diff_attention",
    )(qk_t, qk_t, vg, vg, vec(lq1), vec(lk1), vec(lq2), vec(lk2), vec(sub_g))


BIAS_PARTS = 3


def _cumsum_kernel(logf_ref, out_ref):
    S = logf_ref.shape[0]
    row = lax.broadcasted_iota(jnp.int32, (CUM_BLK, CUM_BLK), 0)
    col = lax.broadcasted_iota(jnp.int32, (CUM_BLK, CUM_BLK), 1)
    tri = (col <= row).astype(F32)
    src = lax.broadcasted_iota(jnp.int32, (HEAD_DIM, HEAD_DIM), 0)
    dst = lax.broadcasted_iota(jnp.int32, (HEAD_DIM, HEAD_DIM), 1)
    spread = [(dst == BIAS_PARTS * src + part).astype(BF16) for part in range(BIAS_PARTS)]
    off = jnp.zeros((1, HEAD_DIM), F32)
    for kb in range(S // CUM_BLK):
        rows = slice(kb * CUM_BLK, (kb + 1) * CUM_BLK)
        c = jnp.dot(tri, logf_ref[rows, :], preferred_element_type=F32,
                    precision=lax.Precision.HIGHEST) + off
        off = c[CUM_BLK - 1:CUM_BLK, :]
        rest = c * (-LOG2E)
        packed = jnp.zeros((CUM_BLK, HEAD_DIM), F32)
        for part in range(BIAS_PARTS):
            piece = rest.astype(BF16)
            rest = rest - piece.astype(F32)
            packed = packed + _dot(piece, spread[part])
        out_ref[rows, :] = packed.astype(BF16)


def _forget_bias(logf, B, S):
    T = B * S
    return pl.pallas_call(
        _cumsum_kernel,
        grid=(B,),
        in_specs=[pl.BlockSpec((S, HEAD_DIM), lambda b: (b, 0))],
        out_specs=pl.BlockSpec((S, HEAD_DIM), lambda b: (b, 0)),
        out_shape=jax.ShapeDtypeStruct((T, HEAD_DIM), BF16),
        compiler_params=_compiler_params(("parallel",)),
        name="forget_prefix_sum",
    )(logf)


def _fox_attn_kernel(q_ref, k_ref, v_ref, g_ref, kb_ref, out_ref, vt_scr):
    S = q_ref.shape[0]
    tq = ATTN_TQ
    heads_per_step = q_ref.shape[1] // HEAD_DIM
    key = lax.broadcasted_iota(jnp.int32, (tq, tq), 0)
    qry = lax.broadcasted_iota(jnp.int32, (tq, tq), 1)
    mask_t = key <= qry
    lane_head = lax.broadcasted_iota(jnp.int32, (tq, HEAD_DIM), 1) // BIAS_PARTS
    lanes = [slice(hh * HEAD_DIM, (hh + 1) * HEAD_DIM) for hh in range(heads_per_step)]
    picks = []
    for hh, sl in enumerate(lanes):
        head = pl.program_id(1) * heads_per_step + hh
        picks.append(jnp.where(lane_head == head, 1.0, 0.0).astype(BF16))
        _fill_vt(vt_scr.at[hh], v_ref[:, sl])
    for qi in range(S // tq):
        r0 = qi * tq
        kv = r0 + tq
        s_ts = [_dot_nt(jnp.concatenate([k_ref[:kv, sl], kb_ref[:kv, :]], axis=1),
                        jnp.concatenate([q_ref[r0:kv, sl], pick], axis=1))
                for sl, pick in zip(lanes, picks)]
        p_ts = [_softmax_t(s_t, r0, mask_t) for s_t in s_ts]
        os_ = [_dot(vt_scr[hh, :, :kv], p_t) for hh, p_t in enumerate(p_ts)]
        for sl, o in zip(lanes, os_):
            o = (o[:HEAD_DIM, :] * (1.0 / o[HEAD_DIM:HEAD_DIM + 1, :])).T
            o = o * _silu(g_ref[r0:kv, sl].astype(F32))
            out_ref[r0:kv, sl] = o.astype(BF16)


def _fox_attention(qk, vg, kbias, B, S):
    T = B * S
    hps = FOX_HEADS_PER_STEP
    width = hps * HEAD_DIM
    groups = (qk.shape[1] // 2) // width
    return pl.pallas_call(
        _fox_attn_kernel,
        grid=(B, groups),
        in_specs=[pl.BlockSpec((S, width), lambda b, h: (b, h)),
                  pl.BlockSpec((S, width), lambda b, h: (b, groups + h)),
                  pl.BlockSpec((S, width), lambda b, h: (b, h)),
                  pl.BlockSpec((S, width), lambda b, h: (b, groups + h)),
                  pl.BlockSpec((S, HEAD_DIM), lambda b, h: (b, 0))],
        out_specs=pl.BlockSpec((S, width), lambda b, h: (b, h)),
        out_shape=jax.ShapeDtypeStruct((T, groups * width), BF16),
        scratch_shapes=[pltpu.VMEM((hps, HEAD_DIM + BF16_ROWS, S), BF16)],
        compiler_params=_compiler_params(("parallel", "parallel")),
        name="fox_attention",
    )(qk, qk, vg, vg, kbias)


def kernel(x, positions, a_norm, a_w_in, a_q_norm, a_k_norm, a_lambda_q1, a_lambda_k1,
           a_lambda_q2, a_lambda_k2, a_sub_norm, a_w_out, b_norm, b_w_in, b_f_bias,
           b_q_norm, b_k_norm, b_w_out):
    B, S, D = x.shape
    T = B * S
    width = a_w_out.shape[1]
    heads_per_width = width // HEAD_DIM
    q_scale = HEAD_DIM ** -0.5 * LOG2E
    x2 = x.reshape(T, D)

    def col_gain(qg, kg):
        return jnp.concatenate([jnp.tile(qg * q_scale, heads_per_width),
                                jnp.tile(kg, heads_per_width)]).reshape(1, 2 * width)

    lambda_init0 = 0.8 - 0.6 * math.exp(-0.3 * 0)
    w0 = a_w_in[0].astype(BF16)
    row_gain = jnp.broadcast_to(col_gain(a_q_norm[0], a_k_norm[0]).reshape(2 * width, 1),
                                (2 * width, HEAD_DIM))
    qk_t, h = _qk_proj_t(x2, a_norm[0], w0[:, :2 * width].T, row_gain, _rope_tables(positions))
    (vg,) = _vg_proj(h, w0, 2 * width, 2 * width)
    o = _diff_attention(qk_t, vg, B, S, a_lambda_q1[0], a_lambda_k1[0], a_lambda_q2[0],
                        a_lambda_k2[0], a_sub_norm[0], lambda_init0)
    x2, h = _outproj(o, a_w_out[0].astype(BF16), x2, next_norm_g=b_norm[0])

    w1 = b_w_in[0].astype(BF16)
    n_f = w1.shape[1] - 4 * width
    assert n_f * BIAS_PARTS <= HEAD_DIM
    wf = jnp.pad(w1[:, 4 * width:], ((0, 0), (0, HEAD_DIM - n_f)))
    fb = jnp.pad(b_f_bias[0], (0, HEAD_DIM - n_f)).reshape(1, HEAD_DIM)
    qk = _qk_proj(h, w1, col_gain(b_q_norm[0], b_k_norm[0]))
    vg, logf = _vg_proj(h, w1, 2 * width, 2 * width, wf_bf=wf, f_bias=fb)
    o = _fox_attention(qk, vg, _forget_bias(logf, B, S), B, S)
    (x2,) = _outproj(o, b_w_out[0].astype(BF16), x2)
    return x2.reshape(B, S, D)
```

```python
import functools
import math

import numpy as np
import jax
import jax.numpy as jnp
from jax import lax
from jax.experimental import pallas as pl
from jax.experimental.pallas import tpu as pltpu

F32 = jnp.float32
BF16 = jnp.bfloat16

EPS = 1e-6
CHUNK = 64
ROPE_THETA = 500000.0
HEAD_DIM = 128
BF16_ROWS = 16
ROT_DIM = HEAD_DIM // 4
ROT_HALF = ROT_DIM // 2
LOG2E = math.log2(math.e)
MASKED = -1e30

V7X_VMEM_LIMIT_BYTES = 56 * 1024 * 1024

PROJ_TM = 1024
PROJ_TN = 1024
OUT_TM = 512
ATTN_TQ = 512
CUM_BLK = 256
CAST_TN = 512
DIFF_HEADS_PER_STEP = 2
FOX_HEADS_PER_STEP = 4


def _compiler_params(semantics):
    return pltpu.CompilerParams(dimension_semantics=semantics,
                                vmem_limit_bytes=V7X_VMEM_LIMIT_BYTES)


def _dot(a, b):
    return jnp.dot(a, b, preferred_element_type=F32)


def _dot_nt(a, b):
    return lax.dot_general(a, b, (((1,), (1,)), ((), ())), preferred_element_type=F32)


def _cast_kernel(w_ref, out_ref, *, transpose):
    w = w_ref[...]
    out_ref[...] = (w.T if transpose else w).astype(BF16)


def _cast_weight(w, col0, n_cols, transpose=False):
    K = w.shape[0]
    tc = CAST_TN
    j0 = col0 // tc
    if transpose:
        out_spec = pl.BlockSpec((tc, K), lambda j: (j, 0))
        out_shape = jax.ShapeDtypeStruct((n_cols, K), BF16)
    else:
        out_spec = pl.BlockSpec((K, tc), lambda j: (0, j))
        out_shape = jax.ShapeDtypeStruct((K, n_cols), BF16)
    return pl.pallas_call(
        functools.partial(_cast_kernel, transpose=transpose),
        grid=(n_cols // tc,),
        in_specs=[pl.BlockSpec((K, tc), lambda j: (0, j0 + j))],
        out_specs=out_spec,
        out_shape=out_shape,
        compiler_params=_compiler_params(("parallel",)),
        name="cast_weight_t" if transpose else "cast_weight",
    )(w)


def _rope_tables_kernel(pos_ref, invf_ref, cos_ref, sin_ref):
    ang = invf_ref[...] * pos_ref[...]
    cos_ref[...] = jnp.cos(ang)
    sin_ref[...] = jnp.sin(ang)


def _rope_tables(positions):
    T = positions.size
    tb = 2048
    pos = positions.reshape(1, T).astype(F32)
    invf = (ROPE_THETA ** (-np.arange(ROT_HALF, dtype=np.float32) / ROT_HALF)).astype(np.float32)
    invf = jnp.asarray(invf.reshape(ROT_HALF, 1))
    tab = jax.ShapeDtypeStruct((ROT_HALF, T), F32)
    return pl.pallas_call(
        _rope_tables_kernel,
        grid=(T // tb,),
        in_specs=[pl.BlockSpec((1, tb), lambda i: (0, i)),
                  pl.BlockSpec((ROT_HALF, 1), lambda i: (0, 0))],
        out_specs=[pl.BlockSpec((ROT_HALF, tb), lambda i: (0, i))] * 2,
        out_shape=[tab, tab],
        compiler_params=_compiler_params(("parallel",)),
        name="rope_tables",
    )(pos, invf)


def _rmsnorm_to(h_ref, x_ref, ng_ref):
    xf = x_ref[...]
    ms = jnp.mean(xf * xf, axis=-1, keepdims=True)
    h_ref[...] = (xf * lax.rsqrt(ms + EPS) * ng_ref[...]).astype(BF16)


def _qk_proj_kernel(h_ref, w_ref, cg_ref, out_ref):
    acc = _dot(h_ref[...], w_ref[...])
    for hh in range(acc.shape[1] // HEAD_DIM):
        sl = slice(hh * HEAD_DIM, (hh + 1) * HEAD_DIM)
        a = acc[:, sl]
        r = lax.rsqrt(jnp.mean(a * a, axis=-1, keepdims=True) + EPS)
        out_ref[:, sl] = (a * r * cg_ref[:, sl]).astype(BF16)


def _qk_proj(h, w_bf, col_gain):
    T, D = h.shape
    N = col_gain.shape[1]
    tm, tn = PROJ_TM, PROJ_TN
    return pl.pallas_call(
        _qk_proj_kernel,
        grid=(T // tm, N // tn),
        in_specs=[pl.BlockSpec((tm, D), lambda i, j: (i, 0)),
                  pl.BlockSpec((D, tn), lambda i, j: (0, j)),
                  pl.BlockSpec((1, tn), lambda i, j: (0, j))],
        out_specs=pl.BlockSpec((tm, tn), lambda i, j: (i, j)),
        out_shape=jax.ShapeDtypeStruct((T, N), BF16),
        compiler_params=_compiler_params(("parallel", "arbitrary")),
        name="qk_proj",
    )(h, w_bf, col_gain)


def _qk_proj_t_kernel(x_ref, ng_ref, wt_ref, g_ref, cos_ref, sin_ref, out_ref, h_ref):
    @pl.when(pl.program_id(1) == 0)
    def _():
        _rmsnorm_to(h_ref, x_ref, ng_ref)

    acc = _dot_nt(wt_ref[...], h_ref[...])
    tm = acc.shape[1]
    cos = cos_ref[...]
    sin = sin_ref[...]
    for hh in range(acc.shape[0] // HEAD_DIM):
        rows = slice(hh * HEAD_DIM, (hh + 1) * HEAD_DIM)
        a = acc[rows, :]
        r = lax.rsqrt(jnp.mean(a * a, axis=0, keepdims=True) + EPS)
        gain = jnp.concatenate([g_ref[rows, :]] * (tm // HEAD_DIM), axis=1)
        y = a * r * gain
        y1 = y[:ROT_HALF, :]
        y2 = y[ROT_HALF:ROT_DIM, :]
        y = jnp.concatenate([y1 * cos - y2 * sin, y2 * cos + y1 * sin, y[ROT_DIM:, :]], axis=0)
        out_ref[rows, :] = y.astype(BF16)


def _qk_proj_t(x2, norm_g, wt_bf, row_gain, rope_tabs):
    T, D = x2.shape
    N = row_gain.shape[0]
    tm, tn = PROJ_TM, PROJ_TN
    n_row_steps, n_col_steps = T // tm, N // tn
    assert n_col_steps >= 2

    def x_block(i, j):
        return (jnp.minimum(i + jnp.minimum(j, 1), n_row_steps - 1), 0)

    return pl.pallas_call(
        _qk_proj_t_kernel,
        grid=(n_row_steps, n_col_steps),
        in_specs=[pl.BlockSpec((tm, D), x_block),
                  pl.BlockSpec((1, D), lambda i, j: (0, 0)),
                  pl.BlockSpec((tn, D), lambda i, j: (j, 0)),
                  pl.BlockSpec((tn, HEAD_DIM), lambda i, j: (j, 0)),
                  pl.BlockSpec((ROT_HALF, tm), lambda i, j: (0, i)),
                  pl.BlockSpec((ROT_HALF, tm), lambda i, j: (0, i))],
        out_specs=[pl.BlockSpec((tn, tm), lambda i, j: (j, i)),
                   pl.BlockSpec((tm, D), lambda i, j: (i, 0))],
        out_shape=[jax.ShapeDtypeStruct((N, T), BF16), jax.ShapeDtypeStruct((T, D), BF16)],
        compiler_params=_compiler_params(("arbitrary", "arbitrary")),
        name="qk_proj_rope",
    )(x2, norm_g.reshape(1, D), wt_bf, row_gain, *rope_tabs)


def _vg_proj_kernel(*refs, with_f):
    if with_f:
        h_ref, w_ref, wf_ref, fb_ref, out_ref, logf_ref = refs

        @pl.when(pl.program_id(1) == 0)
        def _():
            f = _dot(h_ref[...], wf_ref[...]) + fb_ref[...]
            logf_ref[...] = jnp.minimum(f, 0.0) - jnp.log(1.0 + jnp.exp(-jnp.abs(f)))
    else:
        h_ref, w_ref, out_ref = refs
    out_ref[...] = _dot(h_ref[...], w_ref[...]).astype(BF16)


def _vg_proj(h, w_bf, col0, n_cols, wf_bf=None, f_bias=None):
    T, D = h.shape
    tm, tn = PROJ_TM, PROJ_TN
    j0 = col0 // tn
    with_f = wf_bf is not None
    in_specs = [pl.BlockSpec((tm, D), lambda i, j: (i, 0)),
                pl.BlockSpec((D, tn), lambda i, j: (0, j0 + j))]
    args = [h, w_bf]
    out_specs = [pl.BlockSpec((tm, tn), lambda i, j: (i, j))]
    out_shape = [jax.ShapeDtypeStruct((T, n_cols), BF16)]
    if with_f:
        in_specs += [pl.BlockSpec((D, HEAD_DIM), lambda i, j: (0, 0)),
                     pl.BlockSpec((1, HEAD_DIM), lambda i, j: (0, 0))]
        args += [wf_bf, f_bias]
        out_specs.append(pl.BlockSpec((tm, HEAD_DIM), lambda i, j: (i, 0)))
        out_shape.append(jax.ShapeDtypeStruct((T, HEAD_DIM), F32))
    return pl.pallas_call(
        functools.partial(_vg_proj_kernel, with_f=with_f),
        grid=(T // tm, n_cols // tn),
        in_specs=in_specs,
        out_specs=out_specs,
        out_shape=out_shape,
        compiler_params=_compiler_params(("parallel", "arbitrary")),
        name="vg_proj_fgate" if with_f else "vg_proj",
    )(*args)


def _outproj_kernel(*refs, with_next_norm):
    if with_next_norm:
        o_ref, w_ref, x_ref, ng_ref, out_ref, h_ref = refs
    else:
        o_ref, w_ref, x_ref, out_ref = refs
    y = x_ref[...] + _dot(o_ref[...], w_ref[...])
    out_ref[...] = y
    if with_next_norm:
        ms = jnp.mean(y * y, axis=-1, keepdims=True)
        h_ref[...] = (y * lax.rsqrt(ms + EPS) * ng_ref[...]).astype(BF16)


def _outproj(o_bf, w_bf, x2, next_norm_g=None):
    T, D = x2.shape
    K = o_bf.shape[1]
    tm = OUT_TM
    with_next_norm = next_norm_g is not None
    row_block = pl.BlockSpec((tm, D), lambda i: (i, 0))
    in_specs = [pl.BlockSpec((tm, K), lambda i: (i, 0)),
                pl.BlockSpec((K, D), lambda i: (0, 0)),
                row_block]
    args = [o_bf, w_bf, x2]
    out_specs = [row_block]
    out_shape = [jax.ShapeDtypeStruct((T, D), F32)]
    if with_next_norm:
        in_specs.append(pl.BlockSpec((1, D), lambda i: (0, 0)))
        args.append(next_norm_g.reshape(1, D))
        out_specs.append(row_block)
        out_shape.append(jax.ShapeDtypeStruct((T, D), BF16))
    return pl.pallas_call(
        functools.partial(_outproj_kernel, with_next_norm=with_next_norm),
        grid=(T // tm,),
        in_specs=in_specs,
        out_specs=out_specs,
        out_shape=out_shape,
        compiler_params=_compiler_params(("parallel",)),
        name="outproj_residual_norm" if with_next_norm else "outproj_residual",
    )(*args)


def _softmax_t(s_t, r0, mask_t):
    last = jnp.where(mask_t, s_t[r0:, :], MASKED)
    m = jnp.max(last, axis=0, keepdims=True)
    if r0 == 0:
        return jnp.exp2(last - m).astype(BF16)
    first = s_t[:r0, :]
    m = jnp.maximum(m, jnp.max(first, axis=0, keepdims=True))
    return jnp.concatenate([jnp.exp2(first - m), jnp.exp2(last - m)], axis=0).astype(BF16)


def _fill_vt(vt_scr, v):
    d = v.shape[1]
    vt_scr[:d, :] = v.T
    vt_scr[d:, :] = jnp.ones((vt_scr.shape[0] - d, vt_scr.shape[1]), BF16)


def _silu(g):
    return g * (1.0 / (1.0 + jnp.exp(-g)))


def _diff_attn_kernel(qt_ref, kt_ref, v_ref, g_ref, lq1_ref, lk1_ref, lq2_ref, lk2_ref, sg_ref,
                      out_ref, vt_scr, k_scr, *, lambda_init):
    S = v_ref.shape[0]
    tq = ATTN_TQ
    dv = 2 * HEAD_DIM
    n_heads = v_ref.shape[1] // dv
    lam = (jnp.exp(jnp.sum(lq1_ref[...] * lk1_ref[...], axis=-1, keepdims=True))
           - jnp.exp(jnp.sum(lq2_ref[...] * lk2_ref[...], axis=-1, keepdims=True))
           + lambda_init)
    key_chunk = lax.broadcasted_iota(jnp.int32, (tq, tq), 0) // CHUNK
    qry_chunk = lax.broadcasted_iota(jnp.int32, (tq, tq), 1) // CHUNK
    mask_t = key_chunk <= qry_chunk
    sub_gain = sg_ref[...] * (1.0 - lambda_init)
    for hh in range(n_heads):
        _fill_vt(vt_scr.at[hh], v_ref[:, hh * dv:(hh + 1) * dv])
    k_scr[...] = kt_ref[...].T

    maps = [slice(m * HEAD_DIM, (m + 1) * HEAD_DIM) for m in range(2 * n_heads)]
    for qi in range(S // tq):
        r0 = qi * tq
        kv = r0 + tq
        s_ts = [_dot(k_scr[:kv, sl], qt_ref[sl, r0:kv]) for sl in maps]
        p_ts = [_softmax_t(s_t, r0, mask_t) for s_t in s_ts]
        os_ = [_dot(vt_scr[m // 2, :, :kv], p_t) for m, p_t in enumerate(p_ts)]
        for hh in range(n_heads):
            o1, o2 = os_[2 * hh], os_[2 * hh + 1]
            w1 = 1.0 / o1[dv:dv + 1, :]
            w2 = lam * (1.0 / o2[dv:dv + 1, :])
            o = (o1[:dv, :] * w1 - o2[:dv, :] * w2).T
            o = o * lax.rsqrt(jnp.mean(o * o, axis=-1, keepdims=True) + EPS) * sub_gain
            o = o * _silu(g_ref[r0:kv, hh * dv:(hh + 1) * dv].astype(F32))
            out_ref[r0:kv, hh * dv:(hh + 1) * dv] = o.astype(BF16)


def _diff_attention(qk_t, vg, B, S, lq1, lk1, lq2, lk2, sub_g, lambda_init):
    T = B * S
    hps = DIFF_HEADS_PER_STEP
    dv = 2 * HEAD_DIM
    width = hps * dv
    heads = (qk_t.shape[0] // 2) // width
    vec = lambda a: a.reshape(1, -1)
    small = lambda n: pl.BlockSpec((1, n), lambda b, h: (0, 0))
    return pl.pallas_call(
        functools.partial(_diff_attn_kernel, lambda_init=lambda_init),
        grid=(B, heads),
        in_specs=[pl.BlockSpec((width, S), lambda b, h: (h, b)),
                  pl.BlockSpec((width, S), lambda b, h: (heads + h, b)),
                  pl.BlockSpec((S, width), lambda b, h: (b, h)),
                  pl.BlockSpec((S, width), lambda b, h: (b, heads + h)),
                  small(HEAD_DIM), small(HEAD_DIM), small(HEAD_DIM), small(HEAD_DIM),
                  small(dv)],
        out_specs=pl.BlockSpec((S, width), lambda b, h: (b, h)),
        out_shape=jax.ShapeDtypeStruct((T, heads * width), BF16),
        scratch_shapes=[pltpu.VMEM((hps, dv + BF16_ROWS, S), BF16),
                        pltpu.VMEM((S, width), BF16)],
        compiler_params=_compiler_params(("parallel", "parallel")),
        name="diff_attention",
    )(qk_t, qk_t, vg, vg, vec(lq1), vec(lk1), vec(lq2), vec(lk2), vec(sub_g))


BIAS_PARTS = 3


def _cumsum_kernel(logf_ref, out_ref):
    S = logf_ref.shape[0]
    row = lax.broadcasted_iota(jnp.int32, (CUM_BLK, CUM_BLK), 0)
    col = lax.broadcasted_iota(jnp.int32, (CUM_BLK, CUM_BLK), 1)
    tri = (col <= row).astype(F32)
    src = lax.broadcasted_iota(jnp.int32, (HEAD_DIM, HEAD_DIM), 0)
    dst = lax.broadcasted_iota(jnp.int32, (HEAD_DIM, HEAD_DIM), 1)
    spread = [(dst == BIAS_PARTS * src + part).astype(BF16) for part in range(BIAS_PARTS)]
    off = jnp.zeros((1, HEAD_DIM), F32)
    for kb in range(S // CUM_BLK):
        rows = slice(kb * CUM_BLK, (kb + 1) * CUM_BLK)
        c = jnp.dot(tri, logf_ref[rows, :], preferred_element_type=F32,
                    precision=lax.Precision.HIGHEST) + off
        off = c[CUM_BLK - 1:CUM_BLK, :]
        rest = c * (-LOG2E)
        packed = jnp.zeros((CUM_BLK, HEAD_DIM), F32)
        for part in range(BIAS_PARTS):
            piece = rest.astype(BF16)
            rest = rest - piece.astype(F32)
            packed = packed + _dot(piece, spread[part])
        out_ref[rows, :] = packed.astype(BF16)


def _forget_bias(logf, B, S):
    T = B * S
    return pl.pallas_call(
        _cumsum_kernel,
        grid=(B,),
        in_specs=[pl.BlockSpec((S, HEAD_DIM), lambda b: (b, 0))],
        out_specs=pl.BlockSpec((S, HEAD_DIM), lambda b: (b, 0)),
        out_shape=jax.ShapeDtypeStruct((T, HEAD_DIM), BF16),
        compiler_params=_compiler_params(("parallel",)),
        name="forget_prefix_sum",
    )(logf)


def _fox_attn_kernel(q_ref, k_ref, v_ref, g_ref, kb_ref, out_ref, vt_scr):
    S = q_ref.shape[0]
    tq = ATTN_TQ
    heads_per_step = q_ref.shape[1] // HEAD_DIM
    key = lax.broadcasted_iota(jnp.int32, (tq, tq), 0)
    qry = lax.broadcasted_iota(jnp.int32, (tq, tq), 1)
    mask_t = key <= qry
    lane_head = lax.broadcasted_iota(jnp.int32, (tq, HEAD_DIM), 1) // BIAS_PARTS
    lanes = [slice(hh * HEAD_DIM, (hh + 1) * HEAD_DIM) for hh in range(heads_per_step)]
    picks = []
    for hh, sl in enumerate(lanes):
        head = pl.program_id(1) * heads_per_step + hh
        picks.append(jnp.where(lane_head == head, 1.0, 0.0).astype(BF16))
        _fill_vt(vt_scr.at[hh], v_ref[:, sl])
    for qi in range(S // tq):
        r0 = qi * tq
        kv = r0 + tq
        s_ts = [_dot_nt(jnp.concatenate([k_ref[:kv, sl], kb_ref[:kv, :]], axis=1),
                        jnp.concatenate([q_ref[r0:kv, sl], pick], axis=1))
                for sl, pick in zip(lanes, picks)]
        p_ts = [_softmax_t(s_t, r0, mask_t) for s_t in s_ts]
        os_ = [_dot(vt_scr[hh, :, :kv], p_t) for hh, p_t in enumerate(p_ts)]
        for sl, o in zip(lanes, os_):
            o = (o[:HEAD_DIM, :] * (1.0 / o[HEAD_DIM:HEAD_DIM + 1, :])).T
            o = o * _silu(g_ref[r0:kv, sl].astype(F32))
            out_ref[r0:kv, sl] = o.astype(BF16)


def _fox_attention(qk, vg, kbias, B, S):
    T = B * S
    hps = FOX_HEADS_PER_STEP
    width = hps * HEAD_DIM
    groups = (qk.shape[1] // 2) // width
    return pl.pallas_call(
        _fox_attn_kernel,
        grid=(B, groups),
        in_specs=[pl.BlockSpec((S, width), lambda b, h: (b, h)),
                  pl.BlockSpec((S, width), lambda b, h: (b, groups + h)),
                  pl.BlockSpec((S, width), lambda b, h: (b, h)),
                  pl.BlockSpec((S, width), lambda b, h: (b, groups + h)),
                  pl.BlockSpec((S, HEAD_DIM), lambda b, h: (b, 0))],
        out_specs=pl.BlockSpec((S, width), lambda b, h: (b, h)),
        out_shape=jax.ShapeDtypeStruct((T, groups * width), BF16),
        scratch_shapes=[pltpu.VMEM((hps, HEAD_DIM + BF16_ROWS, S), BF16)],
        compiler_params=_compiler_params(("parallel", "parallel")),
        name="fox_attention",
    )(qk, qk, vg, vg, kbias)


def kernel(x, positions, a_norm, a_w_in, a_q_norm, a_k_norm, a_lambda_q1, a_lambda_k1,
           a_lambda_q2, a_lambda_k2, a_sub_norm, a_w_out, b_norm, b_w_in, b_f_bias,
           b_q_norm, b_k_norm, b_w_out):
    B, S, D = x.shape
    T = B * S
    width = a_w_out.shape[1]
    heads_per_width = width // HEAD_DIM
    q_scale = HEAD_DIM ** -0.5 * LOG2E
    x2 = x.reshape(T, D)

    def col_gain(qg, kg):
        return jnp.concatenate([jnp.tile(qg * q_scale, heads_per_width),
                                jnp.tile(kg, heads_per_width)]).reshape(1, 2 * width)

    lambda_init0 = 0.8 - 0.6 * math.exp(-0.3 * 0)
    row_gain = jnp.broadcast_to(col_gain(a_q_norm[0], a_k_norm[0]).reshape(2 * width, 1),
                                (2 * width, HEAD_DIM))
    qk_t, h = _qk_proj_t(x2, a_norm[0], _cast_weight(a_w_in[0], 0, 2 * width, transpose=True),
                         row_gain, _rope_tables(positions))
    (vg,) = _vg_proj(h, _cast_weight(a_w_in[0], 2 * width, 2 * width), 0, 2 * width)
    o = _diff_attention(qk_t, vg, B, S, a_lambda_q1[0], a_lambda_k1[0], a_lambda_q2[0],
                        a_lambda_k2[0], a_sub_norm[0], lambda_init0)
    x2, h = _outproj(o, a_w_out[0].astype(BF16), x2, next_norm_g=b_norm[0])

    w1 = _cast_weight(b_w_in[0], 0, 4 * width)
    n_f = b_w_in.shape[2] - 4 * width
    assert n_f * BIAS_PARTS <= HEAD_DIM
    wf = jnp.pad(b_w_in[0][:, 4 * width:], ((0, 0), (0, HEAD_DIM - n_f))).astype(BF16)
    fb = jnp.pad(b_f_bias[0], (0, HEAD_DIM - n_f)).reshape(1, HEAD_DIM)
    qk = _qk_proj(h, w1, col_gain(b_q_norm[0], b_k_norm[0]))
    vg, logf = _vg_proj(h, w1, 2 * width, 2 * width, wf_bf=wf, f_bias=fb)
    o = _fox_attention(qk, vg, _forget_bias(logf, B, S), B, S)
    (x2,) = _outproj(o, b_w_out[0].astype(BF16), x2)
    return x2.reshape(B, S, D)
```

```python
import functools
import math

import numpy as np
import jax
import jax.numpy as jnp
from jax import lax
from jax.experimental import pallas as pl
from jax.experimental.pallas import tpu as pltpu

F32 = jnp.float32
BF16 = jnp.bfloat16

EPS = 1e-6
CHUNK = 64
ROPE_THETA = 500000.0
HEAD_DIM = 128
BF16_ROWS = 16
ROT_DIM = HEAD_DIM // 4
ROT_HALF = ROT_DIM // 2
LOG2E = math.log2(math.e)
MASKED = -1e30

V7X_VMEM_LIMIT_BYTES = 56 * 1024 * 1024

PROJ_TM = 1024
PROJ_TN = 1024
PROJ_TN_FROM_H = 2048
OUT_TM = 512
ATTN_TQ = 512
CUM_BLK = 256
CAST_TN = 512
DIFF_HEADS_PER_STEP = 2
FOX_HEADS_PER_STEP = 4


def _compiler_params(semantics):
    return pltpu.CompilerParams(dimension_semantics=semantics,
                                vmem_limit_bytes=V7X_VMEM_LIMIT_BYTES)


def _dot(a, b):
    return jnp.dot(a, b, preferred_element_type=F32)


def _dot_nt(a, b):
    return lax.dot_general(a, b, (((1,), (1,)), ((), ())), preferred_element_type=F32)


def _cast_kernel(w_ref, out_ref, *, flip, col_axis, n_valid):
    w = w_ref[...]
    if n_valid is not None:
        col = (pl.program_id(0) * w.shape[col_axis]
               + lax.broadcasted_iota(jnp.int32, w.shape, col_axis))
        w = jnp.where(col < n_valid, w, 0.0)
    out_ref[...] = (w.T if flip else w).astype(BF16)


def _cast_weight(w, col0, n_cols, transpose_out=False, w_is_transposed=False):
    N, K = w.shape if w_is_transposed else w.shape[::-1]
    tc = CAST_TN
    j0 = col0 // tc
    n_cols = pl.cdiv(n_cols, tc) * tc
    n_valid = N - col0 if col0 + n_cols > N else None
    if w_is_transposed:
        in_spec = pl.BlockSpec((tc, K), lambda j: (j0 + j, 0))
    else:
        in_spec = pl.BlockSpec((K, tc), lambda j: (0, j0 + j))
    if transpose_out:
        out_spec = pl.BlockSpec((tc, K), lambda j: (j, 0))
        out_shape = jax.ShapeDtypeStruct((n_cols, K), BF16)
    else:
        out_spec = pl.BlockSpec((K, tc), lambda j: (0, j))
        out_shape = jax.ShapeDtypeStruct((K, n_cols), BF16)
    flip = transpose_out != w_is_transposed
    return pl.pallas_call(
        functools.partial(_cast_kernel, flip=flip, col_axis=0 if w_is_transposed else 1,
                          n_valid=n_valid),
        grid=(n_cols // tc,),
        in_specs=[in_spec],
        out_specs=out_spec,
        out_shape=out_shape,
        compiler_params=_compiler_params(("parallel",)),
        name="cast_weight_flip" if flip else "cast_weight",
    )(w)


def _rope_tables_kernel(pos_ref, invf_ref, cos_ref, sin_ref):
    ang = invf_ref[...] * pos_ref[...]
    cos_ref[...] = jnp.cos(ang)
    sin_ref[...] = jnp.sin(ang)


def _rope_tables(positions):
    T = positions.size
    tb = 2048
    pos = positions.reshape(1, T).astype(F32)
    invf = (ROPE_THETA ** (-np.arange(ROT_HALF, dtype=np.float32) / ROT_HALF)).astype(np.float32)
    invf = jnp.asarray(invf.reshape(ROT_HALF, 1))
    tab = jax.ShapeDtypeStruct((ROT_HALF, T), F32)
    return pl.pallas_call(
        _rope_tables_kernel,
        grid=(T // tb,),
        in_specs=[pl.BlockSpec((1, tb), lambda i: (0, i)),
                  pl.BlockSpec((ROT_HALF, 1), lambda i: (0, 0))],
        out_specs=[pl.BlockSpec((ROT_HALF, tb), lambda i: (0, i))] * 2,
        out_shape=[tab, tab],
        compiler_params=_compiler_params(("parallel",)),
        name="rope_tables",
    )(pos, invf)


def _rmsnorm_to(h_ref, x_ref, ng_ref):
    xf = x_ref[...]
    ms = jnp.mean(xf * xf, axis=-1, keepdims=True)
    h_ref[...] = (xf * lax.rsqrt(ms + EPS) * ng_ref[...]).astype(BF16)


def _qk_proj_kernel(h_ref, w_ref, cg_ref, out_ref):
    acc = _dot(h_ref[...], w_ref[...])
    for hh in range(acc.shape[1] // HEAD_DIM):
        sl = slice(hh * HEAD_DIM, (hh + 1) * HEAD_DIM)
        a = acc[:, sl]
        r = lax.rsqrt(jnp.mean(a * a, axis=-1, keepdims=True) + EPS)
        out_ref[:, sl] = (a * r * cg_ref[:, sl]).astype(BF16)


def _qk_proj(h, w_bf, col_gain):
    T, D = h.shape
    N = col_gain.shape[1]
    tm, tn = PROJ_TM, PROJ_TN_FROM_H
    return pl.pallas_call(
        _qk_proj_kernel,
        grid=(T // tm, N // tn),
        in_specs=[pl.BlockSpec((tm, D), lambda i, j: (i, 0)),
                  pl.BlockSpec((D, tn), lambda i, j: (0, j)),
                  pl.BlockSpec((1, tn), lambda i, j: (0, j))],
        out_specs=pl.BlockSpec((tm, tn), lambda i, j: (i, j)),
        out_shape=jax.ShapeDtypeStruct((T, N), BF16),
        compiler_params=_compiler_params(("parallel", "arbitrary")),
        name="qk_proj",
    )(h, w_bf, col_gain)


def _qk_proj_t_kernel(x_ref, ng_ref, wt_ref, g_ref, cos_ref, sin_ref, out_ref, h_ref):
    @pl.when(pl.program_id(1) == 0)
    def _():
        _rmsnorm_to(h_ref, x_ref, ng_ref)

    acc = _dot_nt(wt_ref[...], h_ref[...])
    tm = acc.shape[1]
    cos = cos_ref[...]
    sin = sin_ref[...]
    for hh in range(acc.shape[0] // HEAD_DIM):
        rows = slice(hh * HEAD_DIM, (hh + 1) * HEAD_DIM)
        a = acc[rows, :]
        r = lax.rsqrt(jnp.mean(a * a, axis=0, keepdims=True) + EPS)
        gain = jnp.concatenate([g_ref[rows, :]] * (tm // HEAD_DIM), axis=1)
        y = a * r * gain
        y1 = y[:ROT_HALF, :]
        y2 = y[ROT_HALF:ROT_DIM, :]
        y = jnp.concatenate([y1 * cos - y2 * sin, y2 * cos + y1 * sin, y[ROT_DIM:, :]], axis=0)
        out_ref[rows, :] = y.astype(BF16)


def _qk_proj_t(x2, norm_g, wt_bf, row_gain, rope_tabs):
    T, D = x2.shape
    N = row_gain.shape[0]
    tm, tn = PROJ_TM, PROJ_TN
    n_row_steps, n_col_steps = T // tm, N // tn
    assert n_col_steps >= 2

    def x_block(i, j):
        return (jnp.minimum(i + jnp.minimum(j, 1), n_row_steps - 1), 0)

    return pl.pallas_call(
        _qk_proj_t_kernel,
        grid=(n_row_steps, n_col_steps),
        in_specs=[pl.BlockSpec((tm, D), x_block),
                  pl.BlockSpec((1, D), lambda i, j: (0, 0)),
                  pl.BlockSpec((tn, D), lambda i, j: (j, 0)),
                  pl.BlockSpec((tn, HEAD_DIM), lambda i, j: (j, 0)),
                  pl.BlockSpec((ROT_HALF, tm), lambda i, j: (0, i)),
                  pl.BlockSpec((ROT_HALF, tm), lambda i, j: (0, i))],
        out_specs=[pl.BlockSpec((tn, tm), lambda i, j: (j, i)),
                   pl.BlockSpec((tm, D), lambda i, j: (i, 0))],
        out_shape=[jax.ShapeDtypeStruct((N, T), BF16), jax.ShapeDtypeStruct((T, D), BF16)],
        compiler_params=_compiler_params(("arbitrary", "arbitrary")),
        name="qk_proj_rope",
    )(x2, norm_g.reshape(1, D), wt_bf, row_gain, *rope_tabs)


def _vg_proj_kernel(*refs, with_f):
    if with_f:
        h_ref, w_ref, wf_ref, fb_ref, out_ref, logf_ref = refs

        @pl.when(pl.program_id(1) == 0)
        def _():
            f = _dot(h_ref[...], wf_ref[...]) + fb_ref[...]
            logf_ref[...] = jnp.minimum(f, 0.0) - jnp.log(1.0 + jnp.exp(-jnp.abs(f)))
    else:
        h_ref, w_ref, out_ref = refs
    out_ref[...] = _dot(h_ref[...], w_ref[...]).astype(BF16)


def _vg_proj(h, w_bf, col0, n_cols, f_col0=None, f_bias=None):
    T, D = h.shape
    tm, tn = PROJ_TM, PROJ_TN_FROM_H
    j0 = col0 // tn
    with_f = f_col0 is not None
    in_specs = [pl.BlockSpec((tm, D), lambda i, j: (i, 0)),
                pl.BlockSpec((D, tn), lambda i, j: (0, j0 + j))]
    args = [h, w_bf]
    out_specs = [pl.BlockSpec((tm, tn), lambda i, j: (i, j))]
    out_shape = [jax.ShapeDtypeStruct((T, n_cols), BF16)]
    if with_f:
        jf = f_col0 // HEAD_DIM
        in_specs += [pl.BlockSpec((D, HEAD_DIM), lambda i, j: (0, jf)),
                     pl.BlockSpec((1, HEAD_DIM), lambda i, j: (0, 0))]
        args += [w_bf, f_bias]
        out_specs.append(pl.BlockSpec((tm, HEAD_DIM), lambda i, j: (i, 0)))
        out_shape.append(jax.ShapeDtypeStruct((T, HEAD_DIM), F32))
    return pl.pallas_call(
        functools.partial(_vg_proj_kernel, with_f=with_f),
        grid=(T // tm, n_cols // tn),
        in_specs=in_specs,
        out_specs=out_specs,
        out_shape=out_shape,
        compiler_params=_compiler_params(("parallel", "arbitrary")),
        name="vg_proj_fgate" if with_f else "vg_proj",
    )(*args)


def _outproj_kernel(*refs, with_next_norm):
    if with_next_norm:
        o_ref, w_ref, x_ref, ng_ref, out_ref, h_ref = refs
    else:
        o_ref, w_ref, x_ref, out_ref = refs
    y = x_ref[...] + _dot(o_ref[...], w_ref[...])
    out_ref[...] = y
    if with_next_norm:
        ms = jnp.mean(y * y, axis=-1, keepdims=True)
        h_ref[...] = (y * lax.rsqrt(ms + EPS) * ng_ref[...]).astype(BF16)


def _outproj(o_bf, w_bf, x2, next_norm_g=None):
    T, D = x2.shape
    K = o_bf.shape[1]
    tm = OUT_TM
    with_next_norm = next_norm_g is not None
    row_block = pl.BlockSpec((tm, D), lambda i: (i, 0))
    in_specs = [pl.BlockSpec((tm, K), lambda i: (i, 0)),
                pl.BlockSpec((K, D), lambda i: (0, 0)),
                row_block]
    args = [o_bf, w_bf, x2]
    out_specs = [row_block]
    out_shape = [jax.ShapeDtypeStruct((T, D), F32)]
    if with_next_norm:
        in_specs.append(pl.BlockSpec((1, D), lambda i: (0, 0)))
        args.append(next_norm_g.reshape(1, D))
        out_specs.append(row_block)
        out_shape.append(jax.ShapeDtypeStruct((T, D), BF16))
    return pl.pallas_call(
        functools.partial(_outproj_kernel, with_next_norm=with_next_norm),
        grid=(T // tm,),
        in_specs=in_specs,
        out_specs=out_specs,
        out_shape=out_shape,
        compiler_params=_compiler_params(("parallel",)),
        name="outproj_residual_norm" if with_next_norm else "outproj_residual",
    )(*args)


def _softmax_t(s_t, r0, mask_t):
    last = jnp.where(mask_t, s_t[r0:, :], MASKED)
    m = jnp.max(last, axis=0, keepdims=True)
    if r0 == 0:
        return jnp.exp2(last - m).astype(BF16)
    first = s_t[:r0, :]
    m = jnp.maximum(m, jnp.max(first, axis=0, keepdims=True))
    return jnp.concatenate([jnp.exp2(first - m), jnp.exp2(last - m)], axis=0).astype(BF16)


def _fill_vt(vt_scr, v):
    d = v.shape[1]
    vt_scr[:d, :] = v.T
    vt_scr[d:, :] = jnp.ones((vt_scr.shape[0] - d, vt_scr.shape[1]), BF16)


def _silu(g):
    return g * (1.0 / (1.0 + jnp.exp(-g)))


def _diff_attn_kernel(qt_ref, kt_ref, v_ref, g_ref, lq1_ref, lk1_ref, lq2_ref, lk2_ref, sg_ref,
                      out_ref, vt_scr, k_scr, *, lambda_init):
    S = v_ref.shape[0]
    tq = ATTN_TQ
    dv = 2 * HEAD_DIM
    n_heads = v_ref.shape[1] // dv
    lam = (jnp.exp(jnp.sum(lq1_ref[...] * lk1_ref[...], axis=-1, keepdims=True))
           - jnp.exp(jnp.sum(lq2_ref[...] * lk2_ref[...], axis=-1, keepdims=True))
           + lambda_init)
    key_chunk = lax.broadcasted_iota(jnp.int32, (tq, tq), 0) // CHUNK
    qry_chunk = lax.broadcasted_iota(jnp.int32, (tq, tq), 1) // CHUNK
    mask_t = key_chunk <= qry_chunk
    sub_gain = sg_ref[...] * (1.0 - lambda_init)
    for hh in range(n_heads):
        _fill_vt(vt_scr.at[hh], v_ref[:, hh * dv:(hh + 1) * dv])
    k_scr[...] = kt_ref[...].T

    maps = [slice(m * HEAD_DIM, (m + 1) * HEAD_DIM) for m in range(2 * n_heads)]
    for qi in range(S // tq):
        r0 = qi * tq
        kv = r0 + tq
        s_ts = [_dot(k_scr[:kv, sl], qt_ref[sl, r0:kv]) for sl in maps]
        p_ts = [_softmax_t(s_t, r0, mask_t) for s_t in s_ts]
        os_ = [_dot(vt_scr[m // 2, :, :kv], p_t) for m, p_t in enumerate(p_ts)]
        for hh in range(n_heads):
            o1, o2 = os_[2 * hh], os_[2 * hh + 1]
            w1 = 1.0 / o1[dv:dv + 1, :]
            w2 = lam * (1.0 / o2[dv:dv + 1, :])
            o = (o1[:dv, :] * w1 - o2[:dv, :] * w2).T
            o = o * lax.rsqrt(jnp.mean(o * o, axis=-1, keepdims=True) + EPS) * sub_gain
            o = o * _silu(g_ref[r0:kv, hh * dv:(hh + 1) * dv].astype(F32))
            out_ref[r0:kv, hh * dv:(hh + 1) * dv] = o.astype(BF16)


def _diff_attention(qk_t, vg, B, S, lq1, lk1, lq2, lk2, sub_g, lambda_init):
    T = B * S
    hps = DIFF_HEADS_PER_STEP
    dv = 2 * HEAD_DIM
    width = hps * dv
    heads = (qk_t.shape[0] // 2) // width
    vec = lambda a: a.reshape(1, -1)
    small = lambda n: pl.BlockSpec((1, n), lambda b, h: (0, 0))
    return pl.pallas_call(
        functools.partial(_diff_attn_kernel, lambda_init=lambda_init),
        grid=(B, heads),
        in_specs=[pl.BlockSpec((width, S), lambda b, h: (h, b)),
                  pl.BlockSpec((width, S), lambda b, h: (heads + h, b)),
                  pl.BlockSpec((S, width), lambda b, h: (b, h)),
                  pl.BlockSpec((S, width), lambda b, h: (b, heads + h)),
                  small(HEAD_DIM), small(HEAD_DIM), small(HEAD_DIM), small(HEAD_DIM),
                  small(dv)],
        out_specs=pl.BlockSpec((S, width), lambda b, h: (b, h)),
        out_shape=jax.ShapeDtypeStruct((T, heads * width), BF16),
        scratch_shapes=[pltpu.VMEM((hps, dv + BF16_ROWS, S), BF16),
                        pltpu.VMEM((S, width), BF16)],
        compiler_params=_compiler_params(("parallel", "parallel")),
        name="diff_attention",
    )(qk_t, qk_t, vg, vg, vec(lq1), vec(lk1), vec(lq2), vec(lk2), vec(sub_g))


BIAS_PARTS = 3


def _cumsum_kernel(logf_ref, out_ref):
    S = logf_ref.shape[0]
    row = lax.broadcasted_iota(jnp.int32, (CUM_BLK, CUM_BLK), 0)
    col = lax.broadcasted_iota(jnp.int32, (CUM_BLK, CUM_BLK), 1)
    tri = (col <= row).astype(F32)
    src = lax.broadcasted_iota(jnp.int32, (HEAD_DIM, HEAD_DIM), 0)
    dst = lax.broadcasted_iota(jnp.int32, (HEAD_DIM, HEAD_DIM), 1)
    spread = [(dst == BIAS_PARTS * src + part).astype(BF16) for part in range(BIAS_PARTS)]
    off = jnp.zeros((1, HEAD_DIM), F32)
    for kb in range(S // CUM_BLK):
        rows = slice(kb * CUM_BLK, (kb + 1) * CUM_BLK)
        c = jnp.dot(tri, logf_ref[rows, :], preferred_element_type=F32,
                    precision=lax.Precision.HIGHEST) + off
        off = c[CUM_BLK - 1:CUM_BLK, :]
        rest = c * (-LOG2E)
        packed = jnp.zeros((CUM_BLK, HEAD_DIM), F32)
        for part in range(BIAS_PARTS):
            piece = rest.astype(BF16)
            rest = rest - piece.astype(F32)
            packed = packed + _dot(piece, spread[part])
        out_ref[rows, :] = packed.astype(BF16)


def _forget_bias(logf, B, S):
    T = B * S
    return pl.pallas_call(
        _cumsum_kernel,
        grid=(B,),
        in_specs=[pl.BlockSpec((S, HEAD_DIM), lambda b: (b, 0))],
        out_specs=pl.BlockSpec((S, HEAD_DIM), lambda b: (b, 0)),
        out_shape=jax.ShapeDtypeStruct((T, HEAD_DIM), BF16),
        compiler_params=_compiler_params(("parallel",)),
        name="forget_prefix_sum",
    )(logf)


def _fox_attn_kernel(q_ref, k_ref, v_ref, g_ref, kb_ref, out_ref, vt_scr):
    S = q_ref.shape[0]
    tq = ATTN_TQ
    heads_per_step = q_ref.shape[1] // HEAD_DIM
    key = lax.broadcasted_iota(jnp.int32, (tq, tq), 0)
    qry = lax.broadcasted_iota(jnp.int32, (tq, tq), 1)
    mask_t = key <= qry
    lane_head = lax.broadcasted_iota(jnp.int32, (tq, HEAD_DIM), 1) // BIAS_PARTS
    lanes = [slice(hh * HEAD_DIM, (hh + 1) * HEAD_DIM) for hh in range(heads_per_step)]
    picks = []
    for hh, sl in enumerate(lanes):
        head = pl.program_id(1) * heads_per_step + hh
        picks.append(jnp.where(lane_head == head, 1.0, 0.0).astype(BF16))
        _fill_vt(vt_scr.at[hh], v_ref[:, sl])
    for qi in range(S // tq):
        r0 = qi * tq
        kv = r0 + tq
        s_ts = [_dot_nt(jnp.concatenate([k_ref[:kv, sl], kb_ref[:kv, :]], axis=1),
                        jnp.concatenate([q_ref[r0:kv, sl], pick], axis=1))
                for sl, pick in zip(lanes, picks)]
        p_ts = [_softmax_t(s_t, r0, mask_t) for s_t in s_ts]
        os_ = [_dot(vt_scr[hh, :, :kv], p_t) for hh, p_t in enumerate(p_ts)]
        for sl, o in zip(lanes, os_):
            o = (o[:HEAD_DIM, :] * (1.0 / o[HEAD_DIM:HEAD_DIM + 1, :])).T
            o = o * _silu(g_ref[r0:kv, sl].astype(F32))
            out_ref[r0:kv, sl] = o.astype(BF16)


def _fox_attention(qk, vg, kbias, B, S):
    T = B * S
    hps = FOX_HEADS_PER_STEP
    width = hps * HEAD_DIM
    groups = (qk.shape[1] // 2) // width
    return pl.pallas_call(
        _fox_attn_kernel,
        grid=(B, groups),
        in_specs=[pl.BlockSpec((S, width), lambda b, h: (b, h)),
                  pl.BlockSpec((S, width), lambda b, h: (b, groups + h)),
                  pl.BlockSpec((S, width), lambda b, h: (b, h)),
                  pl.BlockSpec((S, width), lambda b, h: (b, groups + h)),
                  pl.BlockSpec((S, HEAD_DIM), lambda b, h: (b, 0))],
        out_specs=pl.BlockSpec((S, width), lambda b, h: (b, h)),
        out_shape=jax.ShapeDtypeStruct((T, groups * width), BF16),
        scratch_shapes=[pltpu.VMEM((hps, HEAD_DIM + BF16_ROWS, S), BF16)],
        compiler_params=_compiler_params(("parallel", "parallel")),
        name="fox_attention",
    )(qk, qk, vg, vg, kbias)


def kernel(x, positions, a_norm, a_w_in, a_q_norm, a_k_norm, a_lambda_q1, a_lambda_k1,
           a_lambda_q2, a_lambda_k2, a_sub_norm, a_w_out, b_norm, b_w_in, b_f_bias,
           b_q_norm, b_k_norm, b_w_out):
    B, S, D = x.shape
    T = B * S
    width = a_w_out.shape[1]
    heads_per_width = width // HEAD_DIM
    q_scale = HEAD_DIM ** -0.5 * LOG2E
    x2 = x.reshape(T, D)

    def col_gain(qg, kg):
        return jnp.concatenate([jnp.tile(qg * q_scale, heads_per_width),
                                jnp.tile(kg, heads_per_width)]).reshape(1, 2 * width)

    lambda_init0 = 0.8 - 0.6 * math.exp(-0.3 * 0)
    row_gain = jnp.broadcast_to(col_gain(a_q_norm[0], a_k_norm[0]).reshape(2 * width, 1),
                                (2 * width, HEAD_DIM))
    qk_t, h = _qk_proj_t(x2, a_norm[0], _cast_weight(a_w_in[0], 0, 2 * width, transpose_out=True),
                         row_gain, _rope_tables(positions))
    (vg,) = _vg_proj(h, _cast_weight(a_w_in[0], 2 * width, 2 * width), 0, 2 * width)
    o = _diff_attention(qk_t, vg, B, S, a_lambda_q1[0], a_lambda_k1[0], a_lambda_q2[0],
                        a_lambda_k2[0], a_sub_norm[0], lambda_init0)
    x2, h = _outproj(o, a_w_out[0].astype(BF16), x2, next_norm_g=b_norm[0])

    w1 =_cast_weight(b_w_in[0].T, 0, b_w_in.shape[2], w_is_transposed=True)
    n_f = b_w_in.shape[2] - 4 * width
    assert n_f * BIAS_PARTS <= HEAD_DIM
    fb = jnp.pad(b_f_bias[0], (0, HEAD_DIM - n_f)).reshape(1, HEAD_DIM)
    qk = _qk_proj(h, w1, col_gain(b_q_norm[0], b_k_norm[0]))
    vg, logf = _vg_proj(h, w1, 2 * width, 2 * width, f_col0=4 * width, f_bias=fb)
    o = _fox_attention(qk, vg, _forget_bias(logf, B, S), B, S)
    (x2,) = _outproj(o, b_w_out[0].astype(BF16), x2)
    return x2.reshape(B, S, D)
```

```python
import functools
import math

import numpy as np
import jax
import jax.numpy as jnp
from jax import lax
from jax.experimental import pallas as pl
from jax.experimental.pallas import tpu as pltpu

F32 = jnp.float32
BF16 = jnp.bfloat16

EPS = 1e-6
CHUNK = 64
ROPE_THETA = 500000.0
HEAD_DIM = 128
BF16_ROWS = 16
ROT_DIM = HEAD_DIM // 4
ROT_HALF = ROT_DIM // 2
LOG2E = math.log2(math.e)
MASKED = -1e30

V7X_VMEM_LIMIT_BYTES = 56 * 1024 * 1024

PROJ_TM = 1024
PROJ_TN = 1024
PROJ_TN_FROM_H = 2048
OUT_TM = 512
ATTN_TQ = 512
CUM_BLK = 256
CAST_TN = 512
DIFF_HEADS_PER_STEP = 2
FOX_HEADS_PER_STEP = 4


def _compiler_params(semantics):
    return pltpu.CompilerParams(dimension_semantics=semantics,
                                vmem_limit_bytes=V7X_VMEM_LIMIT_BYTES)


def _dot(a, b):
    return jnp.dot(a, b, preferred_element_type=F32)


def _dot_nt(a, b):
    return lax.dot_general(a, b, (((1,), (1,)), ((), ())), preferred_element_type=F32)


def _cast_kernel(w_ref, out_ref, *, flip, col_axis, n_valid):
    w = w_ref[...]
    if n_valid is not None:
        col = (pl.program_id(0) * w.shape[col_axis]
               + lax.broadcasted_iota(jnp.int32, w.shape, col_axis))
        w = jnp.where(col < n_valid, w, 0.0)
    out_ref[...] = (w.T if flip else w).astype(BF16)


def _cast_weight(w, col0, n_cols, transpose_out=False, w_is_transposed=False):
    N, K = w.shape if w_is_transposed else w.shape[::-1]
    tc = CAST_TN
    j0 = col0 // tc
    n_cols = pl.cdiv(n_cols, tc) * tc
    n_valid = N - col0 if col0 + n_cols > N else None
    if w_is_transposed:
        in_spec = pl.BlockSpec((tc, K), lambda j: (j0 + j, 0))
    else:
        in_spec = pl.BlockSpec((K, tc), lambda j: (0, j0 + j))
    if transpose_out:
        out_spec = pl.BlockSpec((tc, K), lambda j: (j, 0))
        out_shape = jax.ShapeDtypeStruct((n_cols, K), BF16)
    else:
        out_spec = pl.BlockSpec((K, tc), lambda j: (0, j))
        out_shape = jax.ShapeDtypeStruct((K, n_cols), BF16)
    flip = transpose_out != w_is_transposed
    return pl.pallas_call(
        functools.partial(_cast_kernel, flip=flip, col_axis=0 if w_is_transposed else 1,
                          n_valid=n_valid),
        grid=(n_cols // tc,),
        in_specs=[in_spec],
        out_specs=out_spec,
        out_shape=out_shape,
        compiler_params=_compiler_params(("parallel",)),
        name="cast_weight_flip" if flip else "cast_weight",
    )(w)


def _rope_tables_kernel(pos_ref, invf_ref, cos_ref, sin_ref):
    ang = invf_ref[...] * pos_ref[...]
    cos_ref[...] = jnp.cos(ang)
    sin_ref[...] = jnp.sin(ang)


def _rope_tables(positions):
    T = positions.size
    tb = 2048
    pos = positions.reshape(1, T).astype(F32)
    invf = (ROPE_THETA ** (-np.arange(ROT_HALF, dtype=np.float32) / ROT_HALF)).astype(np.float32)
    invf = jnp.asarray(invf.reshape(ROT_HALF, 1))
    tab = jax.ShapeDtypeStruct((ROT_HALF, T), F32)
    return pl.pallas_call(
        _rope_tables_kernel,
        grid=(T // tb,),
        in_specs=[pl.BlockSpec((1, tb), lambda i: (0, i)),
                  pl.BlockSpec((ROT_HALF, 1), lambda i: (0, 0))],
        out_specs=[pl.BlockSpec((ROT_HALF, tb), lambda i: (0, i))] * 2,
        out_shape=[tab, tab],
        compiler_params=_compiler_params(("parallel",)),
        name="rope_tables",
    )(pos, invf)


def _rmsnorm_to(h_ref, x_ref, ng_ref):
    xf = x_ref[...]
    ms = jnp.mean(xf * xf, axis=-1, keepdims=True)
    h_ref[...] = (xf * lax.rsqrt(ms + EPS) * ng_ref[...]).astype(BF16)


def _qk_proj_kernel(h_ref, w_ref, cg_ref, out_ref):
    acc = _dot(h_ref[...], w_ref[...])
    for hh in range(acc.shape[1] // HEAD_DIM):
        sl = slice(hh * HEAD_DIM, (hh + 1) * HEAD_DIM)
        a = acc[:, sl]
        r = lax.rsqrt(jnp.mean(a * a, axis=-1, keepdims=True) + EPS)
        out_ref[:, sl] = (a * r * cg_ref[:, sl]).astype(BF16)


def _qk_proj(h, w_bf, col_gain):
    T, D = h.shape
    N = col_gain.shape[1]
    tm, tn = PROJ_TM, PROJ_TN_FROM_H
    return pl.pallas_call(
        _qk_proj_kernel,
        grid=(T // tm, N // tn),
        in_specs=[pl.BlockSpec((tm, D), lambda i, j: (i, 0)),
                  pl.BlockSpec((D, tn), lambda i, j: (0, j)),
                  pl.BlockSpec((1, tn), lambda i, j: (0, j))],
        out_specs=pl.BlockSpec((tm, tn), lambda i, j: (i, j)),
        out_shape=jax.ShapeDtypeStruct((T, N), BF16),
        compiler_params=_compiler_params(("parallel", "arbitrary")),
        name="qk_proj",
    )(h, w_bf, col_gain)


def _qk_proj_t_kernel(x_ref, ng_ref, wt_ref, g_ref, cos_ref, sin_ref, out_ref, h_ref):
    @pl.when(pl.program_id(1) == 0)
    def _():
        _rmsnorm_to(h_ref, x_ref, ng_ref)

    acc = _dot_nt(wt_ref[...], h_ref[...])
    tm = acc.shape[1]
    cos = cos_ref[...]
    sin = sin_ref[...]
    for hh in range(acc.shape[0] // HEAD_DIM):
        rows = slice(hh * HEAD_DIM, (hh + 1) * HEAD_DIM)
        a = acc[rows, :]
        r = lax.rsqrt(jnp.mean(a * a, axis=0, keepdims=True) + EPS)
        gain = jnp.concatenate([g_ref[rows, :]] * (tm // HEAD_DIM), axis=1)
        y = a * r * gain
        y1 = y[:ROT_HALF, :]
        y2 = y[ROT_HALF:ROT_DIM, :]
        y = jnp.concatenate([y1 * cos - y2 * sin, y2 * cos + y1 * sin, y[ROT_DIM:, :]], axis=0)
        out_ref[rows, :] = y.astype(BF16)


def _qk_proj_t(x2, norm_g, wt_bf, row_gain, rope_tabs):
    T, D = x2.shape
    N = row_gain.shape[0]
    tm, tn = PROJ_TM, PROJ_TN
    n_row_steps, n_col_steps = T // tm, N // tn
    assert n_col_steps >= 2

    def x_block(i, j):
        return (jnp.minimum(i + jnp.minimum(j, 1), n_row_steps - 1), 0)

    return pl.pallas_call(
        _qk_proj_t_kernel,
        grid=(n_row_steps, n_col_steps),
        in_specs=[pl.BlockSpec((tm, D), x_block),
                  pl.BlockSpec((1, D), lambda i, j: (0, 0)),
                  pl.BlockSpec((tn, D), lambda i, j: (j, 0)),
                  pl.BlockSpec((tn, HEAD_DIM), lambda i, j: (j, 0)),
                  pl.BlockSpec((ROT_HALF, tm), lambda i, j: (0, i)),
                  pl.BlockSpec((ROT_HALF, tm), lambda i, j: (0, i))],
        out_specs=[pl.BlockSpec((tn, tm), lambda i, j: (j, i)),
                   pl.BlockSpec((tm, D), lambda i, j: (i, 0))],
        out_shape=[jax.ShapeDtypeStruct((N, T), BF16), jax.ShapeDtypeStruct((T, D), BF16)],
        compiler_params=_compiler_params(("arbitrary", "arbitrary")),
        name="qk_proj_rope",
    )(x2, norm_g.reshape(1, D), wt_bf, row_gain, *rope_tabs)


def _vg_proj_kernel(*refs, with_f, n_col_steps):
    if with_f:
        h_ref, w_ref, wf_ref, fb_ref, out_ref, logf_ref = refs
        share = h_ref.shape[0] // n_col_steps
        rows = pl.ds(pl.multiple_of(pl.program_id(1) * share, share), share)
        f = _dot(h_ref[rows, :], wf_ref[...]) + fb_ref[...]
        logf_ref[rows, :] = jnp.minimum(f, 0.0) - jnp.log(1.0 + jnp.exp(-jnp.abs(f)))
    else:
        h_ref, w_ref, out_ref = refs
    out_ref[...] = _dot(h_ref[...], w_ref[...]).astype(BF16)


def _vg_proj(h, w_bf, col0, n_cols, f_col0=None, f_bias=None):
    T, D = h.shape
    tm, tn = PROJ_TM, PROJ_TN_FROM_H
    j0 = col0 // tn
    with_f = f_col0 is not None
    in_specs = [pl.BlockSpec((tm, D), lambda i, j: (i, 0)),
                pl.BlockSpec((D, tn), lambda i, j: (0, j0 + j))]
    args = [h, w_bf]
    out_specs = [pl.BlockSpec((tm, tn), lambda i, j: (i, j))]
    out_shape = [jax.ShapeDtypeStruct((T, n_cols), BF16)]
    if with_f:
        jf = f_col0 // HEAD_DIM
        in_specs += [pl.BlockSpec((D, HEAD_DIM), lambda i, j: (0, jf)),
                     pl.BlockSpec((1, HEAD_DIM), lambda i, j: (0, 0))]
        args += [w_bf, f_bias]
        out_specs.append(pl.BlockSpec((tm, HEAD_DIM), lambda i, j: (i, 0)))
        out_shape.append(jax.ShapeDtypeStruct((T, HEAD_DIM), F32))
    return pl.pallas_call(
        functools.partial(_vg_proj_kernel, with_f=with_f, n_col_steps=n_cols // tn),
        grid=(T // tm, n_cols // tn),
        in_specs=in_specs,
        out_specs=out_specs,
        out_shape=out_shape,
        compiler_params=_compiler_params(("parallel", "arbitrary")),
        name="vg_proj_fgate" if with_f else "vg_proj",
    )(*args)


def _outproj_kernel(*refs, with_next_norm):
    if with_next_norm:
        o_ref, w_ref, x_ref, ng_ref, out_ref, h_ref = refs
    else:
        o_ref, w_ref, x_ref, out_ref = refs
    y = x_ref[...] + _dot(o_ref[...], w_ref[...])
    out_ref[...] = y
    if with_next_norm:
        ms = jnp.mean(y * y, axis=-1, keepdims=True)
        h_ref[...] = (y * lax.rsqrt(ms + EPS) * ng_ref[...]).astype(BF16)


def _outproj(o_bf, w_bf, x2, next_norm_g=None):
    T, D = x2.shape
    K = o_bf.shape[1]
    tm = OUT_TM
    with_next_norm = next_norm_g is not None
    row_block = pl.BlockSpec((tm, D), lambda i: (i, 0))
    in_specs = [pl.BlockSpec((tm, K), lambda i: (i, 0)),
                pl.BlockSpec((K, D), lambda i: (0, 0)),
                row_block]
    args = [o_bf, w_bf, x2]
    out_specs = [row_block]
    out_shape = [jax.ShapeDtypeStruct((T, D), F32)]
    if with_next_norm:
        in_specs.append(pl.BlockSpec((1, D), lambda i: (0, 0)))
        args.append(next_norm_g.reshape(1, D))
        out_specs.append(row_block)
        out_shape.append(jax.ShapeDtypeStruct((T, D), BF16))
    return pl.pallas_call(
        functools.partial(_outproj_kernel, with_next_norm=with_next_norm),
        grid=(T // tm,),
        in_specs=in_specs,
        out_specs=out_specs,
        out_shape=out_shape,
        compiler_params=_compiler_params(("parallel",)),
        name="outproj_residual_norm" if with_next_norm else "outproj_residual",
    )(*args)


LOGIT_BOUND = 46.0


def _logits_are_bounded(q_gain, k_gain):
    bound = 1.02 * HEAD_DIM * jnp.max(jnp.abs(q_gain)) * jnp.max(jnp.abs(k_gain))
    return (bound < LOGIT_BOUND).astype(jnp.int32).reshape(1)


def _softmax_t(s_t, r0, mask_t, subtract_max):
    last = jnp.where(mask_t, s_t[r0:, :], MASKED)
    first = s_t[:r0, :] if r0 > 0 else None
    if subtract_max:
        m = jnp.max(last, axis=0, keepdims=True)
        if r0 > 0:
            m = jnp.maximum(m, jnp.max(first, axis=0, keepdims=True))
            first = first - m
        last = last - m
    if r0 == 0:
        return jnp.exp2(last).astype(BF16)
    return jnp.concatenate([jnp.exp2(first), jnp.exp2(last)], axis=0).astype(BF16)


def _fill_vt(vt_scr, v):
    d = v.shape[1]
    vt_scr[:d, :] = v.T
    vt_scr[d:, :] = jnp.ones((vt_scr.shape[0] - d, vt_scr.shape[1]), BF16)


def _silu(g):
    return g * (1.0 / (1.0 + jnp.exp(-g)))


def _diff_attn_kernel(bounded_ref, qt_ref, kt_ref, v_ref, g_ref, lq1_ref, lk1_ref, lq2_ref, lk2_ref,
                      sg_ref, out_ref, vt_scr, k_scr, *, lambda_init):
    S = v_ref.shape[0]
    tq = ATTN_TQ
    dv = 2 * HEAD_DIM
    n_heads = v_ref.shape[1] // dv
    lam = (jnp.exp(jnp.sum(lq1_ref[...] * lk1_ref[...], axis=-1, keepdims=True))
           - jnp.exp(jnp.sum(lq2_ref[...] * lk2_ref[...], axis=-1, keepdims=True))
           + lambda_init)
    key_chunk = lax.broadcasted_iota(jnp.int32, (tq, tq), 0) // CHUNK
    qry_chunk = lax.broadcasted_iota(jnp.int32, (tq, tq), 1) // CHUNK
    mask_t = key_chunk <= qry_chunk
    sub_gain = sg_ref[...] * (1.0 - lambda_init)

    maps = [slice(m * HEAD_DIM, (m + 1) * HEAD_DIM) for m in range(2 * n_heads)]

    def attend(subtract_max):
        for hh in range(n_heads):
            _fill_vt(vt_scr.at[hh], v_ref[:, hh * dv:(hh + 1) * dv])
        k_scr[...] = kt_ref[...].T
        for qi in range(S // tq):
            r0 = qi * tq
            kv = r0 + tq
            s_ts = [_dot(k_scr[:kv, sl], qt_ref[sl, r0:kv]) for sl in maps]
            p_ts = [_softmax_t(s_t, r0, mask_t, subtract_max) for s_t in s_ts]
            os_ = [_dot(vt_scr[m // 2, :, :kv], p_t) for m, p_t in enumerate(p_ts)]
            for hh in range(n_heads):
                o1, o2 = os_[2 * hh], os_[2 * hh + 1]
                w1 = 1.0 / o1[dv:dv + 1, :]
                w2 = lam * (1.0 / o2[dv:dv + 1, :])
                o = (o1[:dv, :] * w1 - o2[:dv, :] * w2).T
                o = o * lax.rsqrt(jnp.mean(o * o, axis=-1, keepdims=True) + EPS) * sub_gain
                o = o * _silu(g_ref[r0:kv, hh * dv:(hh + 1) * dv].astype(F32))
                out_ref[r0:kv, hh * dv:(hh + 1) * dv] = o.astype(BF16)

    bounded = bounded_ref[0] != 0
    pl.when(bounded)(lambda: attend(subtract_max=False))
    pl.when(jnp.logical_not(bounded))(lambda: attend(subtract_max=True))


def _diff_attention(bounded, qk_t, vg, B, S, lq1, lk1, lq2, lk2, sub_g, lambda_init):
    T = B * S
    hps = DIFF_HEADS_PER_STEP
    dv = 2 * HEAD_DIM
    width = hps * dv
    heads = (qk_t.shape[0] // 2) // width
    vec = lambda a: a.reshape(1, -1)
    small = lambda n: pl.BlockSpec((1, n), lambda b, h: (0, 0))
    return pl.pallas_call(
        functools.partial(_diff_attn_kernel, lambda_init=lambda_init),
        grid=(B, heads),
        in_specs=[pl.BlockSpec(memory_space=pltpu.SMEM),
                  pl.BlockSpec((width, S), lambda b, h: (h, b)),
                  pl.BlockSpec((width, S), lambda b, h: (heads + h, b)),
                  pl.BlockSpec((S, width), lambda b, h: (b, h)),
                  pl.BlockSpec((S, width), lambda b, h: (b, heads + h)),
                  small(HEAD_DIM), small(HEAD_DIM), small(HEAD_DIM), small(HEAD_DIM),
                  small(dv)],
        out_specs=pl.BlockSpec((S, width), lambda b, h: (b, h)),
        out_shape=jax.ShapeDtypeStruct((T, heads * width), BF16),
        scratch_shapes=[pltpu.VMEM((hps, dv + BF16_ROWS, S), BF16),
                        pltpu.VMEM((S, width), BF16)],
        compiler_params=_compiler_params(("parallel", "parallel")),
        name="diff_attention",
    )(bounded, qk_t, qk_t, vg, vg, vec(lq1), vec(lk1), vec(lq2), vec(lk2), vec(sub_g))


BIAS_PARTS = 3


QUERY_LANE0 = HEAD_DIM // 2


def _cumsum_kernel(logf_ref, kkey_ref, kqry_ref, *, n_heads):
    S = logf_ref.shape[0]
    row = lax.broadcasted_iota(jnp.int32, (CUM_BLK, CUM_BLK), 0)
    col = lax.broadcasted_iota(jnp.int32, (CUM_BLK, CUM_BLK), 1)
    tri = (col <= row).astype(F32)
    src = lax.broadcasted_iota(jnp.int32, (HEAD_DIM, HEAD_DIM), 0)
    dst = lax.broadcasted_iota(jnp.int32, (HEAD_DIM, HEAD_DIM), 1)
    is_head = src < n_heads
    spread = [jnp.where(is_head & (dst == BIAS_PARTS * src + part), 1.0,
                        jnp.where(is_head & (dst == QUERY_LANE0 + BIAS_PARTS * src + part),
                                  -1.0, 0.0)).astype(BF16)
              for part in range(BIAS_PARTS)]
    lane = lax.broadcasted_iota(jnp.int32, (CUM_BLK, HEAD_DIM), 1)
    key_lanes = lane < QUERY_LANE0
    ones_lanes = (lane >= QUERY_LANE0) & (lane < QUERY_LANE0 + BIAS_PARTS * n_heads)
    local = [jnp.dot(tri, logf_ref[kb * CUM_BLK:(kb + 1) * CUM_BLK, :], preferred_element_type=F32,
                     precision=lax.Precision.HIGHEST) for kb in range(S // CUM_BLK)]
    off = jnp.zeros((1, HEAD_DIM), F32)
    for kb in range(S // CUM_BLK):
        rows = slice(kb * CUM_BLK, (kb + 1) * CUM_BLK)
        c = local[kb] + off
        off = off + local[kb][CUM_BLK - 1:CUM_BLK, :]
        rest = c * (-LOG2E)
        packed = jnp.zeros((CUM_BLK, HEAD_DIM), F32)
        for part in range(BIAS_PARTS):
            piece = rest.astype(BF16)
            rest = rest - piece.astype(F32)
            packed = packed + _dot(piece, spread[part])
        kkey_ref[rows, :] = jnp.where(key_lanes, packed, jnp.where(ones_lanes, 1.0, 0.0)).astype(BF16)
        kqry_ref[rows, :] = jnp.where(key_lanes, 0.0, packed).astype(BF16)


def _forget_bias(logf, B, S, n_heads):
    T = B * S
    assert QUERY_LANE0 + BIAS_PARTS * n_heads <= HEAD_DIM and BIAS_PARTS * n_heads <= QUERY_LANE0
    table = jax.ShapeDtypeStruct((T, HEAD_DIM), BF16)
    return pl.pallas_call(
        functools.partial(_cumsum_kernel, n_heads=n_heads),
        grid=(B,),
        in_specs=[pl.BlockSpec((S, HEAD_DIM), lambda b: (b, 0))],
        out_specs=[pl.BlockSpec((S, HEAD_DIM), lambda b: (b, 0))] * 2,
        out_shape=[table, table],
        compiler_params=_compiler_params(("parallel",)),
        name="forget_prefix_sum",
    )(logf)


def _fox_attn_kernel(bounded_ref, q_ref, k_ref, v_ref, g_ref, kkey_ref, kqry_ref, out_ref, vt_scr):
    S = q_ref.shape[0]
    tq = ATTN_TQ
    heads_per_step = q_ref.shape[1] // HEAD_DIM
    key = lax.broadcasted_iota(jnp.int32, (tq, tq), 0)
    qry = lax.broadcasted_iota(jnp.int32, (tq, tq), 1)
    mask_t = key <= qry
    lane = lax.broadcasted_iota(jnp.int32, (tq, HEAD_DIM), 1)
    lanes = [slice(hh * HEAD_DIM, (hh + 1) * HEAD_DIM) for hh in range(heads_per_step)]
    heads = [pl.program_id(1) * heads_per_step + hh for hh in range(heads_per_step)]

    def attend(subtract_max):
        for hh, sl in enumerate(lanes):
            _fill_vt(vt_scr.at[hh], v_ref[:, sl])
        for qi in range(S // tq):
            r0 = qi * tq
            kv = r0 + tq
            s_ts = []
            for sl, head in zip(lanes, heads):
                pick = lane // BIAS_PARTS == head
                extra = jnp.where(pick, 1.0, 0.0).astype(BF16)
                if not subtract_max:
                    own = (lane >= QUERY_LANE0) & ((lane - QUERY_LANE0) // BIAS_PARTS == head)
                    extra = jnp.where(own, kqry_ref[r0:kv, :], extra)
                s_ts.append(_dot_nt(jnp.concatenate([k_ref[:kv, sl], kkey_ref[:kv, :]], axis=1),
                                    jnp.concatenate([q_ref[r0:kv, sl], extra], axis=1)))
            p_ts = [_softmax_t(s_t, r0, mask_t, subtract_max) for s_t in s_ts]
            os_ = [_dot(vt_scr[hh, :, :kv], p_t) for hh, p_t in enumerate(p_ts)]
            for sl, o in zip(lanes, os_):
                o = (o[:HEAD_DIM, :] * (1.0 / o[HEAD_DIM:HEAD_DIM + 1, :])).T
                o = o * _silu(g_ref[r0:kv, sl].astype(F32))
                out_ref[r0:kv, sl] = o.astype(BF16)

    bounded = bounded_ref[0] != 0
    pl.when(bounded)(lambda: attend(subtract_max=False))
    pl.when(jnp.logical_not(bounded))(lambda: attend(subtract_max=True))


def _fox_attention(bounded, qk, vg, kkey, kqry, B, S):
    T = B * S
    hps = FOX_HEADS_PER_STEP
    width = hps * HEAD_DIM
    groups = (qk.shape[1] // 2) // width
    table = pl.BlockSpec((S, HEAD_DIM), lambda b, h: (b, 0))
    return pl.pallas_call(
        _fox_attn_kernel,
        grid=(B, groups),
        in_specs=[pl.BlockSpec(memory_space=pltpu.SMEM),
                  pl.BlockSpec((S, width), lambda b, h: (b, h)),
                  pl.BlockSpec((S, width), lambda b, h: (b, groups + h)),
                  pl.BlockSpec((S, width), lambda b, h: (b, h)),
                  pl.BlockSpec((S, width), lambda b, h: (b, groups + h)),
                  table, table],
        out_specs=pl.BlockSpec((S, width), lambda b, h: (b, h)),
        out_shape=jax.ShapeDtypeStruct((T, groups * width), BF16),
        scratch_shapes=[pltpu.VMEM((hps, HEAD_DIM + BF16_ROWS, S), BF16)],
        compiler_params=_compiler_params(("parallel", "parallel")),
        name="fox_attention",
    )(bounded, qk, qk, vg, vg, kkey, kqry)


def kernel(x, positions, a_norm, a_w_in, a_q_norm, a_k_norm, a_lambda_q1, a_lambda_k1,
           a_lambda_q2, a_lambda_k2, a_sub_norm, a_w_out, b_norm, b_w_in, b_f_bias,
           b_q_norm, b_k_norm, b_w_out):
    B, S, D = x.shape
    T = B * S
    width = a_w_out.shape[1]
    heads_per_width = width // HEAD_DIM
    q_scale = HEAD_DIM ** -0.5 * LOG2E
    x2 = x.reshape(T, D)

    def col_gain(qg, kg):
        return jnp.concatenate([jnp.tile(qg * q_scale, heads_per_width),
                                jnp.tile(kg, heads_per_width)]).reshape(1, 2 * width)

    lambda_init0 = 0.8 - 0.6 * math.exp(-0.3 * 0)
    row_gain = jnp.broadcast_to(col_gain(a_q_norm[0], a_k_norm[0]).reshape(2 * width, 1),
                                (2 * width, HEAD_DIM))
    qk_t, h = _qk_proj_t(x2, a_norm[0], _cast_weight(a_w_in[0], 0, 2 * width, transpose_out=True),
                         row_gain, _rope_tables(positions))
    (vg,) = _vg_proj(h, _cast_weight(a_w_in[0], 2 * width, 2 * width), 0, 2 * width)
    o = _diff_attention(_logits_are_bounded(a_q_norm[0] * q_scale, a_k_norm[0]), qk_t, vg, B, S,
                        a_lambda_q1[0], a_lambda_k1[0], a_lambda_q2[0], a_lambda_k2[0],
                        a_sub_norm[0], lambda_init0)
    x2, h = _outproj(o, a_w_out[0].astype(BF16), x2, next_norm_g=b_norm[0])

    w1 = _cast_weight(b_w_in[0].T, 0, b_w_in.shape[2], w_is_transposed=True)
    n_f = b_w_in.shape[2] - 4 * width
    fb = jnp.pad(b_f_bias[0], (0, HEAD_DIM - n_f)).reshape(1, HEAD_DIM)
    qk = _qk_proj(h, w1, col_gain(b_q_norm[0], b_k_norm[0]))
    vg, logf = _vg_proj(h, w1, 2 * width, 2 * width, f_col0=4 * width, f_bias=fb)
    kkey, kqry = _forget_bias(logf, B, S, n_f)
    o = _fox_attention(_logits_are_bounded(b_q_norm[0] * q_scale, b_k_norm[0]), qk, vg, kkey, kqry,
                       B, S)
    (x2,) = _outproj(o, b_w_out[0].astype(BF16), x2)
    return x2.reshape(B, S, D)
```

```python
import functools
import math

import numpy as np
import jax
import jax.numpy as jnp
from jax import lax
from jax.experimental import pallas as pl
from jax.experimental.pallas import tpu as pltpu

F32 = jnp.float32
BF16 = jnp.bfloat16

EPS = 1e-6
CHUNK = 64
ROPE_THETA = 500000.0
HEAD_DIM = 128
BF16_ROWS = 16
ROT_DIM = HEAD_DIM // 4
ROT_HALF = ROT_DIM // 2
LOG2E = math.log2(math.e)
MASKED = -1e30

V7X_VMEM_LIMIT_BYTES = 56 * 1024 * 1024

PROJ_TM = 1024
PROJ_TN = 1024
PROJ_TN_FROM_H = 2048
OUT_TM = 512
ATTN_TQ = 512
CUM_BLK = 256
CAST_TN = 512
DIFF_HEADS_PER_STEP = 2
FOX_HEADS_PER_STEP = 4


def _compiler_params(semantics):
    return pltpu.CompilerParams(dimension_semantics=semantics,
                                vmem_limit_bytes=V7X_VMEM_LIMIT_BYTES)


def _dot(a, b):
    return jnp.dot(a, b, preferred_element_type=F32)


def _dot_nt(a, b):
    return lax.dot_general(a, b, (((1,), (1,)), ((), ())), preferred_element_type=F32)


def _cast_kernel(w_ref, out_ref, *, flip, col_axis, n_valid):
    w = w_ref[...]
    if n_valid is not None:
        col = (pl.program_id(0) * w.shape[col_axis]
               + lax.broadcasted_iota(jnp.int32, w.shape, col_axis))
        w = jnp.where(col < n_valid, w, 0.0)
    out_ref[...] = (w.T if flip else w).astype(BF16)


def _cast_weight(w, col0, n_cols, transpose_out=False, w_is_transposed=False):
    N, K = w.shape if w_is_transposed else w.shape[::-1]
    tc = CAST_TN
    j0 = col0 // tc
    n_cols = pl.cdiv(n_cols, tc) * tc
    n_valid = N - col0 if col0 + n_cols > N else None
    if w_is_transposed:
        in_spec = pl.BlockSpec((tc, K), lambda j: (j0 + j, 0))
    else:
        in_spec = pl.BlockSpec((K, tc), lambda j: (0, j0 + j))
    if transpose_out:
        out_spec = pl.BlockSpec((tc, K), lambda j: (j, 0))
        out_shape = jax.ShapeDtypeStruct((n_cols, K), BF16)
    else:
        out_spec = pl.BlockSpec((K, tc), lambda j: (0, j))
        out_shape = jax.ShapeDtypeStruct((K, n_cols), BF16)
    flip = transpose_out != w_is_transposed
    return pl.pallas_call(
        functools.partial(_cast_kernel, flip=flip, col_axis=0 if w_is_transposed else 1,
                          n_valid=n_valid),
        grid=(n_cols // tc,),
        in_specs=[in_spec],
        out_specs=out_spec,
        out_shape=out_shape,
        compiler_params=_compiler_params(("parallel",)),
        name="cast_weight_flip" if flip else "cast_weight",
    )(w)


def _rope_tables_kernel(pos_ref, invf_ref, cos_ref, sin_ref):
    ang = invf_ref[...] * pos_ref[...]
    cos_ref[...] = jnp.cos(ang)
    sin_ref[...] = jnp.sin(ang)


def _rope_tables(positions):
    T = positions.size
    tb = 2048
    pos = positions.reshape(1, T).astype(F32)
    invf = (ROPE_THETA ** (-np.arange(ROT_HALF, dtype=np.float32) / ROT_HALF)).astype(np.float32)
    invf = jnp.asarray(invf.reshape(ROT_HALF, 1))
    tab = jax.ShapeDtypeStruct((ROT_HALF, T), F32)
    return pl.pallas_call(
        _rope_tables_kernel,
        grid=(T // tb,),
        in_specs=[pl.BlockSpec((1, tb), lambda i: (0, i)),
                  pl.BlockSpec((ROT_HALF, 1), lambda i: (0, 0))],
        out_specs=[pl.BlockSpec((ROT_HALF, tb), lambda i: (0, i))] * 2,
        out_shape=[tab, tab],
        compiler_params=_compiler_params(("parallel",)),
        name="rope_tables",
    )(pos, invf)


def _rmsnorm_to(h_ref, x_ref, ng_ref):
    xf = x_ref[...]
    ms = jnp.mean(xf * xf, axis=-1, keepdims=True)
    h_ref[...] = (xf * lax.rsqrt(ms + EPS) * ng_ref[...]).astype(BF16)


def _qk_proj_kernel(h_ref, w_ref, cg_ref, out_ref):
    acc = _dot(h_ref[...], w_ref[...])
    for hh in range(acc.shape[1] // HEAD_DIM):
        sl = slice(hh * HEAD_DIM, (hh + 1) * HEAD_DIM)
        a = acc[:, sl]
        r = lax.rsqrt(jnp.mean(a * a, axis=-1, keepdims=True) + EPS)
        out_ref[:, sl] = (a * r * cg_ref[:, sl]).astype(BF16)


def _qk_proj(h, w_bf, col_gain):
    T, D = h.shape
    N = col_gain.shape[1]
    tm, tn = PROJ_TM, PROJ_TN_FROM_H
    return pl.pallas_call(
        _qk_proj_kernel,
        grid=(T // tm, N // tn),
        in_specs=[pl.BlockSpec((tm, D), lambda i, j: (i, 0)),
                  pl.BlockSpec((D, tn), lambda i, j: (0, j)),
                  pl.BlockSpec((1, tn), lambda i, j: (0, j))],
        out_specs=pl.BlockSpec((tm, tn), lambda i, j: (i, j)),
        out_shape=jax.ShapeDtypeStruct((T, N), BF16),
        compiler_params=_compiler_params(("parallel", "arbitrary")),
        name="qk_proj",
    )(h, w_bf, col_gain)


def _qk_proj_t_kernel(x_ref, ng_ref, wt_ref, g_ref, cos_ref, sin_ref, out_ref, h_ref):
    @pl.when(pl.program_id(1) == 0)
    def _():
        _rmsnorm_to(h_ref, x_ref, ng_ref)

    acc = _dot_nt(wt_ref[...], h_ref[...])
    tm = acc.shape[1]
    cos = cos_ref[...]
    sin = sin_ref[...]
    for hh in range(acc.shape[0] // HEAD_DIM):
        rows = slice(hh * HEAD_DIM, (hh + 1) * HEAD_DIM)
        a = acc[rows, :]
        r = lax.rsqrt(jnp.mean(a * a, axis=0, keepdims=True) + EPS)
        gain = jnp.concatenate([g_ref[rows, :]] * (tm // HEAD_DIM), axis=1)
        y = a * r * gain
        y1 = y[:ROT_HALF, :]
        y2 = y[ROT_HALF:ROT_DIM, :]
        y = jnp.concatenate([y1 * cos - y2 * sin, y2 * cos + y1 * sin, y[ROT_DIM:, :]], axis=0)
        out_ref[rows, :] = y.astype(BF16)


def _qk_proj_t(x2, norm_g, wt_bf, row_gain, rope_tabs):
    T, D = x2.shape
    N = row_gain.shape[0]
    tm, tn = PROJ_TM, PROJ_TN
    n_row_steps, n_col_steps = T // tm, N // tn
    assert n_col_steps >= 2

    def x_block(i, j):
        return (jnp.minimum(i + jnp.minimum(j, 1), n_row_steps - 1), 0)

    return pl.pallas_call(
        _qk_proj_t_kernel,
        grid=(n_row_steps, n_col_steps),
        in_specs=[pl.BlockSpec((tm, D), x_block),
                  pl.BlockSpec((1, D), lambda i, j: (0, 0)),
                  pl.BlockSpec((tn, D), lambda i, j: (j, 0)),
                  pl.BlockSpec((tn, HEAD_DIM), lambda i, j: (j, 0)),
                  pl.BlockSpec((ROT_HALF, tm), lambda i, j: (0, i)),
                  pl.BlockSpec((ROT_HALF, tm), lambda i, j: (0, i))],
        out_specs=[pl.BlockSpec((tn, tm), lambda i, j: (j, i)),
                   pl.BlockSpec((tm, D), lambda i, j: (i, 0))],
        out_shape=[jax.ShapeDtypeStruct((N, T), BF16), jax.ShapeDtypeStruct((T, D), BF16)],
        compiler_params=_compiler_params(("arbitrary", "arbitrary")),
        name="qk_proj_rope",
    )(x2, norm_g.reshape(1, D), wt_bf, row_gain, *rope_tabs)


def _vg_proj_kernel(*refs, with_f, n_col_steps):
    if with_f:
        h_ref, w_ref, wf_ref, fb_ref, out_ref, logf_ref = refs
        share = h_ref.shape[0] // n_col_steps
        rows = pl.ds(pl.multiple_of(pl.program_id(1) * share, share), share)
        f = _dot(h_ref[rows, :], wf_ref[...]) + fb_ref[...]
        logf_ref[rows, :] = jnp.minimum(f, 0.0) - jnp.log(1.0 + jnp.exp(-jnp.abs(f)))
    else:
        h_ref, w_ref, out_ref = refs
    out_ref[...] = _dot(h_ref[...], w_ref[...]).astype(BF16)


def _vg_proj(h, w_bf, col0, n_cols, f_col0=None, f_bias=None):
    T, D = h.shape
    tm, tn = PROJ_TM, PROJ_TN_FROM_H
    j0 = col0 // tn
    with_f = f_col0 is not None
    in_specs = [pl.BlockSpec((tm, D), lambda i, j: (i, 0)),
                pl.BlockSpec((D, tn), lambda i, j: (0, j0 + j))]
    args = [h, w_bf]
    out_specs = [pl.BlockSpec((tm, tn), lambda i, j: (i, j))]
    out_shape = [jax.ShapeDtypeStruct((T, n_cols), BF16)]
    if with_f:
        jf = f_col0 // HEAD_DIM
        in_specs += [pl.BlockSpec((D, HEAD_DIM), lambda i, j: (0, jf)),
                     pl.BlockSpec((1, HEAD_DIM), lambda i, j: (0, 0))]
        args += [w_bf, f_bias]
        out_specs.append(pl.BlockSpec((tm, HEAD_DIM), lambda i, j: (i, 0)))
        out_shape.append(jax.ShapeDtypeStruct((T, HEAD_DIM), F32))
    return pl.pallas_call(
        functools.partial(_vg_proj_kernel, with_f=with_f, n_col_steps=n_cols // tn),
        grid=(T // tm, n_cols // tn),
        in_specs=in_specs,
        out_specs=out_specs,
        out_shape=out_shape,
        compiler_params=_compiler_params(("parallel", "arbitrary")),
        name="vg_proj_fgate" if with_f else "vg_proj",
    )(*args)


def _outproj_kernel(*refs, with_next_norm):
    if with_next_norm:
        o_ref, w_ref, x_ref, ng_ref, out_ref, h_ref = refs
    else:
        o_ref, w_ref, x_ref, out_ref = refs
    y = x_ref[...] + _dot(o_ref[...], w_ref[...])
    out_ref[...] = y
    if with_next_norm:
        ms = jnp.mean(y * y, axis=-1, keepdims=True)
        h_ref[...] = (y * lax.rsqrt(ms + EPS) * ng_ref[...]).astype(BF16)


def _outproj(o_bf, w_bf, x2, next_norm_g=None):
    T, D = x2.shape
    K = o_bf.shape[1]
    tm = OUT_TM
    with_next_norm = next_norm_g is not None
    row_block = pl.BlockSpec((tm, D), lambda i: (i, 0))
    in_specs = [pl.BlockSpec((tm, K), lambda i: (i, 0)),
                pl.BlockSpec((K, D), lambda i: (0, 0)),
                row_block]
    args = [o_bf, w_bf, x2]
    out_specs = [row_block]
    out_shape = [jax.ShapeDtypeStruct((T, D), F32)]
    if with_next_norm:
        in_specs.append(pl.BlockSpec((1, D), lambda i: (0, 0)))
        args.append(next_norm_g.reshape(1, D))
        out_specs.append(row_block)
        out_shape.append(jax.ShapeDtypeStruct((T, D), BF16))
    return pl.pallas_call(
        functools.partial(_outproj_kernel, with_next_norm=with_next_norm),
        grid=(T // tm,),
        in_specs=in_specs,
        out_specs=out_specs,
        out_shape=out_shape,
        compiler_params=_compiler_params(("parallel",)),
        name="outproj_residual_norm" if with_next_norm else "outproj_residual",
    )(*args)


LOGIT_BOUND = 46.0


def _logits_are_bounded(q_gain, k_gain):
    bound = 1.02 * HEAD_DIM * jnp.max(jnp.abs(q_gain)) * jnp.max(jnp.abs(k_gain))
    return (bound < LOGIT_BOUND).astype(jnp.int32).reshape(1)


def _softmax_t(s_t, r0, mask_t, subtract_max, keep_f32=False):
    last = jnp.where(mask_t, s_t[r0:, :], MASKED)
    first = s_t[:r0, :] if r0 > 0 else None
    if subtract_max:
        m = jnp.max(last, axis=0, keepdims=True)
        if r0 > 0:
            m = jnp.maximum(m, jnp.max(first, axis=0, keepdims=True))
            first = first - m
        last = last - m
    e = jnp.exp2(last) if r0 == 0 else jnp.concatenate([jnp.exp2(first), jnp.exp2(last)], axis=0)
    return e if keep_f32 else e.astype(BF16)


def _fill_vt(vt_scr, v):
    d = v.shape[1]
    vt_scr[:d, :] = v.T
    vt_scr[d:, :] = jnp.ones((vt_scr.shape[0] - d, vt_scr.shape[1]), BF16)


def _silu(g):
    return g * (1.0 / (1.0 + jnp.exp(-g)))


def _diff_attn_kernel(bounded_ref, qt_ref, kt_ref, v_ref, g_ref, lq1_ref, lk1_ref, lq2_ref, lk2_ref,
                      sg_ref, out_ref, vt_scr, k_scr, *, lambda_init):
    S = v_ref.shape[0]
    tq = ATTN_TQ
    dv = 2 * HEAD_DIM
    n_heads = v_ref.shape[1] // dv
    lam = (jnp.exp(jnp.sum(lq1_ref[...] * lk1_ref[...], axis=-1, keepdims=True))
           - jnp.exp(jnp.sum(lq2_ref[...] * lk2_ref[...], axis=-1, keepdims=True))
           + lambda_init)
    key_chunk = lax.broadcasted_iota(jnp.int32, (tq, tq), 0) // CHUNK
    qry_chunk = lax.broadcasted_iota(jnp.int32, (tq, tq), 1) // CHUNK
    mask_t = key_chunk <= qry_chunk
    sub_gain = sg_ref[...] * (1.0 - lambda_init)

    maps = [slice(m * HEAD_DIM, (m + 1) * HEAD_DIM) for m in range(2 * n_heads)]

    def attend(subtract_max):
        for hh in range(n_heads):
            _fill_vt(vt_scr.at[hh], v_ref[:, hh * dv:(hh + 1) * dv])
        k_scr[...] = kt_ref[...].T
        for qi in range(S // tq):
            r0 = qi * tq
            kv = r0 + tq
            s_ts = [_dot(k_scr[:kv, sl], qt_ref[sl, r0:kv]) for sl in maps]
            if not subtract_max:
                e_ts = [_softmax_t(s_t, r0, mask_t, False, keep_f32=True) for s_t in s_ts]
                ws = [1.0 / jnp.sum(e_t, axis=0, keepdims=True) for e_t in e_ts]
                p_ts = [(e_ts[2 * hh] * ws[2 * hh]
                         - e_ts[2 * hh + 1] * (lam * ws[2 * hh + 1])).astype(BF16)
                        for hh in range(n_heads)]
                os_t = [_dot(vt_scr[hh, :dv, :kv], p_t) for hh, p_t in enumerate(p_ts)]
            else:
                p_ts = [_softmax_t(s_t, r0, mask_t, True) for s_t in s_ts]
                os_ = [_dot(vt_scr[m // 2, :, :kv], p_t) for m, p_t in enumerate(p_ts)]
                os_t = []
                for hh in range(n_heads):
                    o1, o2 = os_[2 * hh], os_[2 * hh + 1]
                    w1 = 1.0 / o1[dv:dv + 1, :]
                    w2 = lam * (1.0 / o2[dv:dv + 1, :])
                    os_t.append(o1[:dv, :] * w1 - o2[:dv, :] * w2)
            for hh in range(n_heads):
                o = os_t[hh].T
                o = o * lax.rsqrt(jnp.mean(o * o, axis=-1, keepdims=True) + EPS) * sub_gain
                o = o * _silu(g_ref[r0:kv, hh * dv:(hh + 1) * dv].astype(F32))
                out_ref[r0:kv, hh * dv:(hh + 1) * dv] = o.astype(BF16)

    bounded = bounded_ref[0] != 0
    pl.when(bounded)(lambda: attend(subtract_max=False))
    pl.when(jnp.logical_not(bounded))(lambda: attend(subtract_max=True))


def _diff_attention(bounded, qk_t, vg, B, S, lq1, lk1, lq2, lk2, sub_g, lambda_init):
    T = B * S
    hps = DIFF_HEADS_PER_STEP
    dv = 2 * HEAD_DIM
    width = hps * dv
    heads = (qk_t.shape[0] // 2) // width
    vec = lambda a: a.reshape(1, -1)
    small = lambda n: pl.BlockSpec((1, n), lambda b, h: (0, 0))
    return pl.pallas_call(
        functools.partial(_diff_attn_kernel, lambda_init=lambda_init),
        grid=(B, heads),
        in_specs=[pl.BlockSpec(memory_space=pltpu.SMEM),
                  pl.BlockSpec((width, S), lambda b, h: (h, b)),
                  pl.BlockSpec((width, S), lambda b, h: (heads + h, b)),
                  pl.BlockSpec((S, width), lambda b, h: (b, h)),
                  pl.BlockSpec((S, width), lambda b, h: (b, heads + h)),
                  small(HEAD_DIM), small(HEAD_DIM), small(HEAD_DIM), small(HEAD_DIM),
                  small(dv)],
        out_specs=pl.BlockSpec((S, width), lambda b, h: (b, h)),
        out_shape=jax.ShapeDtypeStruct((T, heads * width), BF16),
        scratch_shapes=[pltpu.VMEM((hps, dv + BF16_ROWS, S), BF16),
                        pltpu.VMEM((S, width), BF16)],
        compiler_params=_compiler_params(("parallel", "parallel")),
        name="diff_attention",
    )(bounded, qk_t, qk_t, vg, vg, vec(lq1), vec(lk1), vec(lq2), vec(lk2), vec(sub_g))


BIAS_PARTS = 3


QUERY_LANE0 = HEAD_DIM // 2


def _cumsum_kernel(logf_ref, kkey_ref, kqry_ref, *, n_heads):
    S = logf_ref.shape[0]
    row = lax.broadcasted_iota(jnp.int32, (CUM_BLK, CUM_BLK), 0)
    col = lax.broadcasted_iota(jnp.int32, (CUM_BLK, CUM_BLK), 1)
    tri = (col <= row).astype(F32)
    src = lax.broadcasted_iota(jnp.int32, (HEAD_DIM, HEAD_DIM), 0)
    dst = lax.broadcasted_iota(jnp.int32, (HEAD_DIM, HEAD_DIM), 1)
    is_head = src < n_heads
    spread = [jnp.where(is_head & (dst == BIAS_PARTS * src + part), 1.0,
                        jnp.where(is_head & (dst == QUERY_LANE0 + BIAS_PARTS * src + part),
                                  -1.0, 0.0)).astype(BF16)
              for part in range(BIAS_PARTS)]
    lane = lax.broadcasted_iota(jnp.int32, (CUM_BLK, HEAD_DIM), 1)
    key_lanes = lane < QUERY_LANE0
    ones_lanes = (lane >= QUERY_LANE0) & (lane < QUERY_LANE0 + BIAS_PARTS * n_heads)
    local = [jnp.dot(tri, logf_ref[kb * CUM_BLK:(kb + 1) * CUM_BLK, :], preferred_element_type=F32,
                     precision=lax.Precision.HIGHEST) for kb in range(S // CUM_BLK)]
    off = jnp.zeros((1, HEAD_DIM), F32)
    for kb in range(S // CUM_BLK):
        rows = slice(kb * CUM_BLK, (kb + 1) * CUM_BLK)
        c = local[kb] + off
        off = off + local[kb][CUM_BLK - 1:CUM_BLK, :]
        rest = c * (-LOG2E)
        packed = jnp.zeros((CUM_BLK, HEAD_DIM), F32)
        for part in range(BIAS_PARTS):
            piece = rest.astype(BF16)
            rest = rest - piece.astype(F32)
            packed = packed + _dot(piece, spread[part])
        kkey_ref[rows, :] = jnp.where(key_lanes, packed, jnp.where(ones_lanes, 1.0, 0.0)).astype(BF16)
        kqry_ref[rows, :] = jnp.where(key_lanes, 0.0, packed).astype(BF16)


def _forget_bias(logf, B, S, n_heads):
    T = B * S
    assert QUERY_LANE0 + BIAS_PARTS * n_heads <= HEAD_DIM and BIAS_PARTS * n_heads <= QUERY_LANE0
    table = jax.ShapeDtypeStruct((T, HEAD_DIM), BF16)
    return pl.pallas_call(
        functools.partial(_cumsum_kernel, n_heads=n_heads),
        grid=(B,),
        in_specs=[pl.BlockSpec((S, HEAD_DIM), lambda b: (b, 0))],
        out_specs=[pl.BlockSpec((S, HEAD_DIM), lambda b: (b, 0))] * 2,
        out_shape=[table, table],
        compiler_params=_compiler_params(("parallel",)),
        name="forget_prefix_sum",
    )(logf)


def _fox_attn_kernel(bounded_ref, q_ref, k_ref, v_ref, g_ref, kkey_ref, kqry_ref, out_ref, vt_scr):
    S = q_ref.shape[0]
    tq = ATTN_TQ
    heads_per_step = q_ref.shape[1] // HEAD_DIM
    key = lax.broadcasted_iota(jnp.int32, (tq, tq), 0)
    qry = lax.broadcasted_iota(jnp.int32, (tq, tq), 1)
    mask_t = key <= qry
    lane = lax.broadcasted_iota(jnp.int32, (tq, HEAD_DIM), 1)
    lanes = [slice(hh * HEAD_DIM, (hh + 1) * HEAD_DIM) for hh in range(heads_per_step)]
    heads = [pl.program_id(1) * heads_per_step + hh for hh in range(heads_per_step)]

    def attend(subtract_max):
        for hh, sl in enumerate(lanes):
            _fill_vt(vt_scr.at[hh], v_ref[:, sl])
        for qi in range(S // tq):
            r0 = qi * tq
            kv = r0 + tq
            s_ts = []
            for sl, head in zip(lanes, heads):
                pick = lane // BIAS_PARTS == head
                extra = jnp.where(pick, 1.0, 0.0).astype(BF16)
                if not subtract_max:
                    own = (lane >= QUERY_LANE0) & ((lane - QUERY_LANE0) // BIAS_PARTS == head)
                    extra = jnp.where(own, kqry_ref[r0:kv, :], extra)
                s_ts.append(_dot_nt(jnp.concatenate([k_ref[:kv, sl], kkey_ref[:kv, :]], axis=1),
                                    jnp.concatenate([q_ref[r0:kv, sl], extra], axis=1)))
            p_ts = [_softmax_t(s_t, r0, mask_t, subtract_max) for s_t in s_ts]
            os_ = [_dot(vt_scr[hh, :, :kv], p_t) for hh, p_t in enumerate(p_ts)]
            for sl, o in zip(lanes, os_):
                o = (o[:HEAD_DIM, :] * (1.0 / o[HEAD_DIM:HEAD_DIM + 1, :])).T
                o = o * _silu(g_ref[r0:kv, sl].astype(F32))
                out_ref[r0:kv, sl] = o.astype(BF16)

    bounded = bounded_ref[0] != 0
    pl.when(bounded)(lambda: attend(subtract_max=False))
    pl.when(jnp.logical_not(bounded))(lambda: attend(subtract_max=True))


def _fox_attention(bounded, qk, vg, kkey, kqry, B, S):
    T = B * S
    hps = FOX_HEADS_PER_STEP
    width = hps * HEAD_DIM
    groups = (qk.shape[1] // 2) // width
    table = pl.BlockSpec((S, HEAD_DIM), lambda b, h: (b, 0))
    return pl.pallas_call(
        _fox_attn_kernel,
        grid=(B, groups),
        in_specs=[pl.BlockSpec(memory_space=pltpu.SMEM),
                  pl.BlockSpec((S, width), lambda b, h: (b, h)),
                  pl.BlockSpec((S, width), lambda b, h: (b, groups + h)),
                  pl.BlockSpec((S, width), lambda b, h: (b, h)),
                  pl.BlockSpec((S, width), lambda b, h: (b, groups + h)),
                  table, table],
        out_specs=pl.BlockSpec((S, width), lambda b, h: (b, h)),
        out_shape=jax.ShapeDtypeStruct((T, groups * width), BF16),
        scratch_shapes=[pltpu.VMEM((hps, HEAD_DIM + BF16_ROWS, S), BF16)],
        compiler_params=_compiler_params(("parallel", "parallel")),
        name="fox_attention",
    )(bounded, qk, qk, vg, vg, kkey, kqry)


def kernel(x, positions, a_norm, a_w_in, a_q_norm, a_k_norm, a_lambda_q1, a_lambda_k1,
           a_lambda_q2, a_lambda_k2, a_sub_norm, a_w_out, b_norm, b_w_in, b_f_bias,
           b_q_norm, b_k_norm, b_w_out):
    B, S, D = x.shape
    T = B * S
    width = a_w_out.shape[1]
    heads_per_width = width // HEAD_DIM
    q_scale = HEAD_DIM ** -0.5 * LOG2E
    x2 = x.reshape(T, D)

    def col_gain(qg, kg):
        return jnp.concatenate([jnp.tile(qg * q_scale, heads_per_width),
                                jnp.tile(kg, heads_per_width)]).reshape(1, 2 * width)

    lambda_init0 = 0.8 - 0.6 * math.exp(-0.3 * 0)
    row_gain = jnp.broadcast_to(col_gain(a_q_norm[0], a_k_norm[0]).reshape(2 * width, 1),
                                (2 * width, HEAD_DIM))
    qk_t, h = _qk_proj_t(x2, a_norm[0], _cast_weight(a_w_in[0], 0, 2 * width, transpose_out=True),
                         row_gain, _rope_tables(positions))
    (vg,) = _vg_proj(h, _cast_weight(a_w_in[0], 2 * width, 2 * width), 0, 2 * width)
    o = _diff_attention(_logits_are_bounded(a_q_norm[0] * q_scale, a_k_norm[0]), qk_t, vg, B, S,
                        a_lambda_q1[0], a_lambda_k1[0], a_lambda_q2[0], a_lambda_k2[0],
                        a_sub_norm[0], lambda_init0)
    x2, h = _outproj(o, a_w_out[0].astype(BF16), x2, next_norm_g=b_norm[0])

    w1 = _cast_weight(b_w_in[0].T, 0, b_w_in.shape[2], w_is_transposed=True)
    n_f = b_w_in.shape[2] - 4 * width
    fb = jnp.pad(b_f_bias[0], (0, HEAD_DIM - n_f)).reshape(1, HEAD_DIM)
    qk = _qk_proj(h, w1, col_gain(b_q_norm[0], b_k_norm[0]))
    vg, logf = _vg_proj(h, w1, 2 * width, 2 * width, f_col0=4 * width, f_bias=fb)
    kkey, kqry = _forget_bias(logf, B, S, n_f)
    o = _fox_attention(_logits_are_bounded(b_q_norm[0] * q_scale, b_k_norm[0]), qk, vg, kkey, kqry,
                       B, S)
    (x2,) = _outproj(o, b_w_out[0].astype(BF16), x2)
    return x2.reshape(B, S, D)
```

```python
import functools
import math

import numpy as np
import jax
import jax.numpy as jnp
from jax import lax
from jax.experimental import pallas as pl
from jax.experimental.pallas import tpu as pltpu

F32 = jnp.float32
BF16 = jnp.bfloat16

EPS = 1e-6
CHUNK = 64
ROPE_THETA = 500000.0
HEAD_DIM = 128
BF16_ROWS = 16
ROT_DIM = HEAD_DIM // 4
ROT_HALF = ROT_DIM // 2
LOG2E = math.log2(math.e)
MASKED = -1e30

V7X_VMEM_LIMIT_BYTES = 56 * 1024 * 1024

PROJ_TM = 1024
PROJ_TN = 1024
PROJ_TN_FROM_H = 2048
OUT_TM = 512
ATTN_TQ = 256
CUM_BLK = 256
CAST_TN = 512
DIFF_HEADS_PER_STEP = 2
FOX_HEADS_PER_STEP = 4


def _compiler_params(semantics):
    return pltpu.CompilerParams(dimension_semantics=semantics,
                                vmem_limit_bytes=V7X_VMEM_LIMIT_BYTES)


def _dot(a, b):
    return jnp.dot(a, b, preferred_element_type=F32)


def _dot_nt(a, b):
    return lax.dot_general(a, b, (((1,), (1,)), ((), ())), preferred_element_type=F32)


def _cast_kernel(w_ref, out_ref, *, flip, col_axis, n_valid):
    w = w_ref[...]
    if n_valid is not None:
        col = (pl.program_id(0) * w.shape[col_axis]
               + lax.broadcasted_iota(jnp.int32, w.shape, col_axis))
        w = jnp.where(col < n_valid, w, 0.0)
    out_ref[...] = (w.T if flip else w).astype(BF16)


def _cast_weight(w, col0, n_cols, transpose_out=False, w_is_transposed=False):
    N, K = w.shape if w_is_transposed else w.shape[::-1]
    tc = CAST_TN
    j0 = col0 // tc
    n_cols = pl.cdiv(n_cols, tc) * tc
    n_valid = N - col0 if col0 + n_cols > N else None
    if w_is_transposed:
        in_spec = pl.BlockSpec((tc, K), lambda j: (j0 + j, 0))
    else:
        in_spec = pl.BlockSpec((K, tc), lambda j: (0, j0 + j))
    if transpose_out:
        out_spec = pl.BlockSpec((tc, K), lambda j: (j, 0))
        out_shape = jax.ShapeDtypeStruct((n_cols, K), BF16)
    else:
        out_spec = pl.BlockSpec((K, tc), lambda j: (0, j))
        out_shape = jax.ShapeDtypeStruct((K, n_cols), BF16)
    flip = transpose_out != w_is_transposed
    return pl.pallas_call(
        functools.partial(_cast_kernel, flip=flip, col_axis=0 if w_is_transposed else 1,
                          n_valid=n_valid),
        grid=(n_cols // tc,),
        in_specs=[in_spec],
        out_specs=out_spec,
        out_shape=out_shape,
        compiler_params=_compiler_params(("parallel",)),
        name="cast_weight_flip" if flip else "cast_weight",
    )(w)


def _rope_tables_kernel(pos_ref, invf_ref, cos_ref, sin_ref):
    ang = invf_ref[...] * pos_ref[...]
    cos_ref[...] = jnp.cos(ang)
    sin_ref[...] = jnp.sin(ang)


def _rope_tables(positions):
    T = positions.size
    tb = 2048
    pos = positions.reshape(1, T).astype(F32)
    invf = (ROPE_THETA ** (-np.arange(ROT_HALF, dtype=np.float32) / ROT_HALF)).astype(np.float32)
    invf = jnp.asarray(invf.reshape(ROT_HALF, 1))
    tab = jax.ShapeDtypeStruct((ROT_HALF, T), F32)
    return pl.pallas_call(
        _rope_tables_kernel,
        grid=(T // tb,),
        in_specs=[pl.BlockSpec((1, tb), lambda i: (0, i)),
                  pl.BlockSpec((ROT_HALF, 1), lambda i: (0, 0))],
        out_specs=[pl.BlockSpec((ROT_HALF, tb), lambda i: (0, i))] * 2,
        out_shape=[tab, tab],
        compiler_params=_compiler_params(("parallel",)),
        name="rope_tables",
    )(pos, invf)


def _rmsnorm_to(h_ref, x_ref, ng_ref):
    xf = x_ref[...]
    ms = jnp.mean(xf * xf, axis=-1, keepdims=True)
    h_ref[...] = (xf * lax.rsqrt(ms + EPS) * ng_ref[...]).astype(BF16)


def _qk_proj_kernel(h_ref, w_ref, cg_ref, out_ref):
    acc = _dot(h_ref[...], w_ref[...])
    for hh in range(acc.shape[1] // HEAD_DIM):
        sl = slice(hh * HEAD_DIM, (hh + 1) * HEAD_DIM)
        a = acc[:, sl]
        r = lax.rsqrt(jnp.mean(a * a, axis=-1, keepdims=True) + EPS)
        out_ref[:, sl] = (a * r * cg_ref[:, sl]).astype(BF16)


def _qk_proj(h, w_bf, col_gain):
    T, D = h.shape
    N = col_gain.shape[1]
    tm, tn = PROJ_TM, PROJ_TN_FROM_H
    return pl.pallas_call(
        _qk_proj_kernel,
        grid=(T // tm, N // tn),
        in_specs=[pl.BlockSpec((tm, D), lambda i, j: (i, 0)),
                  pl.BlockSpec((D, tn), lambda i, j: (0, j)),
                  pl.BlockSpec((1, tn), lambda i, j: (0, j))],
        out_specs=pl.BlockSpec((tm, tn), lambda i, j: (i, j)),
        out_shape=jax.ShapeDtypeStruct((T, N), BF16),
        compiler_params=_compiler_params(("parallel", "arbitrary")),
        name="qk_proj",
    )(h, w_bf, col_gain)


def _qk_proj_t_kernel(x_ref, ng_ref, wt_ref, g_ref, cos_ref, sin_ref, out_ref, h_ref):
    @pl.when(pl.program_id(1) == 0)
    def _():
        _rmsnorm_to(h_ref, x_ref, ng_ref)

    acc = _dot_nt(wt_ref[...], h_ref[...])
    tm = acc.shape[1]
    cos = cos_ref[...]
    sin = sin_ref[...]
    for hh in range(acc.shape[0] // HEAD_DIM):
        rows = slice(hh * HEAD_DIM, (hh + 1) * HEAD_DIM)
        a = acc[rows, :]
        r = lax.rsqrt(jnp.mean(a * a, axis=0, keepdims=True) + EPS)
        gain = jnp.concatenate([g_ref[rows, :]] * (tm // HEAD_DIM), axis=1)
        y = a * r * gain
        y1 = y[:ROT_HALF, :]
        y2 = y[ROT_HALF:ROT_DIM, :]
        y = jnp.concatenate([y1 * cos - y2 * sin, y2 * cos + y1 * sin, y[ROT_DIM:, :]], axis=0)
        out_ref[rows, :] = y.astype(BF16)


def _qk_proj_t(x2, norm_g, wt_bf, row_gain, rope_tabs):
    T, D = x2.shape
    N = row_gain.shape[0]
    tm, tn = PROJ_TM, PROJ_TN
    n_row_steps, n_col_steps = T // tm, N // tn
    assert n_col_steps >= 2

    def x_block(i, j):
        return (jnp.minimum(i + jnp.minimum(j, 1), n_row_steps - 1), 0)

    return pl.pallas_call(
        _qk_proj_t_kernel,
        grid=(n_row_steps, n_col_steps),
        in_specs=[pl.BlockSpec((tm, D), x_block),
                  pl.BlockSpec((1, D), lambda i, j: (0, 0)),
                  pl.BlockSpec((tn, D), lambda i, j: (j, 0)),
                  pl.BlockSpec((tn, HEAD_DIM), lambda i, j: (j, 0)),
                  pl.BlockSpec((ROT_HALF, tm), lambda i, j: (0, i)),
                  pl.BlockSpec((ROT_HALF, tm), lambda i, j: (0, i))],
        out_specs=[pl.BlockSpec((tn, tm), lambda i, j: (j, i)),
                   pl.BlockSpec((tm, D), lambda i, j: (i, 0))],
        out_shape=[jax.ShapeDtypeStruct((N, T), BF16), jax.ShapeDtypeStruct((T, D), BF16)],
        compiler_params=_compiler_params(("arbitrary", "arbitrary")),
        name="qk_proj_rope",
    )(x2, norm_g.reshape(1, D), wt_bf, row_gain, *rope_tabs)


def _vg_proj_kernel(*refs, with_f, n_col_steps):
    if with_f:
        h_ref, w_ref, wf_ref, fb_ref, out_ref, logf_ref = refs
        share = h_ref.shape[0] // n_col_steps
        rows = pl.ds(pl.multiple_of(pl.program_id(1) * share, share), share)
        f = _dot(h_ref[rows, :], wf_ref[...]) + fb_ref[...]
        logf_ref[rows, :] = jnp.minimum(f, 0.0) - jnp.log(1.0 + jnp.exp(-jnp.abs(f)))
    else:
        h_ref, w_ref, out_ref = refs
    out_ref[...] = _dot(h_ref[...], w_ref[...]).astype(BF16)


def _vg_proj(h, w_bf, col0, n_cols, f_col0=None, f_bias=None):
    T, D = h.shape
    tm, tn = PROJ_TM, PROJ_TN_FROM_H
    j0 = col0 // tn
    with_f = f_col0 is not None
    in_specs = [pl.BlockSpec((tm, D), lambda i, j: (i, 0)),
                pl.BlockSpec((D, tn), lambda i, j: (0, j0 + j))]
    args = [h, w_bf]
    out_specs = [pl.BlockSpec((tm, tn), lambda i, j: (i, j))]
    out_shape = [jax.ShapeDtypeStruct((T, n_cols), BF16)]
    if with_f:
        jf = f_col0 // HEAD_DIM
        in_specs += [pl.BlockSpec((D, HEAD_DIM), lambda i, j: (0, jf)),
                     pl.BlockSpec((1, HEAD_DIM), lambda i, j: (0, 0))]
        args += [w_bf, f_bias]
        out_specs.append(pl.BlockSpec((tm, HEAD_DIM), lambda i, j: (i, 0)))
        out_shape.append(jax.ShapeDtypeStruct((T, HEAD_DIM), F32))
    return pl.pallas_call(
        functools.partial(_vg_proj_kernel, with_f=with_f, n_col_steps=n_cols // tn),
        grid=(T // tm, n_cols // tn),
        in_specs=in_specs,
        out_specs=out_specs,
        out_shape=out_shape,
        compiler_params=_compiler_params(("parallel", "arbitrary")),
        name="vg_proj_fgate" if with_f else "vg_proj",
    )(*args)


def _outproj_kernel(*refs, with_next_norm):
    if with_next_norm:
        o_ref, w_ref, x_ref, ng_ref, out_ref, h_ref = refs
    else:
        o_ref, w_ref, x_ref, out_ref = refs
    y = x_ref[...] + _dot(o_ref[...], w_ref[...])
    out_ref[...] = y
    if with_next_norm:
        ms = jnp.mean(y * y, axis=-1, keepdims=True)
        h_ref[...] = (y * lax.rsqrt(ms + EPS) * ng_ref[...]).astype(BF16)


def _outproj(o_bf, w_bf, x2, next_norm_g=None):
    T, D = x2.shape
    K = o_bf.shape[1]
    tm = OUT_TM
    with_next_norm = next_norm_g is not None
    row_block = pl.BlockSpec((tm, D), lambda i: (i, 0))
    in_specs = [pl.BlockSpec((tm, K), lambda i: (i, 0)),
                pl.BlockSpec((K, D), lambda i: (0, 0)),
                row_block]
    args = [o_bf, w_bf, x2]
    out_specs = [row_block]
    out_shape = [jax.ShapeDtypeStruct((T, D), F32)]
    if with_next_norm:
        in_specs.append(pl.BlockSpec((1, D), lambda i: (0, 0)))
        args.append(next_norm_g.reshape(1, D))
        out_specs.append(row_block)
        out_shape.append(jax.ShapeDtypeStruct((T, D), BF16))
    return pl.pallas_call(
        functools.partial(_outproj_kernel, with_next_norm=with_next_norm),
        grid=(T // tm,),
        in_specs=in_specs,
        out_specs=out_specs,
        out_shape=out_shape,
        compiler_params=_compiler_params(("parallel",)),
        name="outproj_residual_norm" if with_next_norm else "outproj_residual",
    )(*args)


LOGIT_BOUND = 46.0


def _logits_are_bounded(q_gain, k_gain):
    bound = 1.02 * HEAD_DIM * jnp.max(jnp.abs(q_gain)) * jnp.max(jnp.abs(k_gain))
    return (bound < LOGIT_BOUND).astype(jnp.int32).reshape(1)


def _softmax_t(s_t, r0, mask_t, subtract_max, keep_f32=False):
    last = jnp.where(mask_t, s_t[r0:, :], MASKED)
    first = s_t[:r0, :] if r0 > 0 else None
    if subtract_max:
        m = jnp.max(last, axis=0, keepdims=True)
        if r0 > 0:
            m = jnp.maximum(m, jnp.max(first, axis=0, keepdims=True))
            first = first - m
        last = last - m
    e = jnp.exp2(last) if r0 == 0 else jnp.concatenate([jnp.exp2(first), jnp.exp2(last)], axis=0)
    return e if keep_f32 else e.astype(BF16)


def _fill_vt(vt_scr, v):
    d = v.shape[1]
    vt_scr[:d, :] = v.T
    vt_scr[d:, :] = jnp.ones((vt_scr.shape[0] - d, vt_scr.shape[1]), BF16)


def _silu(g):
    return g * (1.0 / (1.0 + jnp.exp(-g)))


def _diff_attn_kernel(bounded_ref, qt_ref, kt_ref, v_ref, g_ref, lq1_ref, lk1_ref, lq2_ref, lk2_ref,
                      sg_ref, out_ref, vt_scr, k_scr, *, lambda_init):
    S = v_ref.shape[0]
    tq = ATTN_TQ
    dv = 2 * HEAD_DIM
    n_heads = v_ref.shape[1] // dv
    lam = (jnp.exp(jnp.sum(lq1_ref[...] * lk1_ref[...], axis=-1, keepdims=True))
           - jnp.exp(jnp.sum(lq2_ref[...] * lk2_ref[...], axis=-1, keepdims=True))
           + lambda_init)
    key_chunk = lax.broadcasted_iota(jnp.int32, (tq, tq), 0) // CHUNK
    qry_chunk = lax.broadcasted_iota(jnp.int32, (tq, tq), 1) // CHUNK
    mask_t = key_chunk <= qry_chunk
    sub_gain = sg_ref[...] * (1.0 - lambda_init)

    maps = [slice(m * HEAD_DIM, (m + 1) * HEAD_DIM) for m in range(2 * n_heads)]

    def attend(subtract_max):
        for hh in range(n_heads):
            _fill_vt(vt_scr.at[hh], v_ref[:, hh * dv:(hh + 1) * dv])
        k_scr[...] = kt_ref[...].T
        for qi in range(S // tq):
            r0 = qi * tq
            kv = r0 + tq
            s_ts = [_dot(k_scr[:kv, sl], qt_ref[sl, r0:kv]) for sl in maps]
            if not subtract_max:
                e_ts = [_softmax_t(s_t, r0, mask_t, False, keep_f32=True) for s_t in s_ts]
                ws = [1.0 / jnp.sum(e_t, axis=0, keepdims=True) for e_t in e_ts]
                p_ts = [(e_ts[2 * hh] * ws[2 * hh]
                         - e_ts[2 * hh + 1] * (lam * ws[2 * hh + 1])).astype(BF16)
                        for hh in range(n_heads)]
                os_t = [_dot(vt_scr[hh, :dv, :kv], p_t) for hh, p_t in enumerate(p_ts)]
            else:
                p_ts = [_softmax_t(s_t, r0, mask_t, True) for s_t in s_ts]
                os_ = [_dot(vt_scr[m // 2, :, :kv], p_t) for m, p_t in enumerate(p_ts)]
                os_t = []
                for hh in range(n_heads):
                    o1, o2 = os_[2 * hh], os_[2 * hh + 1]
                    w1 = 1.0 / o1[dv:dv + 1, :]
                    w2 = lam * (1.0 / o2[dv:dv + 1, :])
                    os_t.append(o1[:dv, :] * w1 - o2[:dv, :] * w2)
            for hh in range(n_heads):
                o = os_t[hh].T
                o = o * lax.rsqrt(jnp.mean(o * o, axis=-1, keepdims=True) + EPS) * sub_gain
                o = o * _silu(g_ref[r0:kv, hh * dv:(hh + 1) * dv].astype(F32))
                out_ref[r0:kv, hh * dv:(hh + 1) * dv] = o.astype(BF16)

    bounded = bounded_ref[0] != 0
    pl.when(bounded)(lambda: attend(subtract_max=False))
    pl.when(jnp.logical_not(bounded))(lambda: attend(subtract_max=True))


def _diff_attention(bounded, qk_t, vg, B, S, lq1, lk1, lq2, lk2, sub_g, lambda_init):
    T = B * S
    hps = DIFF_HEADS_PER_STEP
    dv = 2 * HEAD_DIM
    width = hps * dv
    heads = (qk_t.shape[0] // 2) // width
    vec = lambda a: a.reshape(1, -1)
    small = lambda n: pl.BlockSpec((1, n), lambda b, h: (0, 0))
    return pl.pallas_call(
        functools.partial(_diff_attn_kernel, lambda_init=lambda_init),
        grid=(B, heads),
        in_specs=[pl.BlockSpec(memory_space=pltpu.SMEM),
                  pl.BlockSpec((width, S), lambda b, h: (h, b)),
                  pl.BlockSpec((width, S), lambda b, h: (heads + h, b)),
                  pl.BlockSpec((S, width), lambda b, h: (b, h)),
                  pl.BlockSpec((S, width), lambda b, h: (b, heads + h)),
                  small(HEAD_DIM), small(HEAD_DIM), small(HEAD_DIM), small(HEAD_DIM),
                  small(dv)],
        out_specs=pl.BlockSpec((S, width), lambda b, h: (b, h)),
        out_shape=jax.ShapeDtypeStruct((T, heads * width), BF16),
        scratch_shapes=[pltpu.VMEM((hps, dv + BF16_ROWS, S), BF16),
                        pltpu.VMEM((S, width), BF16)],
        compiler_params=_compiler_params(("parallel", "parallel")),
        name="diff_attention",
    )(bounded, qk_t, qk_t, vg, vg, vec(lq1), vec(lk1), vec(lq2), vec(lk2), vec(sub_g))


BIAS_PARTS = 3


QUERY_LANE0 = HEAD_DIM // 2


def _cumsum_kernel(logf_ref, kkey_ref, kqry_ref, *, n_heads):
    S = logf_ref.shape[0]
    row = lax.broadcasted_iota(jnp.int32, (CUM_BLK, CUM_BLK), 0)
    col = lax.broadcasted_iota(jnp.int32, (CUM_BLK, CUM_BLK), 1)
    tri = (col <= row).astype(F32)
    src = lax.broadcasted_iota(jnp.int32, (HEAD_DIM, HEAD_DIM), 0)
    dst = lax.broadcasted_iota(jnp.int32, (HEAD_DIM, HEAD_DIM), 1)
    is_head = src < n_heads
    spread = [jnp.where(is_head & (dst == BIAS_PARTS * src + part), 1.0,
                        jnp.where(is_head & (dst == QUERY_LANE0 + BIAS_PARTS * src + part),
                                  -1.0, 0.0)).astype(BF16)
              for part in range(BIAS_PARTS)]
    lane = lax.broadcasted_iota(jnp.int32, (CUM_BLK, HEAD_DIM), 1)
    key_lanes = lane < QUERY_LANE0
    ones_lanes = (lane >= QUERY_LANE0) & (lane < QUERY_LANE0 + BIAS_PARTS * n_heads)
    local = [jnp.dot(tri, logf_ref[kb * CUM_BLK:(kb + 1) * CUM_BLK, :], preferred_element_type=F32,
                     precision=lax.Precision.HIGHEST) for kb in range(S // CUM_BLK)]
    off = jnp.zeros((1, HEAD_DIM), F32)
    for kb in range(S // CUM_BLK):
        rows = slice(kb * CUM_BLK, (kb + 1) * CUM_BLK)
        c = local[kb] + off
        off = off + local[kb][CUM_BLK - 1:CUM_BLK, :]
        rest = c * (-LOG2E)
        packed = jnp.zeros((CUM_BLK, HEAD_DIM), F32)
        for part in range(BIAS_PARTS):
            piece = rest.astype(BF16)
            rest = rest - piece.astype(F32)
            packed = packed + _dot(piece, spread[part])
        kkey_ref[rows, :] = jnp.where(key_lanes, packed, jnp.where(ones_lanes, 1.0, 0.0)).astype(BF16)
        kqry_ref[rows, :] = jnp.where(key_lanes, 0.0, packed).astype(BF16)


def _forget_bias(logf, B, S, n_heads):
    T = B * S
    assert QUERY_LANE0 + BIAS_PARTS * n_heads <= HEAD_DIM and BIAS_PARTS * n_heads <= QUERY_LANE0
    table = jax.ShapeDtypeStruct((T, HEAD_DIM), BF16)
    return pl.pallas_call(
        functools.partial(_cumsum_kernel, n_heads=n_heads),
        grid=(B,),
        in_specs=[pl.BlockSpec((S, HEAD_DIM), lambda b: (b, 0))],
        out_specs=[pl.BlockSpec((S, HEAD_DIM), lambda b: (b, 0))] * 2,
        out_shape=[table, table],
        compiler_params=_compiler_params(("parallel",)),
        name="forget_prefix_sum",
    )(logf)


def _fox_attn_kernel(bounded_ref, q_ref, k_ref, v_ref, g_ref, kkey_ref, kqry_ref, out_ref, vt_scr):
    S = q_ref.shape[0]
    tq = ATTN_TQ
    heads_per_step = q_ref.shape[1] // HEAD_DIM
    key = lax.broadcasted_iota(jnp.int32, (tq, tq), 0)
    qry = lax.broadcasted_iota(jnp.int32, (tq, tq), 1)
    mask_t = key <= qry
    lane = lax.broadcasted_iota(jnp.int32, (tq, HEAD_DIM), 1)
    lanes = [slice(hh * HEAD_DIM, (hh + 1) * HEAD_DIM) for hh in range(heads_per_step)]
    heads = [pl.program_id(1) * heads_per_step + hh for hh in range(heads_per_step)]

    def attend(subtract_max):
        for hh, sl in enumerate(lanes):
            _fill_vt(vt_scr.at[hh], v_ref[:, sl])
        for qi in range(S // tq):
            r0 = qi * tq
            kv = r0 + tq
            s_ts = []
            for sl, head in zip(lanes, heads):
                pick = lane // BIAS_PARTS == head
                extra = jnp.where(pick, 1.0, 0.0).astype(BF16)
                if not subtract_max:
                    own = (lane >= QUERY_LANE0) & ((lane - QUERY_LANE0) // BIAS_PARTS == head)
                    extra = jnp.where(own, kqry_ref[r0:kv, :], extra)
                s_ts.append(_dot_nt(jnp.concatenate([k_ref[:kv, sl], kkey_ref[:kv, :]], axis=1),
                                    jnp.concatenate([q_ref[r0:kv, sl], extra], axis=1)))
            p_ts = [_softmax_t(s_t, r0, mask_t, subtract_max) for s_t in s_ts]
            os_ = [_dot(vt_scr[hh, :, :kv], p_t) for hh, p_t in enumerate(p_ts)]
            for sl, o in zip(lanes, os_):
                o = (o[:HEAD_DIM, :] * (1.0 / o[HEAD_DIM:HEAD_DIM + 1, :])).T
                o = o * _silu(g_ref[r0:kv, sl].astype(F32))
                out_ref[r0:kv, sl] = o.astype(BF16)

    bounded = bounded_ref[0] != 0
    pl.when(bounded)(lambda: attend(subtract_max=False))
    pl.when(jnp.logical_not(bounded))(lambda: attend(subtract_max=True))


def _fox_attention(bounded, qk, vg, kkey, kqry, B, S):
    T = B * S
    hps = FOX_HEADS_PER_STEP
    width = hps * HEAD_DIM
    groups = (qk.shape[1] // 2) // width
    table = pl.BlockSpec((S, HEAD_DIM), lambda b, h: (b, 0))
    return pl.pallas_call(
        _fox_attn_kernel,
        grid=(B, groups),
        in_specs=[pl.BlockSpec(memory_space=pltpu.SMEM),
                  pl.BlockSpec((S, width), lambda b, h: (b, h)),
                  pl.BlockSpec((S, width), lambda b, h: (b, groups + h)),
                  pl.BlockSpec((S, width), lambda b, h: (b, h)),
                  pl.BlockSpec((S, width), lambda b, h: (b, groups + h)),
                  table, table],
        out_specs=pl.BlockSpec((S, width), lambda b, h: (b, h)),
        out_shape=jax.ShapeDtypeStruct((T, groups * width), BF16),
        scratch_shapes=[pltpu.VMEM((hps, HEAD_DIM + BF16_ROWS, S), BF16)],
        compiler_params=_compiler_params(("parallel", "parallel")),
        name="fox_attention",
    )(bounded, qk, qk, vg, vg, kkey, kqry)


def kernel(x, positions, a_norm, a_w_in, a_q_norm, a_k_norm, a_lambda_q1, a_lambda_k1,
           a_lambda_q2, a_lambda_k2, a_sub_norm, a_w_out, b_norm, b_w_in, b_f_bias,
           b_q_norm, b_k_norm, b_w_out):
    B, S, D = x.shape
    T = B * S
    width = a_w_out.shape[1]
    heads_per_width = width // HEAD_DIM
    q_scale = HEAD_DIM ** -0.5 * LOG2E
    x2 = x.reshape(T, D)

    def col_gain(qg, kg):
        return jnp.concatenate([jnp.tile(qg * q_scale, heads_per_width),
                                jnp.tile(kg, heads_per_width)]).reshape(1, 2 * width)

    lambda_init0 = 0.8 - 0.6 * math.exp(-0.3 * 0)
    row_gain = jnp.broadcast_to(col_gain(a_q_norm[0], a_k_norm[0]).reshape(2 * width, 1),
                                (2 * width, HEAD_DIM))
    qk_t, h = _qk_proj_t(x2, a_norm[0], _cast_weight(a_w_in[0], 0, 2 * width, transpose_out=True),
                         row_gain, _rope_tables(positions))
    (vg,) = _vg_proj(h, _cast_weight(a_w_in[0], 2 * width, 2 * width), 0, 2 * width)
    o = _diff_attention(_logits_are_bounded(a_q_norm[0] * q_scale, a_k_norm[0]), qk_t, vg, B, S,
                        a_lambda_q1[0], a_lambda_k1[0], a_lambda_q2[0], a_lambda_k2[0],
                        a_sub_norm[0], lambda_init0)
    x2, h = _outproj(o, a_w_out[0].astype(BF16), x2, next_norm_g=b_norm[0])

    w1 = _cast_weight(b_w_in[0].T, 0, b_w_in.shape[2], w_is_transposed=True)
    n_f = b_w_in.shape[2] - 4 * width
    fb = jnp.pad(b_f_bias[0], (0, HEAD_DIM - n_f)).reshape(1, HEAD_DIM)
    qk = _qk_proj(h, w1, col_gain(b_q_norm[0], b_k_norm[0]))
    vg, logf = _vg_proj(h, w1, 2 * width, 2 * width, f_col0=4 * width, f_bias=fb)
    kkey, kqry = _forget_bias(logf, B, S, n_f)
    o = _fox_attention(_logits_are_bounded(b_q_norm[0] * q_scale, b_k_norm[0]), qk, vg, kkey, kqry,
                       B, S)
    (x2,) = _outproj(o, b_w_out[0].astype(BF16), x2)
    return x2.reshape(B, S, D)
```

```python
import functools
import math

import numpy as np
import jax
import jax.numpy as jnp
from jax import lax
from jax.experimental import pallas as pl
from jax.experimental.pallas import tpu as pltpu

F32 = jnp.float32
BF16 = jnp.bfloat16

EPS = 1e-6
CHUNK = 64
ROPE_THETA = 500000.0
HEAD_DIM = 128
BF16_ROWS = 16
ROT_DIM = HEAD_DIM // 4
ROT_HALF = ROT_DIM // 2
LOG2E = math.log2(math.e)
MASKED = -1e30

V7X_VMEM_LIMIT_BYTES = 56 * 1024 * 1024

PROJ_TM = 1024
PROJ_TN = 1024
PROJ_TN_FROM_H = 2048
OUT_TM = 512
ATTN_TQ = 256
CUM_BLK = 256
CAST_TN = 512
DIFF_HEADS_PER_STEP = 2
FOX_HEADS_PER_STEP = 4


def _compiler_params(semantics):
    return pltpu.CompilerParams(dimension_semantics=semantics,
                                vmem_limit_bytes=V7X_VMEM_LIMIT_BYTES)


def _dot(a, b):
    return jnp.dot(a, b, preferred_element_type=F32)


def _dot_nt(a, b):
    return lax.dot_general(a, b, (((1,), (1,)), ((), ())), preferred_element_type=F32)


def _cast_kernel(w_ref, out_ref, *, flip, col_axis, n_valid):
    w = w_ref[...]
    if n_valid is not None:
        col = (pl.program_id(0) * w.shape[col_axis]
               + lax.broadcasted_iota(jnp.int32, w.shape, col_axis))
        w = jnp.where(col < n_valid, w, 0.0)
    out_ref[...] = (w.T if flip else w).astype(BF16)


def _cast_weight(w, col0, n_cols, transpose_out=False, w_is_transposed=False):
    N, K = w.shape if w_is_transposed else w.shape[::-1]
    tc = CAST_TN
    j0 = col0 // tc
    n_cols = pl.cdiv(n_cols, tc) * tc
    n_valid = N - col0 if col0 + n_cols > N else None
    if w_is_transposed:
        in_spec = pl.BlockSpec((tc, K), lambda j: (j0 + j, 0))
    else:
        in_spec = pl.BlockSpec((K, tc), lambda j: (0, j0 + j))
    if transpose_out:
        out_spec = pl.BlockSpec((tc, K), lambda j: (j, 0))
        out_shape = jax.ShapeDtypeStruct((n_cols, K), BF16)
    else:
        out_spec = pl.BlockSpec((K, tc), lambda j: (0, j))
        out_shape = jax.ShapeDtypeStruct((K, n_cols), BF16)
    flip = transpose_out != w_is_transposed
    return pl.pallas_call(
        functools.partial(_cast_kernel, flip=flip, col_axis=0 if w_is_transposed else 1,
                          n_valid=n_valid),
        grid=(n_cols // tc,),
        in_specs=[in_spec],
        out_specs=out_spec,
        out_shape=out_shape,
        compiler_params=_compiler_params(("parallel",)),
        name="cast_weight_flip" if flip else "cast_weight",
    )(w)


def _rope_tables_kernel(pos_ref, invf_ref, cos_ref, sin_ref):
    ang = invf_ref[...] * pos_ref[...]
    cos_ref[...] = jnp.cos(ang)
    sin_ref[...] = jnp.sin(ang)


def _rope_tables(positions):
    T = positions.size
    tb = 2048
    pos = positions.reshape(1, T).astype(F32)
    invf = (ROPE_THETA ** (-np.arange(ROT_HALF, dtype=np.float32) / ROT_HALF)).astype(np.float32)
    invf = jnp.asarray(invf.reshape(ROT_HALF, 1))
    tab = jax.ShapeDtypeStruct((ROT_HALF, T), F32)
    return pl.pallas_call(
        _rope_tables_kernel,
        grid=(T // tb,),
        in_specs=[pl.BlockSpec((1, tb), lambda i: (0, i)),
                  pl.BlockSpec((ROT_HALF, 1), lambda i: (0, 0))],
        out_specs=[pl.BlockSpec((ROT_HALF, tb), lambda i: (0, i))] * 2,
        out_shape=[tab, tab],
        compiler_params=_compiler_params(("parallel",)),
        name="rope_tables",
    )(pos, invf)


def _rmsnorm_to(h_ref, x_ref, ng_ref):
    xf = x_ref[...]
    ms = jnp.mean(xf * xf, axis=-1, keepdims=True)
    h_ref[...] = (xf * lax.rsqrt(ms + EPS) * ng_ref[...]).astype(BF16)


def _qk_proj_kernel(h_ref, w_ref, cg_ref, out_ref):
    acc = _dot(h_ref[...], w_ref[...])
    for hh in range(acc.shape[1] // HEAD_DIM):
        sl = slice(hh * HEAD_DIM, (hh + 1) * HEAD_DIM)
        a = acc[:, sl]
        r = lax.rsqrt(jnp.mean(a * a, axis=-1, keepdims=True) + EPS)
        out_ref[:, sl] = (a * r * cg_ref[:, sl]).astype(BF16)


def _qk_proj(h, w_bf, col_gain):
    T, D = h.shape
    N = col_gain.shape[1]
    tm, tn = PROJ_TM, PROJ_TN_FROM_H
    return pl.pallas_call(
        _qk_proj_kernel,
        grid=(T // tm, N // tn),
        in_specs=[pl.BlockSpec((tm, D), lambda i, j: (i, 0)),
                  pl.BlockSpec((D, tn), lambda i, j: (0, j)),
                  pl.BlockSpec((1, tn), lambda i, j: (0, j))],
        out_specs=pl.BlockSpec((tm, tn), lambda i, j: (i, j)),
        out_shape=jax.ShapeDtypeStruct((T, N), BF16),
        compiler_params=_compiler_params(("parallel", "arbitrary")),
        name="qk_proj",
    )(h, w_bf, col_gain)


def _qk_proj_t_kernel(x_ref, ng_ref, wt_ref, g_ref, cos_ref, sin_ref, out_ref, h_ref):
    @pl.when(pl.program_id(1) == 0)
    def _():
        _rmsnorm_to(h_ref, x_ref, ng_ref)

    acc = _dot_nt(wt_ref[...], h_ref[...])
    tm = acc.shape[1]
    cos = cos_ref[...]
    sin = sin_ref[...]
    for hh in range(acc.shape[0] // HEAD_DIM):
        rows = slice(hh * HEAD_DIM, (hh + 1) * HEAD_DIM)
        a = acc[rows, :]
        r = lax.rsqrt(jnp.mean(a * a, axis=0, keepdims=True) + EPS)
        gain = jnp.concatenate([g_ref[rows, :]] * (tm // HEAD_DIM), axis=1)
        y = a * r * gain
        y1 = y[:ROT_HALF, :]
        y2 = y[ROT_HALF:ROT_DIM, :]
        y = jnp.concatenate([y1 * cos - y2 * sin, y2 * cos + y1 * sin, y[ROT_DIM:, :]], axis=0)
        out_ref[rows, :] = y.astype(BF16)


def _qk_proj_t(x2, norm_g, wt_bf, row_gain, rope_tabs):
    T, D = x2.shape
    N = row_gain.shape[0]
    tm, tn = PROJ_TM, PROJ_TN
    n_row_steps, n_col_steps = T // tm, N // tn
    assert n_col_steps >= 2

    def x_block(i, j):
        return (jnp.minimum(i + jnp.minimum(j, 1), n_row_steps - 1), 0)

    return pl.pallas_call(
        _qk_proj_t_kernel,
        grid=(n_row_steps, n_col_steps),
        in_specs=[pl.BlockSpec((tm, D), x_block),
                  pl.BlockSpec((1, D), lambda i, j: (0, 0)),
                  pl.BlockSpec((tn, D), lambda i, j: (j, 0)),
                  pl.BlockSpec((tn, HEAD_DIM), lambda i, j: (j, 0)),
                  pl.BlockSpec((ROT_HALF, tm), lambda i, j: (0, i)),
                  pl.BlockSpec((ROT_HALF, tm), lambda i, j: (0, i))],
        out_specs=[pl.BlockSpec((tn, tm), lambda i, j: (j, i)),
                   pl.BlockSpec((tm, D), lambda i, j: (i, 0))],
        out_shape=[jax.ShapeDtypeStruct((N, T), BF16), jax.ShapeDtypeStruct((T, D), BF16)],
        compiler_params=_compiler_params(("arbitrary", "arbitrary")),
        name="qk_proj_rope",
    )(x2, norm_g.reshape(1, D), wt_bf, row_gain, *rope_tabs)


def _vg_proj_kernel(*refs, with_f, n_col_steps):
    if with_f:
        h_ref, w_ref, wf_ref, fb_ref, out_ref, logf_ref = refs
        share = h_ref.shape[0] // n_col_steps
        rows = pl.ds(pl.multiple_of(pl.program_id(1) * share, share), share)
        f = _dot(h_ref[rows, :], wf_ref[...]) + fb_ref[...]
        logf_ref[rows, :] = jnp.minimum(f, 0.0) - jnp.log(1.0 + jnp.exp(-jnp.abs(f)))
    else:
        h_ref, w_ref, out_ref = refs
    out_ref[...] = _dot(h_ref[...], w_ref[...]).astype(BF16)


def _vg_proj(h, w_bf, col0, n_cols, f_col0=None, f_bias=None):
    T, D = h.shape
    tm, tn = PROJ_TM, PROJ_TN_FROM_H
    j0 = col0 // tn
    with_f = f_col0 is not None
    in_specs = [pl.BlockSpec((tm, D), lambda i, j: (i, 0)),
                pl.BlockSpec((D, tn), lambda i, j: (0, j0 + j))]
    args = [h, w_bf]
    out_specs = [pl.BlockSpec((tm, tn), lambda i, j: (i, j))]
    out_shape = [jax.ShapeDtypeStruct((T, n_cols), BF16)]
    if with_f:
        jf = f_col0 // HEAD_DIM
        in_specs += [pl.BlockSpec((D, HEAD_DIM), lambda i, j: (0, jf)),
                     pl.BlockSpec((1, HEAD_DIM), lambda i, j: (0, 0))]
        args += [w_bf, f_bias]
        out_specs.append(pl.BlockSpec((tm, HEAD_DIM), lambda i, j: (i, 0)))
        out_shape.append(jax.ShapeDtypeStruct((T, HEAD_DIM), F32))
    return pl.pallas_call(
        functools.partial(_vg_proj_kernel, with_f=with_f, n_col_steps=n_cols // tn),
        grid=(T // tm, n_cols // tn),
        in_specs=in_specs,
        out_specs=out_specs,
        out_shape=out_shape,
        compiler_params=_compiler_params(("parallel", "arbitrary")),
        name="vg_proj_fgate" if with_f else "vg_proj",
    )(*args)


def _outproj_kernel(*refs, with_next_norm):
    if with_next_norm:
        o_ref, w_ref, x_ref, ng_ref, out_ref, h_ref, w_bf = refs
    else:
        o_ref, w_ref, x_ref, out_ref, w_bf = refs

    @pl.when(pl.program_id(0) == 0)
    def _():
        w_bf[...] = w_ref[...].astype(BF16)

    y = x_ref[...] + _dot(o_ref[...], w_bf[...])
    out_ref[...] = y
    if with_next_norm:
        ms = jnp.mean(y * y, axis=-1, keepdims=True)
        h_ref[...] = (y * lax.rsqrt(ms + EPS) * ng_ref[...]).astype(BF16)


def _outproj(o_bf, w, x2, next_norm_g=None):
    T, D = x2.shape
    K = o_bf.shape[1]
    tm = OUT_TM
    with_next_norm = next_norm_g is not None
    row_block = pl.BlockSpec((tm, D), lambda i: (i, 0))
    in_specs = [pl.BlockSpec((tm, K), lambda i: (i, 0)),
                pl.BlockSpec((K, D), lambda i: (0, 0), pipeline_mode=pl.Buffered(1)),
                row_block]
    args = [o_bf, w, x2]
    out_specs = [row_block]
    out_shape = [jax.ShapeDtypeStruct((T, D), F32)]
    if with_next_norm:
        in_specs.append(pl.BlockSpec((1, D), lambda i: (0, 0)))
        args.append(next_norm_g.reshape(1, D))
        out_specs.append(row_block)
        out_shape.append(jax.ShapeDtypeStruct((T, D), BF16))
    return pl.pallas_call(
        functools.partial(_outproj_kernel, with_next_norm=with_next_norm),
        grid=(T // tm,),
        in_specs=in_specs,
        out_specs=out_specs,
        out_shape=out_shape,
        scratch_shapes=[pltpu.VMEM((K, D), BF16)],
        compiler_params=_compiler_params(("arbitrary",)),
        name="outproj_residual_norm" if with_next_norm else "outproj_residual",
    )(*args)


LOGIT_BOUND = 46.0


def _logits_are_bounded(q_gain, k_gain):
    bound = 1.02 * HEAD_DIM * jnp.max(jnp.abs(q_gain)) * jnp.max(jnp.abs(k_gain))
    return (bound < LOGIT_BOUND).astype(jnp.int32).reshape(1)


def _softmax_t(s_t, r0, mask_t, subtract_max, keep_f32=False):
    last = jnp.where(mask_t, s_t[r0:, :], MASKED)
    first = s_t[:r0, :] if r0 > 0 else None
    if subtract_max:
        m = jnp.max(last, axis=0, keepdims=True)
        if r0 > 0:
            m = jnp.maximum(m, jnp.max(first, axis=0, keepdims=True))
            first = first - m
        last = last - m
    e = jnp.exp2(last) if r0 == 0 else jnp.concatenate([jnp.exp2(first), jnp.exp2(last)], axis=0)
    return e if keep_f32 else e.astype(BF16)


def _fill_vt(vt_scr, v):
    d = v.shape[1]
    vt_scr[:d, :] = v.T
    vt_scr[d:, :] = jnp.ones((vt_scr.shape[0] - d, vt_scr.shape[1]), BF16)


def _silu(g):
    return g * (1.0 / (1.0 + jnp.exp(-g)))


def _diff_attn_kernel(bounded_ref, qt_ref, kt_ref, v_ref, g_ref, lq1_ref, lk1_ref, lq2_ref, lk2_ref,
                      sg_ref, out_ref, vt_scr, k_scr, *, lambda_init):
    S = v_ref.shape[0]
    tq = ATTN_TQ
    dv = 2 * HEAD_DIM
    n_heads = v_ref.shape[1] // dv
    lam = (jnp.exp(jnp.sum(lq1_ref[...] * lk1_ref[...], axis=-1, keepdims=True))
           - jnp.exp(jnp.sum(lq2_ref[...] * lk2_ref[...], axis=-1, keepdims=True))
           + lambda_init)
    key_chunk = lax.broadcasted_iota(jnp.int32, (tq, tq), 0) // CHUNK
    qry_chunk = lax.broadcasted_iota(jnp.int32, (tq, tq), 1) // CHUNK
    mask_t = key_chunk <= qry_chunk
    sub_gain = sg_ref[...] * (1.0 - lambda_init)

    maps = [slice(m * HEAD_DIM, (m + 1) * HEAD_DIM) for m in range(2 * n_heads)]

    def attend(subtract_max):
        for hh in range(n_heads):
            _fill_vt(vt_scr.at[hh], v_ref[:, hh * dv:(hh + 1) * dv])
        k_scr[...] = kt_ref[...].T
        for qi in range(S // tq):
            r0 = qi * tq
            kv = r0 + tq
            s_ts = [_dot(k_scr[:kv, sl], qt_ref[sl, r0:kv]) for sl in maps]
            if not subtract_max:
                e_ts = [_softmax_t(s_t, r0, mask_t, False, keep_f32=True) for s_t in s_ts]
                ws = [1.0 / jnp.sum(e_t, axis=0, keepdims=True) for e_t in e_ts]
                p_ts = [(e_ts[2 * hh] * ws[2 * hh]
                         - e_ts[2 * hh + 1] * (lam * ws[2 * hh + 1])).astype(BF16)
                        for hh in range(n_heads)]
                os_t = [_dot(vt_scr[hh, :dv, :kv], p_t) for hh, p_t in enumerate(p_ts)]
            else:
                p_ts = [_softmax_t(s_t, r0, mask_t, True) for s_t in s_ts]
                os_ = [_dot(vt_scr[m // 2, :, :kv], p_t) for m, p_t in enumerate(p_ts)]
                os_t = []
                for hh in range(n_heads):
                    o1, o2 = os_[2 * hh], os_[2 * hh + 1]
                    w1 = 1.0 / o1[dv:dv + 1, :]
                    w2 = lam * (1.0 / o2[dv:dv + 1, :])
                    os_t.append(o1[:dv, :] * w1 - o2[:dv, :] * w2)
            for hh in range(n_heads):
                o = os_t[hh].T
                o = o * lax.rsqrt(jnp.mean(o * o, axis=-1, keepdims=True) + EPS) * sub_gain
                o = o * _silu(g_ref[r0:kv, hh * dv:(hh + 1) * dv].astype(F32))
                out_ref[r0:kv, hh * dv:(hh + 1) * dv] = o.astype(BF16)

    bounded = bounded_ref[0] != 0
    pl.when(bounded)(lambda: attend(subtract_max=False))
    pl.when(jnp.logical_not(bounded))(lambda: attend(subtract_max=True))


def _diff_attention(bounded, qk_t, vg, B, S, lq1, lk1, lq2, lk2, sub_g, lambda_init):
    T = B * S
    hps = DIFF_HEADS_PER_STEP
    dv = 2 * HEAD_DIM
    width = hps * dv
    heads = (qk_t.shape[0] // 2) // width
    vec = lambda a: a.reshape(1, -1)
    small = lambda n: pl.BlockSpec((1, n), lambda b, h: (0, 0))
    return pl.pallas_call(
        functools.partial(_diff_attn_kernel, lambda_init=lambda_init),
        grid=(B, heads),
        in_specs=[pl.BlockSpec(memory_space=pltpu.SMEM),
                  pl.BlockSpec((width, S), lambda b, h: (h, b)),
                  pl.BlockSpec((width, S), lambda b, h: (heads + h, b)),
                  pl.BlockSpec((S, width), lambda b, h: (b, h)),
                  pl.BlockSpec((S, width), lambda b, h: (b, heads + h)),
                  small(HEAD_DIM), small(HEAD_DIM), small(HEAD_DIM), small(HEAD_DIM),
                  small(dv)],
        out_specs=pl.BlockSpec((S, width), lambda b, h: (b, h)),
        out_shape=jax.ShapeDtypeStruct((T, heads * width), BF16),
        scratch_shapes=[pltpu.VMEM((hps, dv + BF16_ROWS, S), BF16),
                        pltpu.VMEM((S, width), BF16)],
        compiler_params=_compiler_params(("parallel", "parallel")),
        name="diff_attention",
    )(bounded, qk_t, qk_t, vg, vg, vec(lq1), vec(lk1), vec(lq2), vec(lk2), vec(sub_g))


BIAS_PARTS = 3


QUERY_LANE0 = HEAD_DIM // 2


def _cumsum_kernel(logf_ref, kkey_ref, kqry_ref, *, n_heads):
    S = logf_ref.shape[0]
    row = lax.broadcasted_iota(jnp.int32, (CUM_BLK, CUM_BLK), 0)
    col = lax.broadcasted_iota(jnp.int32, (CUM_BLK, CUM_BLK), 1)
    tri = (col <= row).astype(BF16)
    src = lax.broadcasted_iota(jnp.int32, (HEAD_DIM, HEAD_DIM), 0)
    dst = lax.broadcasted_iota(jnp.int32, (HEAD_DIM, HEAD_DIM), 1)
    is_head = src < n_heads
    spread = [jnp.where(is_head & (dst == BIAS_PARTS * src + part), 1.0,
                        jnp.where(is_head & (dst == QUERY_LANE0 + BIAS_PARTS * src + part),
                                  -1.0, 0.0)).astype(BF16)
              for part in range(BIAS_PARTS)]
    lane = lax.broadcasted_iota(jnp.int32, (CUM_BLK, HEAD_DIM), 1)
    key_lanes = lane < QUERY_LANE0
    ones_lanes = (lane >= QUERY_LANE0) & (lane < QUERY_LANE0 + BIAS_PARTS * n_heads)
    def pieces(x):
        out = []
        for _ in range(BIAS_PARTS):
            out.append(x.astype(BF16))
            x = x - out[-1].astype(F32)
        return out

    local = [sum(_dot(tri, piece) for piece in pieces(logf_ref[kb * CUM_BLK:(kb + 1) * CUM_BLK, :]))
             for kb in range(S // CUM_BLK)]
    off = jnp.zeros((1, HEAD_DIM), F32)
    for kb in range(S // CUM_BLK):
        rows = slice(kb * CUM_BLK, (kb + 1) * CUM_BLK)
        c = local[kb] + off
        off = off + local[kb][CUM_BLK - 1:CUM_BLK, :]
        packed = sum(_dot(piece, spread[part])
                     for part, piece in enumerate(pieces(c * (-LOG2E))))
        kkey_ref[rows, :] = jnp.where(key_lanes, packed, jnp.where(ones_lanes, 1.0, 0.0)).astype(BF16)
        kqry_ref[rows, :] = jnp.where(key_lanes, 0.0, packed).astype(BF16)


def _forget_bias(logf, B, S, n_heads):
    T = B * S
    assert QUERY_LANE0 + BIAS_PARTS * n_heads <= HEAD_DIM and BIAS_PARTS * n_heads <= QUERY_LANE0
    table = jax.ShapeDtypeStruct((T, HEAD_DIM), BF16)
    return pl.pallas_call(
        functools.partial(_cumsum_kernel, n_heads=n_heads),
        grid=(B,),
        in_specs=[pl.BlockSpec((S, HEAD_DIM), lambda b: (b, 0))],
        out_specs=[pl.BlockSpec((S, HEAD_DIM), lambda b: (b, 0))] * 2,
        out_shape=[table, table],
        compiler_params=_compiler_params(("parallel",)),
        name="forget_prefix_sum",
    )(logf)


def _fox_attn_kernel(bounded_ref, q_ref, k_ref, v_ref, g_ref, kkey_ref, kqry_ref, out_ref, vt_scr):
    S = q_ref.shape[0]
    tq = ATTN_TQ
    heads_per_step = q_ref.shape[1] // HEAD_DIM
    key = lax.broadcasted_iota(jnp.int32, (tq, tq), 0)
    qry = lax.broadcasted_iota(jnp.int32, (tq, tq), 1)
    mask_t = key <= qry
    lane = lax.broadcasted_iota(jnp.int32, (tq, HEAD_DIM), 1)
    lanes = [slice(hh * HEAD_DIM, (hh + 1) * HEAD_DIM) for hh in range(heads_per_step)]
    heads = [pl.program_id(1) * heads_per_step + hh for hh in range(heads_per_step)]

    def attend(subtract_max):
        for hh, sl in enumerate(lanes):
            _fill_vt(vt_scr.at[hh], v_ref[:, sl])
        for qi in range(S // tq):
            r0 = qi * tq
            kv = r0 + tq
            s_ts = []
            for sl, head in zip(lanes, heads):
                pick = lane // BIAS_PARTS == head
                extra = jnp.where(pick, 1.0, 0.0).astype(BF16)
                if not subtract_max:
                    own = (lane >= QUERY_LANE0) & ((lane - QUERY_LANE0) // BIAS_PARTS == head)
                    extra = jnp.where(own, kqry_ref[r0:kv, :], extra)
                s_ts.append(_dot_nt(jnp.concatenate([k_ref[:kv, sl], kkey_ref[:kv, :]], axis=1),
                                    jnp.concatenate([q_ref[r0:kv, sl], extra], axis=1)))
            p_ts = [_softmax_t(s_t, r0, mask_t, subtract_max) for s_t in s_ts]
            os_ = [_dot(vt_scr[hh, :, :kv], p_t) for hh, p_t in enumerate(p_ts)]
            for sl, o in zip(lanes, os_):
                o = (o[:HEAD_DIM, :] * (1.0 / o[HEAD_DIM:HEAD_DIM + 1, :])).T
                o = o * _silu(g_ref[r0:kv, sl].astype(F32))
                out_ref[r0:kv, sl] = o.astype(BF16)

    bounded = bounded_ref[0] != 0
    pl.when(bounded)(lambda: attend(subtract_max=False))
    pl.when(jnp.logical_not(bounded))(lambda: attend(subtract_max=True))


def _fox_attention(bounded, qk, vg, kkey, kqry, B, S):
    T = B * S
    hps = FOX_HEADS_PER_STEP
    width = hps * HEAD_DIM
    groups = (qk.shape[1] // 2) // width
    table = pl.BlockSpec((S, HEAD_DIM), lambda b, h: (b, 0))
    return pl.pallas_call(
        _fox_attn_kernel,
        grid=(B, groups),
        in_specs=[pl.BlockSpec(memory_space=pltpu.SMEM),
                  pl.BlockSpec((S, width), lambda b, h: (b, h)),
                  pl.BlockSpec((S, width), lambda b, h: (b, groups + h)),
                  pl.BlockSpec((S, width), lambda b, h: (b, h)),
                  pl.BlockSpec((S, width), lambda b, h: (b, groups + h)),
                  table, table],
        out_specs=pl.BlockSpec((S, width), lambda b, h: (b, h)),
        out_shape=jax.ShapeDtypeStruct((T, groups * width), BF16),
        scratch_shapes=[pltpu.VMEM((hps, HEAD_DIM + BF16_ROWS, S), BF16)],
        compiler_params=_compiler_params(("parallel", "parallel")),
        name="fox_attention",
    )(bounded, qk, qk, vg, vg, kkey, kqry)


def kernel(x, positions, a_norm, a_w_in, a_q_norm, a_k_norm, a_lambda_q1, a_lambda_k1,
           a_lambda_q2, a_lambda_k2, a_sub_norm, a_w_out, b_norm, b_w_in, b_f_bias,
           b_q_norm, b_k_norm, b_w_out):
    B, S, D = x.shape
    T = B * S
    width = a_w_out.shape[1]
    heads_per_width = width // HEAD_DIM
    q_scale = HEAD_DIM ** -0.5 * LOG2E
    x2 = x.reshape(T, D)

    def col_gain(qg, kg):
        return jnp.concatenate([jnp.tile(qg * q_scale, heads_per_width),
                                jnp.tile(kg, heads_per_width)]).reshape(1, 2 * width)

    lambda_init0 = 0.8 - 0.6 * math.exp(-0.3 * 0)
    row_gain = jnp.broadcast_to(col_gain(a_q_norm[0], a_k_norm[0]).reshape(2 * width, 1),
                                (2 * width, HEAD_DIM))
    qk_t, h = _qk_proj_t(x2, a_norm[0], _cast_weight(a_w_in[0], 0, 2 * width, transpose_out=True),
                         row_gain, _rope_tables(positions))
    (vg,) = _vg_proj(h, _cast_weight(a_w_in[0], 2 * width, 2 * width), 0, 2 * width)
    o = _diff_attention(_logits_are_bounded(a_q_norm[0] * q_scale, a_k_norm[0]), qk_t, vg, B, S,
                        a_lambda_q1[0], a_lambda_k1[0], a_lambda_q2[0], a_lambda_k2[0],
                        a_sub_norm[0], lambda_init0)
    x2, h = _outproj(o, a_w_out[0], x2, next_norm_g=b_norm[0])

    w1 = _cast_weight(b_w_in[0].T, 0, b_w_in.shape[2], w_is_transposed=True)
    n_f = b_w_in.shape[2] - 4 * width
    fb = jnp.pad(b_f_bias[0], (0, HEAD_DIM - n_f)).reshape(1, HEAD_DIM)
    qk = _qk_proj(h, w1, col_gain(b_q_norm[0], b_k_norm[0]))
    vg, logf = _vg_proj(h, w1, 2 * width, 2 * width, f_col0=4 * width, f_bias=fb)
    kkey, kqry = _forget_bias(logf, B, S, n_f)
    o = _fox_attention(_logits_are_bounded(b_q_norm[0] * q_scale, b_k_norm[0]), qk, vg, kkey, kqry,
                       B, S)
    (x2,) = _outproj(o, b_w_out[0], x2)
    return x2.reshape(B, S, D)
```

```python
import functools
import math

import numpy as np
import jax
import jax.numpy as jnp
from jax import lax
from jax.experimental import pallas as pl
from jax.experimental.pallas import tpu as pltpu

F32 = jnp.float32
BF16 = jnp.bfloat16

EPS = 1e-6
CHUNK = 64
ROPE_THETA = 500000.0
HEAD_DIM = 128
BF16_ROWS = 16
ROT_DIM = HEAD_DIM // 4
ROT_HALF = ROT_DIM // 2
LOG2E = math.log2(math.e)
MASKED = -1e30

V7X_VMEM_LIMIT_BYTES = 56 * 1024 * 1024

PROJ_TM = 1024
PROJ_TN = 1024
PROJ_TN_FROM_H = 2048
OUT_TM = 512
ATTN_TQ = 256
CUM_BLK = 256
CAST_TN = 512
SIDE_ROW_ALIGN = BF16_ROWS
DIFF_HEADS_PER_STEP = 2
FOX_HEADS_PER_STEP = 4


def _compiler_params(semantics):
    return pltpu.CompilerParams(dimension_semantics=semantics,
                                vmem_limit_bytes=V7X_VMEM_LIMIT_BYTES)


def _dot(a, b):
    return jnp.dot(a, b, preferred_element_type=F32)


def _dot_nt(a, b):
    return lax.dot_general(a, b, (((1,), (1,)), ((), ())), preferred_element_type=F32)


def _cast_kernel(w_ref, out_ref, *, flip, col_axis, n_valid):
    w = w_ref[...]
    if n_valid is not None:
        col = (pl.program_id(0) * w.shape[col_axis]
               + lax.broadcasted_iota(jnp.int32, w.shape, col_axis))
        w = jnp.where(col < n_valid, w, 0.0)
    out_ref[...] = (w.T if flip else w).astype(BF16)


def _cast_weight(w, col0, n_cols, transpose_out=False, w_is_transposed=False):
    N, K = w.shape if w_is_transposed else w.shape[::-1]
    tc = CAST_TN
    j0 = col0 // tc
    n_cols = pl.cdiv(n_cols, tc) * tc
    n_valid = N - col0 if col0 + n_cols > N else None
    if w_is_transposed:
        in_spec = pl.BlockSpec((tc, K), lambda j: (j0 + j, 0))
    else:
        in_spec = pl.BlockSpec((K, tc), lambda j: (0, j0 + j))
    if transpose_out:
        out_spec = pl.BlockSpec((tc, K), lambda j: (j, 0))
        out_shape = jax.ShapeDtypeStruct((n_cols, K), BF16)
    else:
        out_spec = pl.BlockSpec((K, tc), lambda j: (0, j))
        out_shape = jax.ShapeDtypeStruct((K, n_cols), BF16)
    flip = transpose_out != w_is_transposed
    return pl.pallas_call(
        functools.partial(_cast_kernel, flip=flip, col_axis=0 if w_is_transposed else 1,
                          n_valid=n_valid),
        grid=(n_cols // tc,),
        in_specs=[in_spec],
        out_specs=out_spec,
        out_shape=out_shape,
        compiler_params=_compiler_params(("parallel",)),
        name="cast_weight_flip" if flip else "cast_weight",
    )(w)


def _rope_tables_kernel(pos_ref, invf_ref, cos_ref, sin_ref):
    ang = invf_ref[...] * pos_ref[...]
    cos_ref[...] = jnp.cos(ang)
    sin_ref[...] = jnp.sin(ang)


def _rope_tables(positions):
    T = positions.size
    tb = 2048
    pos = positions.reshape(1, T).astype(F32)
    invf = (ROPE_THETA ** (-np.arange(ROT_HALF, dtype=np.float32) / ROT_HALF)).astype(np.float32)
    invf = jnp.asarray(invf.reshape(ROT_HALF, 1))
    tab = jax.ShapeDtypeStruct((ROT_HALF, T), F32)
    return pl.pallas_call(
        _rope_tables_kernel,
        grid=(T // tb,),
        in_specs=[pl.BlockSpec((1, tb), lambda i: (0, i)),
                  pl.BlockSpec((ROT_HALF, 1), lambda i: (0, 0))],
        out_specs=[pl.BlockSpec((ROT_HALF, tb), lambda i: (0, i))] * 2,
        out_shape=[tab, tab],
        compiler_params=_compiler_params(("parallel",)),
        name="rope_tables",
    )(pos, invf)


def _rmsnorm_to(h_ref, x_ref, ng_ref):
    xf = x_ref[...]
    ms = jnp.mean(xf * xf, axis=-1, keepdims=True)
    h_ref[...] = (xf * lax.rsqrt(ms + EPS) * ng_ref[...]).astype(BF16)


def _qk_proj_kernel(h_ref, wt_ref, cg_ref, out_ref):
    acc = _dot_nt(h_ref[...], wt_ref[...])
    for hh in range(acc.shape[1] // HEAD_DIM):
        sl = slice(hh * HEAD_DIM, (hh + 1) * HEAD_DIM)
        a = acc[:, sl]
        r = lax.rsqrt(jnp.mean(a * a, axis=-1, keepdims=True) + EPS)
        out_ref[:, sl] = (a * r * cg_ref[:, sl]).astype(BF16)


def _qk_proj(h, wt_bf, col_gain):
    T, D = h.shape
    N = col_gain.shape[1]
    tm, tn = PROJ_TM, PROJ_TN_FROM_H
    return pl.pallas_call(
        _qk_proj_kernel,
        grid=(T // tm, N // tn),
        in_specs=[pl.BlockSpec((tm, D), lambda i, j: (i, 0)),
                  pl.BlockSpec((tn, D), lambda i, j: (j, 0)),
                  pl.BlockSpec((1, tn), lambda i, j: (0, j))],
        out_specs=pl.BlockSpec((tm, tn), lambda i, j: (i, j)),
        out_shape=jax.ShapeDtypeStruct((T, N), BF16),
        compiler_params=_compiler_params(("parallel", "arbitrary")),
        name="qk_proj",
    )(h, wt_bf, col_gain)


def _qk_proj_t_kernel(x_ref, ng_ref, wt_ref, g_ref, cos_ref, sin_ref, out_ref, h_ref):
    @pl.when(pl.program_id(1) == 0)
    def _():
        _rmsnorm_to(h_ref, x_ref, ng_ref)

    acc = _dot_nt(wt_ref[...], h_ref[...])
    tm = acc.shape[1]
    cos = cos_ref[...]
    sin = sin_ref[...]
    for hh in range(acc.shape[0] // HEAD_DIM):
        rows = slice(hh * HEAD_DIM, (hh + 1) * HEAD_DIM)
        a = acc[rows, :]
        r = lax.rsqrt(jnp.mean(a * a, axis=0, keepdims=True) + EPS)
        gain = jnp.concatenate([g_ref[rows, :]] * (tm // HEAD_DIM), axis=1)
        y = a * r * gain
        y1 = y[:ROT_HALF, :]
        y2 = y[ROT_HALF:ROT_DIM, :]
        y = jnp.concatenate([y1 * cos - y2 * sin, y2 * cos + y1 * sin, y[ROT_DIM:, :]], axis=0)
        out_ref[rows, :] = y.astype(BF16)


def _qk_proj_t(x2, norm_g, wt_bf, row_gain, rope_tabs):
    T, D = x2.shape
    N = row_gain.shape[0]
    tm, tn = PROJ_TM, PROJ_TN
    n_row_steps, n_col_steps = T // tm, N // tn
    assert n_col_steps >= 2

    def x_block(i, j):
        return (jnp.minimum(i + jnp.minimum(j, 1), n_row_steps - 1), 0)

    return pl.pallas_call(
        _qk_proj_t_kernel,
        grid=(n_row_steps, n_col_steps),
        in_specs=[pl.BlockSpec((tm, D), x_block),
                  pl.BlockSpec((1, D), lambda i, j: (0, 0)),
                  pl.BlockSpec((tn, D), lambda i, j: (j, 0)),
                  pl.BlockSpec((tn, HEAD_DIM), lambda i, j: (j, 0)),
                  pl.BlockSpec((ROT_HALF, tm), lambda i, j: (0, i)),
                  pl.BlockSpec((ROT_HALF, tm), lambda i, j: (0, i))],
        out_specs=[pl.BlockSpec((tn, tm), lambda i, j: (j, i)),
                   pl.BlockSpec((tm, D), lambda i, j: (i, 0))],
        out_shape=[jax.ShapeDtypeStruct((N, T), BF16), jax.ShapeDtypeStruct((T, D), BF16)],
        compiler_params=_compiler_params(("arbitrary", "arbitrary")),
        name="qk_proj_rope",
    )(x2, norm_g.reshape(1, D), wt_bf, row_gain, *rope_tabs)


def _vg_proj_kernel(*refs, with_f, n_col_steps, w_is_transposed, side_valid_rows):
    refs = list(refs)
    h_ref, w_ref = refs[:2]
    mm = _dot_nt if w_is_transposed else _dot
    pos = 2
    if with_f:
        wf_ref, fb_ref = refs[pos:pos + 2]
        pos += 2
    if side_valid_rows is not None:
        side_ref = refs[pos]
        pos += 1
    out_ref = refs[pos]
    pos += 1
    if with_f:
        logf_ref = refs[pos]
        pos += 1
        share = h_ref.shape[0] // n_col_steps
        rows = pl.ds(pl.multiple_of(pl.program_id(1) * share, share), share)
        f = mm(h_ref[rows, :], wf_ref[...]) + fb_ref[...]
        logf_ref[rows, :] = jnp.minimum(f, 0.0) - jnp.log(1.0 + jnp.exp(-jnp.abs(f)))
    if side_valid_rows is not None:
        side_out_ref = refs[pos]
        blk = side_ref.shape[0]
        step = pl.program_id(0) * n_col_steps + pl.program_id(1)
        row = step * blk + lax.broadcasted_iota(jnp.int32, side_ref.shape, 0)
        side_out_ref[...] = jnp.where(row < side_valid_rows, side_ref[...], 0.0).astype(BF16)
    out_ref[...] = mm(h_ref[...], w_ref[...]).astype(BF16)


def _vg_proj(h, w_bf, col0, n_cols, f_col0=None, f_bias=None, w_is_transposed=False, side_w=None):
    T, D = h.shape
    tm, tn = PROJ_TM, PROJ_TN_FROM_H
    j0 = col0 // tn
    n_row_steps, n_col_steps = T // tm, n_cols // tn
    with_f = f_col0 is not None
    if w_is_transposed:
        w_spec = pl.BlockSpec((tn, D), lambda i, j: (j0 + j, 0))
    else:
        w_spec = pl.BlockSpec((D, tn), lambda i, j: (0, j0 + j))
    in_specs = [pl.BlockSpec((tm, D), lambda i, j: (i, 0)), w_spec]
    args = [h, w_bf]
    out_specs = [pl.BlockSpec((tm, tn), lambda i, j: (i, j))]
    out_shape = [jax.ShapeDtypeStruct((T, n_cols), BF16)]
    if with_f:
        jf = f_col0 // HEAD_DIM
        wf_spec = (pl.BlockSpec((HEAD_DIM, D), lambda i, j: (jf, 0)) if w_is_transposed
                   else pl.BlockSpec((D, HEAD_DIM), lambda i, j: (0, jf)))
        in_specs += [wf_spec, pl.BlockSpec((1, HEAD_DIM), lambda i, j: (0, 0))]
        args += [w_bf, f_bias]
    side_valid_rows = None
    if side_w is not None:
        side_valid_rows = side_w.shape[0]
        n_steps = n_row_steps * n_col_steps
        blk = pl.cdiv(pl.cdiv(side_valid_rows, n_steps), SIDE_ROW_ALIGN) * SIDE_ROW_ALIGN
        last_blk = (side_valid_rows - 1) // blk
        in_specs.append(pl.BlockSpec(
            (blk, D), lambda i, j: (jnp.minimum(i * n_col_steps + j, last_blk), 0)))
        args.append(side_w)
    if with_f:
        out_specs.append(pl.BlockSpec((tm, HEAD_DIM), lambda i, j: (i, 0)))
        out_shape.append(jax.ShapeDtypeStruct((T, HEAD_DIM), F32))
    if side_w is not None:
        out_specs.append(pl.BlockSpec((blk, D), lambda i, j: (i * n_col_steps + j, 0)))
        out_shape.append(jax.ShapeDtypeStruct((blk * n_steps, D), BF16))
    return pl.pallas_call(
        functools.partial(_vg_proj_kernel, with_f=with_f, n_col_steps=n_col_steps,
                          w_is_transposed=w_is_transposed, side_valid_rows=side_valid_rows),
        grid=(n_row_steps, n_col_steps),
        in_specs=in_specs,
        out_specs=out_specs,
        out_shape=out_shape,
        compiler_params=_compiler_params(("parallel", "arbitrary")),
        name="vg_proj_fgate" if with_f else "vg_proj",
    )(*args)


def _outproj_kernel(*refs, with_next_norm):
    if with_next_norm:
        o_ref, w_ref, x_ref, ng_ref, out_ref, h_ref, w_bf = refs
    else:
        o_ref, w_ref, x_ref, out_ref, w_bf = refs

    @pl.when(pl.program_id(0) == 0)
    def _():
        w_bf[...] = w_ref[...].astype(BF16)

    y = x_ref[...] + _dot(o_ref[...], w_bf[...])
    out_ref[...] = y
    if with_next_norm:
        ms = jnp.mean(y * y, axis=-1, keepdims=True)
        h_ref[...] = (y * lax.rsqrt(ms + EPS) * ng_ref[...]).astype(BF16)


def _outproj(o_bf, w, x2, next_norm_g=None):
    T, D = x2.shape
    K = o_bf.shape[1]
    tm = OUT_TM
    with_next_norm = next_norm_g is not None
    row_block = pl.BlockSpec((tm, D), lambda i: (i, 0))
    in_specs = [pl.BlockSpec((tm, K), lambda i: (i, 0)),
                pl.BlockSpec((K, D), lambda i: (0, 0), pipeline_mode=pl.Buffered(1)),
                row_block]
    args = [o_bf, w, x2]
    out_specs = [row_block]
    out_shape = [jax.ShapeDtypeStruct((T, D), F32)]
    if with_next_norm:
        in_specs.append(pl.BlockSpec((1, D), lambda i: (0, 0)))
        args.append(next_norm_g.reshape(1, D))
        out_specs.append(row_block)
        out_shape.append(jax.ShapeDtypeStruct((T, D), BF16))
    return pl.pallas_call(
        functools.partial(_outproj_kernel, with_next_norm=with_next_norm),
        grid=(T // tm,),
        in_specs=in_specs,
        out_specs=out_specs,
        out_shape=out_shape,
        scratch_shapes=[pltpu.VMEM((K, D), BF16)],
        compiler_params=_compiler_params(("arbitrary",)),
        name="outproj_residual_norm" if with_next_norm else "outproj_residual",
    )(*args)


LOGIT_BOUND = 46.0


def _logits_are_bounded(q_gain, k_gain):
    bound = 1.02 * HEAD_DIM * jnp.max(jnp.abs(q_gain)) * jnp.max(jnp.abs(k_gain))
    return (bound < LOGIT_BOUND).astype(jnp.int32).reshape(1)


def _softmax_t(s_t, r0, mask_t, subtract_max, keep_f32=False):
    last = jnp.where(mask_t, s_t[r0:, :], MASKED)
    first = s_t[:r0, :] if r0 > 0 else None
    if subtract_max:
        m = jnp.max(last, axis=0, keepdims=True)
        if r0 > 0:
            m = jnp.maximum(m, jnp.max(first, axis=0, keepdims=True))
            first = first - m
        last = last - m
    e = jnp.exp2(last) if r0 == 0 else jnp.concatenate([jnp.exp2(first), jnp.exp2(last)], axis=0)
    return e if keep_f32 else e.astype(BF16)


def _fill_vt(vt_scr, v):
    d = v.shape[1]
    vt_scr[:d, :] = v.T
    vt_scr[d:, :] = jnp.ones((vt_scr.shape[0] - d, vt_scr.shape[1]), BF16)


def _silu(g):
    return g * (1.0 / (1.0 + jnp.exp(-g)))


def _diff_attn_kernel(bounded_ref, qt_ref, kt_ref, v_ref, g_ref, lq1_ref, lk1_ref, lq2_ref, lk2_ref,
                      sg_ref, out_ref, vt_scr, k_scr, *, lambda_init):
    S = v_ref.shape[0]
    tq = ATTN_TQ
    dv = 2 * HEAD_DIM
    n_heads = v_ref.shape[1] // dv
    lam = (jnp.exp(jnp.sum(lq1_ref[...] * lk1_ref[...], axis=-1, keepdims=True))
           - jnp.exp(jnp.sum(lq2_ref[...] * lk2_ref[...], axis=-1, keepdims=True))
           + lambda_init)
    key_chunk = lax.broadcasted_iota(jnp.int32, (tq, tq), 0) // CHUNK
    qry_chunk = lax.broadcasted_iota(jnp.int32, (tq, tq), 1) // CHUNK
    mask_t = key_chunk <= qry_chunk
    sub_gain = sg_ref[...] * (1.0 - lambda_init)

    maps = [slice(m * HEAD_DIM, (m + 1) * HEAD_DIM) for m in range(2 * n_heads)]

    def attend(subtract_max):
        for hh in range(n_heads):
            _fill_vt(vt_scr.at[hh], v_ref[:, hh * dv:(hh + 1) * dv])
        k_scr[...] = kt_ref[...].T
        for qi in range(S // tq):
            r0 = qi * tq
            kv = r0 + tq
            s_ts = [_dot(k_scr[:kv, sl], qt_ref[sl, r0:kv]) for sl in maps]
            if not subtract_max:
                e_ts = [_softmax_t(s_t, r0, mask_t, False, keep_f32=True) for s_t in s_ts]
                ws = [1.0 / jnp.sum(e_t, axis=0, keepdims=True) for e_t in e_ts]
                p_ts = [(e_ts[2 * hh] * ws[2 * hh]
                         - e_ts[2 * hh + 1] * (lam * ws[2 * hh + 1])).astype(BF16)
                        for hh in range(n_heads)]
                os_t = [_dot(vt_scr[hh, :dv, :kv], p_t) for hh, p_t in enumerate(p_ts)]
            else:
                p_ts = [_softmax_t(s_t, r0, mask_t, True) for s_t in s_ts]
                os_ = [_dot(vt_scr[m // 2, :, :kv], p_t) for m, p_t in enumerate(p_ts)]
                os_t = []
                for hh in range(n_heads):
                    o1, o2 = os_[2 * hh], os_[2 * hh + 1]
                    w1 = 1.0 / o1[dv:dv + 1, :]
                    w2 = lam * (1.0 / o2[dv:dv + 1, :])
                    os_t.append(o1[:dv, :] * w1 - o2[:dv, :] * w2)
            for hh in range(n_heads):
                o = os_t[hh].T
                o = o * lax.rsqrt(jnp.mean(o * o, axis=-1, keepdims=True) + EPS) * sub_gain
                o = o * _silu(g_ref[r0:kv, hh * dv:(hh + 1) * dv].astype(F32))
                out_ref[r0:kv, hh * dv:(hh + 1) * dv] = o.astype(BF16)

    bounded = bounded_ref[0] != 0
    pl.when(bounded)(lambda: attend(subtract_max=False))
    pl.when(jnp.logical_not(bounded))(lambda: attend(subtract_max=True))


def _diff_attention(bounded, qk_t, vg, B, S, lq1, lk1, lq2, lk2, sub_g, lambda_init):
    T = B * S
    hps = DIFF_HEADS_PER_STEP
    dv = 2 * HEAD_DIM
    width = hps * dv
    heads = (qk_t.shape[0] // 2) // width
    vec = lambda a: a.reshape(1, -1)
    small = lambda n: pl.BlockSpec((1, n), lambda b, h: (0, 0))
    return pl.pallas_call(
        functools.partial(_diff_attn_kernel, lambda_init=lambda_init),
        grid=(B, heads),
        in_specs=[pl.BlockSpec(memory_space=pltpu.SMEM),
                  pl.BlockSpec((width, S), lambda b, h: (h, b)),
                  pl.BlockSpec((width, S), lambda b, h: (heads + h, b)),
                  pl.BlockSpec((S, width), lambda b, h: (b, h)),
                  pl.BlockSpec((S, width), lambda b, h: (b, heads + h)),
                  small(HEAD_DIM), small(HEAD_DIM), small(HEAD_DIM), small(HEAD_DIM),
                  small(dv)],
        out_specs=pl.BlockSpec((S, width), lambda b, h: (b, h)),
        out_shape=jax.ShapeDtypeStruct((T, heads * width), BF16),
        scratch_shapes=[pltpu.VMEM((hps, dv + BF16_ROWS, S), BF16),
                        pltpu.VMEM((S, width), BF16)],
        compiler_params=_compiler_params(("parallel", "parallel")),
        name="diff_attention",
    )(bounded, qk_t, qk_t, vg, vg, vec(lq1), vec(lk1), vec(lq2), vec(lk2), vec(sub_g))


BIAS_PARTS = 3


QUERY_LANE0 = HEAD_DIM // 2


def _cumsum_kernel(logf_ref, kkey_ref, kqry_ref, *, n_heads):
    S = logf_ref.shape[0]
    row = lax.broadcasted_iota(jnp.int32, (CUM_BLK, CUM_BLK), 0)
    col = lax.broadcasted_iota(jnp.int32, (CUM_BLK, CUM_BLK), 1)
    tri = (col <= row).astype(BF16)
    src = lax.broadcasted_iota(jnp.int32, (HEAD_DIM, HEAD_DIM), 0)
    dst = lax.broadcasted_iota(jnp.int32, (HEAD_DIM, HEAD_DIM), 1)
    is_head = src < n_heads
    spread = [jnp.where(is_head & (dst == BIAS_PARTS * src + part), 1.0,
                        jnp.where(is_head & (dst == QUERY_LANE0 + BIAS_PARTS * src + part),
                                  -1.0, 0.0)).astype(BF16)
              for part in range(BIAS_PARTS)]
    lane = lax.broadcasted_iota(jnp.int32, (CUM_BLK, HEAD_DIM), 1)
    key_lanes = lane < QUERY_LANE0
    ones_lanes = (lane >= QUERY_LANE0) & (lane < QUERY_LANE0 + BIAS_PARTS * n_heads)
    def pieces(x):
        out = []
        for _ in range(BIAS_PARTS):
            out.append(x.astype(BF16))
            x = x - out[-1].astype(F32)
        return out

    local = [sum(_dot(tri, piece) for piece in pieces(logf_ref[kb * CUM_BLK:(kb + 1) * CUM_BLK, :]))
             for kb in range(S // CUM_BLK)]
    off = jnp.zeros((1, HEAD_DIM), F32)
    for kb in range(S // CUM_BLK):
        rows = slice(kb * CUM_BLK, (kb + 1) * CUM_BLK)
        c = local[kb] + off
        off = off + local[kb][CUM_BLK - 1:CUM_BLK, :]
        packed = sum(_dot(piece, spread[part])
                     for part, piece in enumerate(pieces(c * (-LOG2E))))
        kkey_ref[rows, :] = jnp.where(key_lanes, packed, jnp.where(ones_lanes, 1.0, 0.0)).astype(BF16)
        kqry_ref[rows, :] = jnp.where(key_lanes, 0.0, packed).astype(BF16)


def _forget_bias(logf, B, S, n_heads):
    T = B * S
    assert QUERY_LANE0 + BIAS_PARTS * n_heads <= HEAD_DIM and BIAS_PARTS * n_heads <= QUERY_LANE0
    table = jax.ShapeDtypeStruct((T, HEAD_DIM), BF16)
    return pl.pallas_call(
        functools.partial(_cumsum_kernel, n_heads=n_heads),
        grid=(B,),
        in_specs=[pl.BlockSpec((S, HEAD_DIM), lambda b: (b, 0))],
        out_specs=[pl.BlockSpec((S, HEAD_DIM), lambda b: (b, 0))] * 2,
        out_shape=[table, table],
        compiler_params=_compiler_params(("parallel",)),
        name="forget_prefix_sum",
    )(logf)


def _fox_attn_kernel(bounded_ref, q_ref, k_ref, v_ref, g_ref, kkey_ref, kqry_ref, out_ref, vt_scr):
    S = q_ref.shape[0]
    tq = ATTN_TQ
    heads_per_step = q_ref.shape[1] // HEAD_DIM
    key = lax.broadcasted_iota(jnp.int32, (tq, tq), 0)
    qry = lax.broadcasted_iota(jnp.int32, (tq, tq), 1)
    mask_t = key <= qry
    lane = lax.broadcasted_iota(jnp.int32, (tq, HEAD_DIM), 1)
    lanes = [slice(hh * HEAD_DIM, (hh + 1) * HEAD_DIM) for hh in range(heads_per_step)]
    heads = [pl.program_id(1) * heads_per_step + hh for hh in range(heads_per_step)]

    def attend(subtract_max):
        for hh, sl in enumerate(lanes):
            _fill_vt(vt_scr.at[hh], v_ref[:, sl])
        for qi in range(S // tq):
            r0 = qi * tq
            kv = r0 + tq
            s_ts = []
            for sl, head in zip(lanes, heads):
                pick = lane // BIAS_PARTS == head
                extra = jnp.where(pick, 1.0, 0.0).astype(BF16)
                if not subtract_max:
                    own = (lane >= QUERY_LANE0) & ((lane - QUERY_LANE0) // BIAS_PARTS == head)
                    extra = jnp.where(own, kqry_ref[r0:kv, :], extra)
                s_ts.append(_dot_nt(jnp.concatenate([k_ref[:kv, sl], kkey_ref[:kv, :]], axis=1),
                                    jnp.concatenate([q_ref[r0:kv, sl], extra], axis=1)))
            p_ts = [_softmax_t(s_t, r0, mask_t, subtract_max) for s_t in s_ts]
            os_ = [_dot(vt_scr[hh, :, :kv], p_t) for hh, p_t in enumerate(p_ts)]
            for sl, o in zip(lanes, os_):
                o = (o[:HEAD_DIM, :] * (1.0 / o[HEAD_DIM:HEAD_DIM + 1, :])).T
                o = o * _silu(g_ref[r0:kv, sl].astype(F32))
                out_ref[r0:kv, sl] = o.astype(BF16)

    bounded = bounded_ref[0] != 0
    pl.when(bounded)(lambda: attend(subtract_max=False))
    pl.when(jnp.logical_not(bounded))(lambda: attend(subtract_max=True))


def _fox_attention(bounded, qk, vg, kkey, kqry, B, S):
    T = B * S
    hps = FOX_HEADS_PER_STEP
    width = hps * HEAD_DIM
    groups = (qk.shape[1] // 2) // width
    table = pl.BlockSpec((S, HEAD_DIM), lambda b, h: (b, 0))
    return pl.pallas_call(
        _fox_attn_kernel,
        grid=(B, groups),
        in_specs=[pl.BlockSpec(memory_space=pltpu.SMEM),
                  pl.BlockSpec((S, width), lambda b, h: (b, h)),
                  pl.BlockSpec((S, width), lambda b, h: (b, groups + h)),
                  pl.BlockSpec((S, width), lambda b, h: (b, h)),
                  pl.BlockSpec((S, width), lambda b, h: (b, groups + h)),
                  table, table],
        out_specs=pl.BlockSpec((S, width), lambda b, h: (b, h)),
        out_shape=jax.ShapeDtypeStruct((T, groups * width), BF16),
        scratch_shapes=[pltpu.VMEM((hps, HEAD_DIM + BF16_ROWS, S), BF16)],
        compiler_params=_compiler_params(("parallel", "parallel")),
        name="fox_attention",
    )(bounded, qk, qk, vg, vg, kkey, kqry)


def kernel(x, positions, a_norm, a_w_in, a_q_norm, a_k_norm, a_lambda_q1, a_lambda_k1,
           a_lambda_q2, a_lambda_k2, a_sub_norm, a_w_out, b_norm, b_w_in, b_f_bias,
           b_q_norm, b_k_norm, b_w_out):
    B, S, D = x.shape
    T = B * S
    width = a_w_out.shape[1]
    heads_per_width = width // HEAD_DIM
    q_scale = HEAD_DIM ** -0.5 * LOG2E
    x2 = x.reshape(T, D)

    def col_gain(qg, kg):
        return jnp.concatenate([jnp.tile(qg * q_scale, heads_per_width),
                                jnp.tile(kg, heads_per_width)]).reshape(1, 2 * width)

    lambda_init0 = 0.8 - 0.6 * math.exp(-0.3 * 0)
    row_gain = jnp.broadcast_to(col_gain(a_q_norm[0], a_k_norm[0]).reshape(2 * width, 1),
                                (2 * width, HEAD_DIM))
    qk_t, h = _qk_proj_t(x2, a_norm[0], _cast_weight(a_w_in[0], 0, 2 * width, transpose_out=True),
                         row_gain, _rope_tables(positions))
    vg, w1t = _vg_proj(h, _cast_weight(a_w_in[0], 2 * width, 2 * width), 0, 2 * width,
                       side_w=b_w_in[0].T)
    o = _diff_attention(_logits_are_bounded(a_q_norm[0] * q_scale, a_k_norm[0]), qk_t, vg, B, S,
                        a_lambda_q1[0], a_lambda_k1[0], a_lambda_q2[0], a_lambda_k2[0],
                        a_sub_norm[0], lambda_init0)
    x2, h = _outproj(o, a_w_out[0], x2, next_norm_g=b_norm[0])

    n_f = b_w_in.shape[2] - 4 * width
    fb = jnp.pad(b_f_bias[0], (0, HEAD_DIM - n_f)).reshape(1, HEAD_DIM)
    qk = _qk_proj(h, w1t, col_gain(b_q_norm[0], b_k_norm[0]))
    vg, logf = _vg_proj(h, w1t, 2 * width, 2 * width, f_col0=4 * width, f_bias=fb,
                        w_is_transposed=True)
    kkey, kqry = _forget_bias(logf, B, S, n_f)
    o = _fox_attention(_logits_are_bounded(b_q_norm[0] * q_scale, b_k_norm[0]), qk, vg, kkey, kqry,
                       B, S)
    (x2,) = _outproj(o, b_w_out[0], x2)
    return x2.reshape(B, S, D)
```

```python
import functools
import math

import numpy as np
import jax
import jax.numpy as jnp
from jax import lax
from jax.experimental import pallas as pl
from jax.experimental.pallas import tpu as pltpu

F32 = jnp.float32
BF16 = jnp.bfloat16

EPS = 1e-6
CHUNK = 64
ROPE_THETA = 500000.0
HEAD_DIM = 128
BF16_ROWS = 16
ROT_DIM = HEAD_DIM // 4
ROT_HALF = ROT_DIM // 2
LOG2E = math.log2(math.e)
MASKED = -1e30

V7X_VMEM_LIMIT_BYTES = 56 * 1024 * 1024

PROJ_TM = 1024
PROJ_TN = 1024
PROJ_TN_FROM_H = 2048
OUT_TM = 512
ATTN_TQ = 256
CUM_BLK = 256
CAST_TN = 512
SIDE_ROW_ALIGN = BF16_ROWS
DIFF_HEADS_PER_STEP = 2
FOX_HEADS_PER_STEP = 4


def _compiler_params(semantics):
    return pltpu.CompilerParams(dimension_semantics=semantics,
                                vmem_limit_bytes=V7X_VMEM_LIMIT_BYTES)


def _dot(a, b):
    return jnp.dot(a, b, preferred_element_type=F32)


def _dot_nt(a, b):
    return lax.dot_general(a, b, (((1,), (1,)), ((), ())), preferred_element_type=F32)


def _cast_kernel(w_ref, out_ref, *, flip, col_axis, n_valid):
    w = w_ref[...]
    if n_valid is not None:
        col = (pl.program_id(0) * w.shape[col_axis]
               + lax.broadcasted_iota(jnp.int32, w.shape, col_axis))
        w = jnp.where(col < n_valid, w, 0.0)
    out_ref[...] = (w.T if flip else w).astype(BF16)


def _cast_weight(w, col0, n_cols, transpose_out=False, w_is_transposed=False):
    N, K = w.shape if w_is_transposed else w.shape[::-1]
    tc = CAST_TN
    j0 = col0 // tc
    n_cols = pl.cdiv(n_cols, tc) * tc
    n_valid = N - col0 if col0 + n_cols > N else None
    if w_is_transposed:
        in_spec = pl.BlockSpec((tc, K), lambda j: (j0 + j, 0))
    else:
        in_spec = pl.BlockSpec((K, tc), lambda j: (0, j0 + j))
    if transpose_out:
        out_spec = pl.BlockSpec((tc, K), lambda j: (j, 0))
        out_shape = jax.ShapeDtypeStruct((n_cols, K), BF16)
    else:
        out_spec = pl.BlockSpec((K, tc), lambda j: (0, j))
        out_shape = jax.ShapeDtypeStruct((K, n_cols), BF16)
    flip = transpose_out != w_is_transposed
    return pl.pallas_call(
        functools.partial(_cast_kernel, flip=flip, col_axis=0 if w_is_transposed else 1,
                          n_valid=n_valid),
        grid=(n_cols // tc,),
        in_specs=[in_spec],
        out_specs=out_spec,
        out_shape=out_shape,
        compiler_params=_compiler_params(("parallel",)),
        name="cast_weight_flip" if flip else "cast_weight",
    )(w)


def _rope_tables_kernel(pos_ref, invf_ref, cos_ref, sin_ref):
    ang = invf_ref[...] * pos_ref[...]
    cos_ref[...] = jnp.cos(ang)
    sin_ref[...] = jnp.sin(ang)


def _rope_tables(positions):
    T = positions.size
    tb = 2048
    pos = positions.reshape(1, T).astype(F32)
    invf = (ROPE_THETA ** (-np.arange(ROT_HALF, dtype=np.float32) / ROT_HALF)).astype(np.float32)
    invf = jnp.asarray(invf.reshape(ROT_HALF, 1))
    tab = jax.ShapeDtypeStruct((ROT_HALF, T), F32)
    return pl.pallas_call(
        _rope_tables_kernel,
        grid=(T // tb,),
        in_specs=[pl.BlockSpec((1, tb), lambda i: (0, i)),
                  pl.BlockSpec((ROT_HALF, 1), lambda i: (0, 0))],
        out_specs=[pl.BlockSpec((ROT_HALF, tb), lambda i: (0, i))] * 2,
        out_shape=[tab, tab],
        compiler_params=_compiler_params(("parallel",)),
        name="rope_tables",
    )(pos, invf)


def _rmsnorm_to(h_ref, x_ref, ng_ref):
    xf = x_ref[...]
    ms = jnp.mean(xf * xf, axis=-1, keepdims=True)
    h_ref[...] = (xf * lax.rsqrt(ms + EPS) * ng_ref[...]).astype(BF16)


def _qk_proj_kernel(h_ref, wt_ref, cg_ref, out_ref):
    acc = _dot_nt(h_ref[...], wt_ref[...])
    for hh in range(acc.shape[1] // HEAD_DIM):
        sl = slice(hh * HEAD_DIM, (hh + 1) * HEAD_DIM)
        a = acc[:, sl]
        r = lax.rsqrt(jnp.mean(a * a, axis=-1, keepdims=True) + EPS)
        out_ref[:, sl] = (a * r * cg_ref[:, sl]).astype(BF16)


def _qk_proj(h, wt_bf, col_gain):
    T, D = h.shape
    N = col_gain.shape[1]
    tm, tn = PROJ_TM, PROJ_TN_FROM_H
    return pl.pallas_call(
        _qk_proj_kernel,
        grid=(T // tm, N // tn),
        in_specs=[pl.BlockSpec((tm, D), lambda i, j: (i, 0)),
                  pl.BlockSpec((tn, D), lambda i, j: (j, 0)),
                  pl.BlockSpec((1, tn), lambda i, j: (0, j))],
        out_specs=pl.BlockSpec((tm, tn), lambda i, j: (i, j)),
        out_shape=jax.ShapeDtypeStruct((T, N), BF16),
        compiler_params=_compiler_params(("parallel", "arbitrary")),
        name="qk_proj",
    )(h, wt_bf, col_gain)


def _qk_proj_t_kernel(x_ref, ng_ref, wt_ref, g_ref, cos_ref, sin_ref, side_ref, out_ref, h_ref,
                      side_out_ref):
    @pl.when(pl.program_id(1) == 0)
    def _():
        _rmsnorm_to(h_ref, x_ref, ng_ref)

    side_out_ref[...] = side_ref[...].astype(BF16)

    acc = _dot_nt(wt_ref[...], h_ref[...])
    tm = acc.shape[1]
    cos = cos_ref[...]
    sin = sin_ref[...]
    for hh in range(acc.shape[0] // HEAD_DIM):
        rows = slice(hh * HEAD_DIM, (hh + 1) * HEAD_DIM)
        a = acc[rows, :]
        r = lax.rsqrt(jnp.mean(a * a, axis=0, keepdims=True) + EPS)
        gain = jnp.concatenate([g_ref[rows, :]] * (tm // HEAD_DIM), axis=1)
        y = a * r * gain
        y1 = y[:ROT_HALF, :]
        y2 = y[ROT_HALF:ROT_DIM, :]
        y = jnp.concatenate([y1 * cos - y2 * sin, y2 * cos + y1 * sin, y[ROT_DIM:, :]], axis=0)
        out_ref[rows, :] = y.astype(BF16)


def _qk_proj_t(x2, norm_g, wt_bf, row_gain, rope_tabs, side_w, side_col0):
    T, D = x2.shape
    N = row_gain.shape[0]
    tm, tn = PROJ_TM, PROJ_TN
    n_row_steps, n_col_steps = T // tm, N // tn
    assert n_col_steps >= 2
    side_rows, side_cols = side_w.shape[0], side_w.shape[1] - side_col0
    side_blk = side_rows // (n_row_steps * n_col_steps)
    assert side_blk * n_row_steps * n_col_steps == side_rows and side_blk % SIDE_ROW_ALIGN == 0
    assert side_col0 % side_cols == 0
    side_in = pl.BlockSpec((side_blk, side_cols),
                           lambda i, j: (i * n_col_steps + j, side_col0 // side_cols))
    side_out = pl.BlockSpec((side_blk, side_cols), lambda i, j: (i * n_col_steps + j, 0))

    def x_block(i, j):
        return (jnp.minimum(i + jnp.minimum(j, 1), n_row_steps - 1), 0)

    return pl.pallas_call(
        _qk_proj_t_kernel,
        grid=(n_row_steps, n_col_steps),
        in_specs=[pl.BlockSpec((tm, D), x_block),
                  pl.BlockSpec((1, D), lambda i, j: (0, 0)),
                  pl.BlockSpec((tn, D), lambda i, j: (j, 0)),
                  pl.BlockSpec((tn, HEAD_DIM), lambda i, j: (j, 0)),
                  pl.BlockSpec((ROT_HALF, tm), lambda i, j: (0, i)),
                  pl.BlockSpec((ROT_HALF, tm), lambda i, j: (0, i)),
                  side_in],
        out_specs=[pl.BlockSpec((tn, tm), lambda i, j: (j, i)),
                   pl.BlockSpec((tm, D), lambda i, j: (i, 0)),
                   side_out],
        out_shape=[jax.ShapeDtypeStruct((N, T), BF16), jax.ShapeDtypeStruct((T, D), BF16),
                   jax.ShapeDtypeStruct((side_rows, side_cols), BF16)],
        compiler_params=_compiler_params(("arbitrary", "arbitrary")),
        name="qk_proj_rope",
    )(x2, norm_g.reshape(1, D), wt_bf, row_gain, *rope_tabs, side_w)


def _vg_proj_kernel(*refs, with_f, n_col_steps, w_is_transposed, side_valid_rows):
    refs = list(refs)
    h_ref, w_ref = refs[:2]
    mm = _dot_nt if w_is_transposed else _dot
    pos = 2
    if with_f:
        wf_ref, fb_ref = refs[pos:pos + 2]
        pos += 2
    if side_valid_rows is not None:
        side_ref = refs[pos]
        pos += 1
    out_ref = refs[pos]
    pos += 1
    if with_f:
        logf_ref = refs[pos]
        pos += 1
        share = h_ref.shape[0] // n_col_steps
        rows = pl.ds(pl.multiple_of(pl.program_id(1) * share, share), share)
        f = mm(h_ref[rows, :], wf_ref[...]) + fb_ref[...]
        logf_ref[rows, :] = jnp.minimum(f, 0.0) - jnp.log(1.0 + jnp.exp(-jnp.abs(f)))
    if side_valid_rows is not None:
        side_out_ref = refs[pos]
        blk = side_ref.shape[0]
        step = pl.program_id(0) * n_col_steps + pl.program_id(1)
        row = step * blk + lax.broadcasted_iota(jnp.int32, side_ref.shape, 0)
        side_out_ref[...] = jnp.where(row < side_valid_rows, side_ref[...], 0.0).astype(BF16)
    out_ref[...] = mm(h_ref[...], w_ref[...]).astype(BF16)


def _vg_proj(h, w_bf, col0, n_cols, f_col0=None, f_bias=None, w_is_transposed=False, side_w=None):
    T, D = h.shape
    tm, tn = PROJ_TM, PROJ_TN_FROM_H
    j0 = col0 // tn
    n_row_steps, n_col_steps = T // tm, n_cols // tn
    with_f = f_col0 is not None
    if w_is_transposed:
        w_spec = pl.BlockSpec((tn, D), lambda i, j: (j0 + j, 0))
    else:
        w_spec = pl.BlockSpec((D, tn), lambda i, j: (0, j0 + j))
    in_specs = [pl.BlockSpec((tm, D), lambda i, j: (i, 0)), w_spec]
    args = [h, w_bf]
    out_specs = [pl.BlockSpec((tm, tn), lambda i, j: (i, j))]
    out_shape = [jax.ShapeDtypeStruct((T, n_cols), BF16)]
    if with_f:
        jf = f_col0 // HEAD_DIM
        wf_spec = (pl.BlockSpec((HEAD_DIM, D), lambda i, j: (jf, 0)) if w_is_transposed
                   else pl.BlockSpec((D, HEAD_DIM), lambda i, j: (0, jf)))
        in_specs += [wf_spec, pl.BlockSpec((1, HEAD_DIM), lambda i, j: (0, 0))]
        args += [w_bf, f_bias]
    side_valid_rows = None
    if side_w is not None:
        side_valid_rows = side_w.shape[0]
        n_steps = n_row_steps * n_col_steps
        blk = pl.cdiv(pl.cdiv(side_valid_rows, n_steps), SIDE_ROW_ALIGN) * SIDE_ROW_ALIGN
        last_blk = (side_valid_rows - 1) // blk
        in_specs.append(pl.BlockSpec(
            (blk, D), lambda i, j: (jnp.minimum(i * n_col_steps + j, last_blk), 0)))
        args.append(side_w)
    if with_f:
        out_specs.append(pl.BlockSpec((tm, HEAD_DIM), lambda i, j: (i, 0)))
        out_shape.append(jax.ShapeDtypeStruct((T, HEAD_DIM), F32))
    if side_w is not None:
        out_specs.append(pl.BlockSpec((blk, D), lambda i, j: (i * n_col_steps + j, 0)))
        out_shape.append(jax.ShapeDtypeStruct((blk * n_steps, D), BF16))
    return pl.pallas_call(
        functools.partial(_vg_proj_kernel, with_f=with_f, n_col_steps=n_col_steps,
                          w_is_transposed=w_is_transposed, side_valid_rows=side_valid_rows),
        grid=(n_row_steps, n_col_steps),
        in_specs=in_specs,
        out_specs=out_specs,
        out_shape=out_shape,
        compiler_params=_compiler_params(("parallel", "arbitrary")),
        name="vg_proj_fgate" if with_f else "vg_proj",
    )(*args)


def _outproj_kernel(*refs, with_next_norm):
    if with_next_norm:
        o_ref, w_ref, x_ref, ng_ref, out_ref, h_ref, w_bf = refs
    else:
        o_ref, w_ref, x_ref, out_ref, w_bf = refs

    @pl.when(pl.program_id(0) == 0)
    def _():
        w_bf[...] = w_ref[...].astype(BF16)

    y = x_ref[...] + _dot(o_ref[...], w_bf[...])
    out_ref[...] = y
    if with_next_norm:
        ms = jnp.mean(y * y, axis=-1, keepdims=True)
        h_ref[...] = (y * lax.rsqrt(ms + EPS) * ng_ref[...]).astype(BF16)


def _outproj(o_bf, w, x2, next_norm_g=None):
    T, D = x2.shape
    K = o_bf.shape[1]
    tm = OUT_TM
    with_next_norm = next_norm_g is not None
    row_block = pl.BlockSpec((tm, D), lambda i: (i, 0))
    in_specs = [pl.BlockSpec((tm, K), lambda i: (i, 0)),
                pl.BlockSpec((K, D), lambda i: (0, 0), pipeline_mode=pl.Buffered(1)),
                row_block]
    args = [o_bf, w, x2]
    out_specs = [row_block]
    out_shape = [jax.ShapeDtypeStruct((T, D), F32)]
    if with_next_norm:
        in_specs.append(pl.BlockSpec((1, D), lambda i: (0, 0)))
        args.append(next_norm_g.reshape(1, D))
        out_specs.append(row_block)
        out_shape.append(jax.ShapeDtypeStruct((T, D), BF16))
    return pl.pallas_call(
        functools.partial(_outproj_kernel, with_next_norm=with_next_norm),
        grid=(T // tm,),
        in_specs=in_specs,
        out_specs=out_specs,
        out_shape=out_shape,
        scratch_shapes=[pltpu.VMEM((K, D), BF16)],
        compiler_params=_compiler_params(("arbitrary",)),
        name="outproj_residual_norm" if with_next_norm else "outproj_residual",
    )(*args)


LOGIT_BOUND = 46.0


def _logits_are_bounded(q_gain, k_gain):
    bound = 1.02 * HEAD_DIM * jnp.max(jnp.abs(q_gain)) * jnp.max(jnp.abs(k_gain))
    return (bound < LOGIT_BOUND).astype(jnp.int32).reshape(1)


def _softmax_t(s_t, r0, mask_t, subtract_max, keep_f32=False):
    last = jnp.where(mask_t, s_t[r0:, :], MASKED)
    first = s_t[:r0, :] if r0 > 0 else None
    if subtract_max:
        m = jnp.max(last, axis=0, keepdims=True)
        if r0 > 0:
            m = jnp.maximum(m, jnp.max(first, axis=0, keepdims=True))
            first = first - m
        last = last - m
    e = jnp.exp2(last) if r0 == 0 else jnp.concatenate([jnp.exp2(first), jnp.exp2(last)], axis=0)
    return e if keep_f32 else e.astype(BF16)


def _fill_vt(vt_scr, v):
    d = v.shape[1]
    vt_scr[:d, :] = v.T
    vt_scr[d:, :] = jnp.ones((vt_scr.shape[0] - d, vt_scr.shape[1]), BF16)


def _silu(g):
    return g * (1.0 / (1.0 + jnp.exp(-g)))


def _diff_attn_kernel(bounded_ref, qt_ref, kt_ref, v_ref, g_ref, lq1_ref, lk1_ref, lq2_ref, lk2_ref,
                      sg_ref, out_ref, vt_scr, k_scr, *, lambda_init):
    S = v_ref.shape[0]
    tq = ATTN_TQ
    dv = 2 * HEAD_DIM
    n_heads = v_ref.shape[1] // dv
    lam = (jnp.exp(jnp.sum(lq1_ref[...] * lk1_ref[...], axis=-1, keepdims=True))
           - jnp.exp(jnp.sum(lq2_ref[...] * lk2_ref[...], axis=-1, keepdims=True))
           + lambda_init)
    key_chunk = lax.broadcasted_iota(jnp.int32, (tq, tq), 0) // CHUNK
    qry_chunk = lax.broadcasted_iota(jnp.int32, (tq, tq), 1) // CHUNK
    mask_t = key_chunk <= qry_chunk
    sub_gain = sg_ref[...] * (1.0 - lambda_init)

    maps = [slice(m * HEAD_DIM, (m + 1) * HEAD_DIM) for m in range(2 * n_heads)]

    def attend(subtract_max):
        for hh in range(n_heads):
            _fill_vt(vt_scr.at[hh], v_ref[:, hh * dv:(hh + 1) * dv])
        k_scr[...] = kt_ref[...].T
        for qi in range(S // tq):
            r0 = qi * tq
            kv = r0 + tq
            s_ts = [_dot(k_scr[:kv, sl], qt_ref[sl, r0:kv]) for sl in maps]
            if not subtract_max:
                e_ts = [_softmax_t(s_t, r0, mask_t, False, keep_f32=True) for s_t in s_ts]
                ws = [1.0 / jnp.sum(e_t, axis=0, keepdims=True) for e_t in e_ts]
                p_ts = [(e_ts[2 * hh] * ws[2 * hh]
                         - e_ts[2 * hh + 1] * (lam * ws[2 * hh + 1])).astype(BF16)
                        for hh in range(n_heads)]
                os_t = [_dot(vt_scr[hh, :dv, :kv], p_t) for hh, p_t in enumerate(p_ts)]
            else:
                p_ts = [_softmax_t(s_t, r0, mask_t, True) for s_t in s_ts]
                os_ = [_dot(vt_scr[m // 2, :, :kv], p_t) for m, p_t in enumerate(p_ts)]
                os_t = []
                for hh in range(n_heads):
                    o1, o2 = os_[2 * hh], os_[2 * hh + 1]
                    w1 = 1.0 / o1[dv:dv + 1, :]
                    w2 = lam * (1.0 / o2[dv:dv + 1, :])
                    os_t.append(o1[:dv, :] * w1 - o2[:dv, :] * w2)
            for hh in range(n_heads):
                o = os_t[hh].T
                o = o * lax.rsqrt(jnp.mean(o * o, axis=-1, keepdims=True) + EPS) * sub_gain
                o = o * _silu(g_ref[r0:kv, hh * dv:(hh + 1) * dv].astype(F32))
                out_ref[r0:kv, hh * dv:(hh + 1) * dv] = o.astype(BF16)

    bounded = bounded_ref[0] != 0
    pl.when(bounded)(lambda: attend(subtract_max=False))
    pl.when(jnp.logical_not(bounded))(lambda: attend(subtract_max=True))


def _diff_attention(bounded, qk_t, vg, B, S, lq1, lk1, lq2, lk2, sub_g, lambda_init):
    T = B * S
    hps = DIFF_HEADS_PER_STEP
    dv = 2 * HEAD_DIM
    width = hps * dv
    heads = (qk_t.shape[0] // 2) // width
    vec = lambda a: a.reshape(1, -1)
    small = lambda n: pl.BlockSpec((1, n), lambda b, h: (0, 0))
    return pl.pallas_call(
        functools.partial(_diff_attn_kernel, lambda_init=lambda_init),
        grid=(B, heads),
        in_specs=[pl.BlockSpec(memory_space=pltpu.SMEM),
                  pl.BlockSpec((width, S), lambda b, h: (h, b)),
                  pl.BlockSpec((width, S), lambda b, h: (heads + h, b)),
                  pl.BlockSpec((S, width), lambda b, h: (b, h)),
                  pl.BlockSpec((S, width), lambda b, h: (b, heads + h)),
                  small(HEAD_DIM), small(HEAD_DIM), small(HEAD_DIM), small(HEAD_DIM),
                  small(dv)],
        out_specs=pl.BlockSpec((S, width), lambda b, h: (b, h)),
        out_shape=jax.ShapeDtypeStruct((T, heads * width), BF16),
        scratch_shapes=[pltpu.VMEM((hps, dv + BF16_ROWS, S), BF16),
                        pltpu.VMEM((S, width), BF16)],
        compiler_params=_compiler_params(("parallel", "parallel")),
        name="diff_attention",
    )(bounded, qk_t, qk_t, vg, vg, vec(lq1), vec(lk1), vec(lq2), vec(lk2), vec(sub_g))


BIAS_PARTS = 3


QUERY_LANE0 = HEAD_DIM // 2


def _cumsum_kernel(logf_ref, kkey_ref, kqry_ref, *, n_heads):
    S = logf_ref.shape[0]
    row = lax.broadcasted_iota(jnp.int32, (CUM_BLK, CUM_BLK), 0)
    col = lax.broadcasted_iota(jnp.int32, (CUM_BLK, CUM_BLK), 1)
    tri = (col <= row).astype(BF16)
    src = lax.broadcasted_iota(jnp.int32, (HEAD_DIM, HEAD_DIM), 0)
    dst = lax.broadcasted_iota(jnp.int32, (HEAD_DIM, HEAD_DIM), 1)
    is_head = src < n_heads
    spread = [jnp.where(is_head & (dst == BIAS_PARTS * src + part), 1.0,
                        jnp.where(is_head & (dst == QUERY_LANE0 + BIAS_PARTS * src + part),
                                  -1.0, 0.0)).astype(BF16)
              for part in range(BIAS_PARTS)]
    lane = lax.broadcasted_iota(jnp.int32, (CUM_BLK, HEAD_DIM), 1)
    key_lanes = lane < QUERY_LANE0
    ones_lanes = (lane >= QUERY_LANE0) & (lane < QUERY_LANE0 + BIAS_PARTS * n_heads)
    def pieces(x):
        out = []
        for _ in range(BIAS_PARTS):
            out.append(x.astype(BF16))
            x = x - out[-1].astype(F32)
        return out

    local = [sum(_dot(tri, piece) for piece in pieces(logf_ref[kb * CUM_BLK:(kb + 1) * CUM_BLK, :]))
             for kb in range(S // CUM_BLK)]
    off = jnp.zeros((1, HEAD_DIM), F32)
    for kb in range(S // CUM_BLK):
        rows = slice(kb * CUM_BLK, (kb + 1) * CUM_BLK)
        c = local[kb] + off
        off = off + local[kb][CUM_BLK - 1:CUM_BLK, :]
        packed = sum(_dot(piece, spread[part])
                     for part, piece in enumerate(pieces(c * (-LOG2E))))
        kkey_ref[rows, :] = jnp.where(key_lanes, packed, jnp.where(ones_lanes, 1.0, 0.0)).astype(BF16)
        kqry_ref[rows, :] = jnp.where(key_lanes, 0.0, packed).astype(BF16)


def _forget_bias(logf, B, S, n_heads):
    T = B * S
    assert QUERY_LANE0 + BIAS_PARTS * n_heads <= HEAD_DIM and BIAS_PARTS * n_heads <= QUERY_LANE0
    table = jax.ShapeDtypeStruct((T, HEAD_DIM), BF16)
    return pl.pallas_call(
        functools.partial(_cumsum_kernel, n_heads=n_heads),
        grid=(B,),
        in_specs=[pl.BlockSpec((S, HEAD_DIM), lambda b: (b, 0))],
        out_specs=[pl.BlockSpec((S, HEAD_DIM), lambda b: (b, 0))] * 2,
        out_shape=[table, table],
        compiler_params=_compiler_params(("parallel",)),
        name="forget_prefix_sum",
    )(logf)


def _fox_attn_kernel(bounded_ref, q_ref, k_ref, v_ref, g_ref, kkey_ref, kqry_ref, out_ref, vt_scr):
    S = q_ref.shape[0]
    tq = ATTN_TQ
    heads_per_step = q_ref.shape[1] // HEAD_DIM
    key = lax.broadcasted_iota(jnp.int32, (tq, tq), 0)
    qry = lax.broadcasted_iota(jnp.int32, (tq, tq), 1)
    mask_t = key <= qry
    lane = lax.broadcasted_iota(jnp.int32, (tq, HEAD_DIM), 1)
    lanes = [slice(hh * HEAD_DIM, (hh + 1) * HEAD_DIM) for hh in range(heads_per_step)]
    heads = [pl.program_id(1) * heads_per_step + hh for hh in range(heads_per_step)]

    def attend(subtract_max):
        for hh, sl in enumerate(lanes):
            _fill_vt(vt_scr.at[hh], v_ref[:, sl])
        for qi in range(S // tq):
            r0 = qi * tq
            kv = r0 + tq
            s_ts = []
            for sl, head in zip(lanes, heads):
                pick = lane // BIAS_PARTS == head
                extra = jnp.where(pick, 1.0, 0.0).astype(BF16)
                if not subtract_max:
                    own = (lane >= QUERY_LANE0) & ((lane - QUERY_LANE0) // BIAS_PARTS == head)
                    extra = jnp.where(own, kqry_ref[r0:kv, :], extra)
                s_ts.append(_dot_nt(jnp.concatenate([k_ref[:kv, sl], kkey_ref[:kv, :]], axis=1),
                                    jnp.concatenate([q_ref[r0:kv, sl], extra], axis=1)))
            p_ts = [_softmax_t(s_t, r0, mask_t, subtract_max) for s_t in s_ts]
            os_ = [_dot(vt_scr[hh, :, :kv], p_t) for hh, p_t in enumerate(p_ts)]
            for sl, o in zip(lanes, os_):
                o = (o[:HEAD_DIM, :] * (1.0 / o[HEAD_DIM:HEAD_DIM + 1, :])).T
                o = o * _silu(g_ref[r0:kv, sl].astype(F32))
                out_ref[r0:kv, sl] = o.astype(BF16)

    bounded = bounded_ref[0] != 0
    pl.when(bounded)(lambda: attend(subtract_max=False))
    pl.when(jnp.logical_not(bounded))(lambda: attend(subtract_max=True))


def _fox_attention(bounded, qk, vg, kkey, kqry, B, S):
    T = B * S
    hps = FOX_HEADS_PER_STEP
    width = hps * HEAD_DIM
    groups = (qk.shape[1] // 2) // width
    table = pl.BlockSpec((S, HEAD_DIM), lambda b, h: (b, 0))
    return pl.pallas_call(
        _fox_attn_kernel,
        grid=(B, groups),
        in_specs=[pl.BlockSpec(memory_space=pltpu.SMEM),
                  pl.BlockSpec((S, width), lambda b, h: (b, h)),
                  pl.BlockSpec((S, width), lambda b, h: (b, groups + h)),
                  pl.BlockSpec((S, width), lambda b, h: (b, h)),
                  pl.BlockSpec((S, width), lambda b, h: (b, groups + h)),
                  table, table],
        out_specs=pl.BlockSpec((S, width), lambda b, h: (b, h)),
        out_shape=jax.ShapeDtypeStruct((T, groups * width), BF16),
        scratch_shapes=[pltpu.VMEM((hps, HEAD_DIM + BF16_ROWS, S), BF16)],
        compiler_params=_compiler_params(("parallel", "parallel")),
        name="fox_attention",
    )(bounded, qk, qk, vg, vg, kkey, kqry)


def kernel(x, positions, a_norm, a_w_in, a_q_norm, a_k_norm, a_lambda_q1, a_lambda_k1,
           a_lambda_q2, a_lambda_k2, a_sub_norm, a_w_out, b_norm, b_w_in, b_f_bias,
           b_q_norm, b_k_norm, b_w_out):
    B, S, D = x.shape
    T = B * S
    width = a_w_out.shape[1]
    heads_per_width = width // HEAD_DIM
    q_scale = HEAD_DIM ** -0.5 * LOG2E
    x2 = x.reshape(T, D)

    def col_gain(qg, kg):
        return jnp.concatenate([jnp.tile(qg * q_scale, heads_per_width),
                                jnp.tile(kg, heads_per_width)]).reshape(1, 2 * width)

    lambda_init0 = 0.8 - 0.6 * math.exp(-0.3 * 0)
    row_gain = jnp.broadcast_to(col_gain(a_q_norm[0], a_k_norm[0]).reshape(2 * width, 1),
                                (2 * width, HEAD_DIM))
    qk_t, h, w0_vg = _qk_proj_t(x2, a_norm[0],
                                _cast_weight(a_w_in[0], 0, 2 * width, transpose_out=True),
                                row_gain, _rope_tables(positions),
                                side_w=a_w_in[0], side_col0=2 * width)
    vg, w1t = _vg_proj(h, w0_vg, 0, 2 * width, side_w=b_w_in[0].T)
    o = _diff_attention(_logits_are_bounded(a_q_norm[0] * q_scale, a_k_norm[0]), qk_t, vg, B, S,
                        a_lambda_q1[0], a_lambda_k1[0], a_lambda_q2[0], a_lambda_k2[0],
                        a_sub_norm[0], lambda_init0)
    x2, h = _outproj(o, a_w_out[0], x2, next_norm_g=b_norm[0])

    n_f = b_w_in.shape[2] - 4 * width
    fb = jnp.pad(b_f_bias[0], (0, HEAD_DIM - n_f)).reshape(1, HEAD_DIM)
    qk = _qk_proj(h, w1t, col_gain(b_q_norm[0], b_k_norm[0]))
    vg, logf = _vg_proj(h, w1t, 2 * width, 2 * width, f_col0=4 * width, f_bias=fb,
                        w_is_transposed=True)
    kkey, kqry = _forget_bias(logf, B, S, n_f)
    o = _fox_attention(_logits_are_bounded(b_q_norm[0] * q_scale, b_k_norm[0]), qk, vg, kkey, kqry,
                       B, S)
    (x2,) = _outproj(o, b_w_out[0], x2)
    return x2.reshape(B, S, D)
```

```python
import functools
import math

import numpy as np
import jax
import jax.numpy as jnp
from jax import lax
from jax.experimental import pallas as pl
from jax.experimental.pallas import tpu as pltpu

F32 = jnp.float32
BF16 = jnp.bfloat16

EPS = 1e-6
CHUNK = 64
ROPE_THETA = 500000.0
HEAD_DIM = 128
BF16_ROWS = 16
ROT_DIM = HEAD_DIM // 4
ROT_HALF = ROT_DIM // 2
LOG2E = math.log2(math.e)
MASKED = -1e30

V7X_VMEM_LIMIT_BYTES = 56 * 1024 * 1024

PROJ_TM = 1024
PROJ_TN = 1024
PROJ_TN_FROM_H = 2048
OUT_TM = 512
ATTN_TQ = 256
CUM_BLK = 256
CAST_TN = 512
SIDE_ROW_ALIGN = BF16_ROWS
DIFF_HEADS_PER_STEP = 2
FOX_HEADS_PER_STEP = 4


def _compiler_params(semantics):
    return pltpu.CompilerParams(dimension_semantics=semantics,
                                vmem_limit_bytes=V7X_VMEM_LIMIT_BYTES)


def _dot(a, b):
    return jnp.dot(a, b, preferred_element_type=F32)


def _dot_nt(a, b):
    return lax.dot_general(a, b, (((1,), (1,)), ((), ())), preferred_element_type=F32)


def _cast_kernel(w_ref, out_ref, *, flip, col_axis, n_valid):
    w = w_ref[...]
    if n_valid is not None:
        col = (pl.program_id(0) * w.shape[col_axis]
               + lax.broadcasted_iota(jnp.int32, w.shape, col_axis))
        w = jnp.where(col < n_valid, w, 0.0)
    out_ref[...] = (w.T if flip else w).astype(BF16)


def _cast_weight(w, col0, n_cols, transpose_out=False, w_is_transposed=False):
    N, K = w.shape if w_is_transposed else w.shape[::-1]
    tc = CAST_TN
    j0 = col0 // tc
    n_cols = pl.cdiv(n_cols, tc) * tc
    n_valid = N - col0 if col0 + n_cols > N else None
    if w_is_transposed:
        in_spec = pl.BlockSpec((tc, K), lambda j: (j0 + j, 0))
    else:
        in_spec = pl.BlockSpec((K, tc), lambda j: (0, j0 + j))
    if transpose_out:
        out_spec = pl.BlockSpec((tc, K), lambda j: (j, 0))
        out_shape = jax.ShapeDtypeStruct((n_cols, K), BF16)
    else:
        out_spec = pl.BlockSpec((K, tc), lambda j: (0, j))
        out_shape = jax.ShapeDtypeStruct((K, n_cols), BF16)
    flip = transpose_out != w_is_transposed
    return pl.pallas_call(
        functools.partial(_cast_kernel, flip=flip, col_axis=0 if w_is_transposed else 1,
                          n_valid=n_valid),
        grid=(n_cols // tc,),
        in_specs=[in_spec],
        out_specs=out_spec,
        out_shape=out_shape,
        compiler_params=_compiler_params(("parallel",)),
        name="cast_weight_flip" if flip else "cast_weight",
    )(w)


def _rope_tables_kernel(pos_ref, invf_ref, cos_ref, sin_ref):
    ang = invf_ref[...] * pos_ref[...]
    cos_ref[...] = jnp.cos(ang)
    sin_ref[...] = jnp.sin(ang)


def _rope_tables(positions):
    T = positions.size
    tb = 2048
    pos = positions.reshape(1, T).astype(F32)
    invf = (ROPE_THETA ** (-np.arange(ROT_HALF, dtype=np.float32) / ROT_HALF)).astype(np.float32)
    invf = jnp.asarray(invf.reshape(ROT_HALF, 1))
    tab = jax.ShapeDtypeStruct((ROT_HALF, T), F32)
    return pl.pallas_call(
        _rope_tables_kernel,
        grid=(T // tb,),
        in_specs=[pl.BlockSpec((1, tb), lambda i: (0, i)),
                  pl.BlockSpec((ROT_HALF, 1), lambda i: (0, 0))],
        out_specs=[pl.BlockSpec((ROT_HALF, tb), lambda i: (0, i))] * 2,
        out_shape=[tab, tab],
        compiler_params=_compiler_params(("parallel",)),
        name="rope_tables",
    )(pos, invf)


def _rmsnorm_to(h_ref, x_ref, ng_ref):
    xf = x_ref[...]
    ms = jnp.mean(xf * xf, axis=-1, keepdims=True)
    h_ref[...] = (xf * lax.rsqrt(ms + EPS) * ng_ref[...]).astype(BF16)


def _snake(i, j, n):
    return jnp.where(i % 2 == 0, j, n - 1 - j)


def _qk_proj_kernel(h_ref, wt_ref, cg_ref, out_ref):
    acc = _dot_nt(h_ref[...], wt_ref[...])
    for hh in range(acc.shape[1] // HEAD_DIM):
        sl = slice(hh * HEAD_DIM, (hh + 1) * HEAD_DIM)
        a = acc[:, sl]
        r = lax.rsqrt(jnp.mean(a * a, axis=-1, keepdims=True) + EPS)
        out_ref[:, sl] = (a * r * cg_ref[:, sl]).astype(BF16)


def _qk_proj(h, wt_bf, col_gain):
    T, D = h.shape
    N = col_gain.shape[1]
    tm, tn = PROJ_TM, PROJ_TN_FROM_H
    col = functools.partial(_snake, n=N // tn)
    return pl.pallas_call(
        _qk_proj_kernel,
        grid=(T // tm, N // tn),
        in_specs=[pl.BlockSpec((tm, D), lambda i, j: (i, 0)),
                  pl.BlockSpec((tn, D), lambda i, j: (col(i, j), 0)),
                  pl.BlockSpec((1, tn), lambda i, j: (0, col(i, j)))],
        out_specs=pl.BlockSpec((tm, tn), lambda i, j: (i, col(i, j))),
        out_shape=jax.ShapeDtypeStruct((T, N), BF16),
        compiler_params=_compiler_params(("parallel", "arbitrary")),
        name="qk_proj",
    )(h, wt_bf, col_gain)


def _qk_proj_t_kernel(x_ref, ng_ref, wt_ref, g_ref, cos_ref, sin_ref, side_ref, out_ref, h_ref,
                      side_out_ref):
    @pl.when(pl.program_id(1) == 0)
    def _():
        _rmsnorm_to(h_ref, x_ref, ng_ref)

    side_out_ref[...] = side_ref[...].astype(BF16)

    acc = _dot_nt(wt_ref[...], h_ref[...])
    tm = acc.shape[1]
    cos = cos_ref[...]
    sin = sin_ref[...]
    for hh in range(acc.shape[0] // HEAD_DIM):
        rows = slice(hh * HEAD_DIM, (hh + 1) * HEAD_DIM)
        a = acc[rows, :]
        r = lax.rsqrt(jnp.mean(a * a, axis=0, keepdims=True) + EPS)
        gain = jnp.concatenate([g_ref[rows, :]] * (tm // HEAD_DIM), axis=1)
        y = a * r * gain
        y1 = y[:ROT_HALF, :]
        y2 = y[ROT_HALF:ROT_DIM, :]
        y = jnp.concatenate([y1 * cos - y2 * sin, y2 * cos + y1 * sin, y[ROT_DIM:, :]], axis=0)
        out_ref[rows, :] = y.astype(BF16)


def _qk_proj_t(x2, norm_g, wt_bf, row_gain, rope_tabs, side_w, side_col0):
    T, D = x2.shape
    N = row_gain.shape[0]
    tm, tn = PROJ_TM, PROJ_TN
    n_row_steps, n_col_steps = T // tm, N // tn
    assert n_col_steps >= 2
    side_rows, side_cols = side_w.shape[0], side_w.shape[1] - side_col0
    side_blk = side_rows // (n_row_steps * n_col_steps)
    assert side_blk * n_row_steps * n_col_steps == side_rows and side_blk % SIDE_ROW_ALIGN == 0
    assert side_col0 % side_cols == 0
    side_in = pl.BlockSpec((side_blk, side_cols),
                           lambda i, j: (i * n_col_steps + j, side_col0 // side_cols))
    side_out = pl.BlockSpec((side_blk, side_cols), lambda i, j: (i * n_col_steps + j, 0))
    col = functools.partial(_snake, n=n_col_steps)

    def x_block(i, j):
        return (jnp.minimum(i + jnp.minimum(j, 1), n_row_steps - 1), 0)

    return pl.pallas_call(
        _qk_proj_t_kernel,
        grid=(n_row_steps, n_col_steps),
        in_specs=[pl.BlockSpec((tm, D), x_block),
                  pl.BlockSpec((1, D), lambda i, j: (0, 0)),
                  pl.BlockSpec((tn, D), lambda i, j: (col(i, j), 0)),
                  pl.BlockSpec((tn, HEAD_DIM), lambda i, j: (col(i, j), 0)),
                  pl.BlockSpec((ROT_HALF, tm), lambda i, j: (0, i)),
                  pl.BlockSpec((ROT_HALF, tm), lambda i, j: (0, i)),
                  side_in],
        out_specs=[pl.BlockSpec((tn, tm), lambda i, j: (col(i, j), i)),
                   pl.BlockSpec((tm, D), lambda i, j: (i, 0)),
                   side_out],
        out_shape=[jax.ShapeDtypeStruct((N, T), BF16), jax.ShapeDtypeStruct((T, D), BF16),
                   jax.ShapeDtypeStruct((side_rows, side_cols), BF16)],
        compiler_params=_compiler_params(("arbitrary", "arbitrary")),
        name="qk_proj_rope",
    )(x2, norm_g.reshape(1, D), wt_bf, row_gain, *rope_tabs, side_w)


def _vg_proj_kernel(*refs, with_f, n_col_steps, w_is_transposed, side_valid_rows):
    refs = list(refs)
    h_ref, w_ref = refs[:2]
    mm = _dot_nt if w_is_transposed else _dot
    pos = 2
    if with_f:
        wf_ref, fb_ref = refs[pos:pos + 2]
        pos += 2
    if side_valid_rows is not None:
        side_ref = refs[pos]
        pos += 1
    out_ref = refs[pos]
    pos += 1
    if with_f:
        logf_ref = refs[pos]
        pos += 1
        share = h_ref.shape[0] // n_col_steps
        rows = pl.ds(pl.multiple_of(pl.program_id(1) * share, share), share)
        f = mm(h_ref[rows, :], wf_ref[...]) + fb_ref[...]
        logf_ref[rows, :] = jnp.minimum(f, 0.0) - jnp.log(1.0 + jnp.exp(-jnp.abs(f)))
    if side_valid_rows is not None:
        side_out_ref = refs[pos]
        blk = side_ref.shape[0]
        step = pl.program_id(0) * n_col_steps + pl.program_id(1)
        row = step * blk + lax.broadcasted_iota(jnp.int32, side_ref.shape, 0)
        side_out_ref[...] = jnp.where(row < side_valid_rows, side_ref[...], 0.0).astype(BF16)
    out_ref[...] = mm(h_ref[...], w_ref[...]).astype(BF16)


def _vg_proj(h, w_bf, col0, n_cols, f_col0=None, f_bias=None, w_is_transposed=False, side_w=None):
    T, D = h.shape
    tm, tn = PROJ_TM, PROJ_TN_FROM_H
    j0 = col0 // tn
    n_row_steps, n_col_steps = T // tm, n_cols // tn
    with_f = f_col0 is not None
    col = functools.partial(_snake, n=n_col_steps)
    if w_is_transposed:
        w_spec = pl.BlockSpec((tn, D), lambda i, j: (j0 + col(i, j), 0))
    else:
        w_spec = pl.BlockSpec((D, tn), lambda i, j: (0, j0 + col(i, j)))
    in_specs = [pl.BlockSpec((tm, D), lambda i, j: (i, 0)), w_spec]
    args = [h, w_bf]
    out_specs = [pl.BlockSpec((tm, tn), lambda i, j: (i, col(i, j)))]
    out_shape = [jax.ShapeDtypeStruct((T, n_cols), BF16)]
    if with_f:
        jf = f_col0 // HEAD_DIM
        wf_spec = (pl.BlockSpec((HEAD_DIM, D), lambda i, j: (jf, 0)) if w_is_transposed
                   else pl.BlockSpec((D, HEAD_DIM), lambda i, j: (0, jf)))
        in_specs += [wf_spec, pl.BlockSpec((1, HEAD_DIM), lambda i, j: (0, 0))]
        args += [w_bf, f_bias]
    side_valid_rows = None
    if side_w is not None:
        side_valid_rows = side_w.shape[0]
        n_steps = n_row_steps * n_col_steps
        blk = pl.cdiv(pl.cdiv(side_valid_rows, n_steps), SIDE_ROW_ALIGN) * SIDE_ROW_ALIGN
        last_blk = (side_valid_rows - 1) // blk
        in_specs.append(pl.BlockSpec(
            (blk, D), lambda i, j: (jnp.minimum(i * n_col_steps + j, last_blk), 0)))
        args.append(side_w)
    if with_f:
        out_specs.append(pl.BlockSpec((tm, HEAD_DIM), lambda i, j: (i, 0)))
        out_shape.append(jax.ShapeDtypeStruct((T, HEAD_DIM), F32))
    if side_w is not None:
        out_specs.append(pl.BlockSpec((blk, D), lambda i, j: (i * n_col_steps + j, 0)))
        out_shape.append(jax.ShapeDtypeStruct((blk * n_steps, D), BF16))
    return pl.pallas_call(
        functools.partial(_vg_proj_kernel, with_f=with_f, n_col_steps=n_col_steps,
                          w_is_transposed=w_is_transposed, side_valid_rows=side_valid_rows),
        grid=(n_row_steps, n_col_steps),
        in_specs=in_specs,
        out_specs=out_specs,
        out_shape=out_shape,
        compiler_params=_compiler_params(("parallel", "arbitrary")),
        name="vg_proj_fgate" if with_f else "vg_proj",
    )(*args)


def _outproj_kernel(*refs, with_next_norm):
    if with_next_norm:
        o_ref, w_ref, x_ref, ng_ref, out_ref, h_ref, w_bf = refs
    else:
        o_ref, w_ref, x_ref, out_ref, w_bf = refs

    @pl.when(pl.program_id(0) == 0)
    def _():
        w_bf[...] = w_ref[...].astype(BF16)

    y = x_ref[...] + _dot(o_ref[...], w_bf[...])
    out_ref[...] = y
    if with_next_norm:
        ms = jnp.mean(y * y, axis=-1, keepdims=True)
        h_ref[...] = (y * lax.rsqrt(ms + EPS) * ng_ref[...]).astype(BF16)


def _outproj(o_bf, w, x2, next_norm_g=None):
    T, D = x2.shape
    K = o_bf.shape[1]
    tm = OUT_TM
    with_next_norm = next_norm_g is not None
    row_block = pl.BlockSpec((tm, D), lambda i: (i, 0))
    in_specs = [pl.BlockSpec((tm, K), lambda i: (i, 0)),
                pl.BlockSpec((K, D), lambda i: (0, 0), pipeline_mode=pl.Buffered(1)),
                row_block]
    args = [o_bf, w, x2]
    out_specs = [row_block]
    out_shape = [jax.ShapeDtypeStruct((T, D), F32)]
    if with_next_norm:
        in_specs.append(pl.BlockSpec((1, D), lambda i: (0, 0)))
        args.append(next_norm_g.reshape(1, D))
        out_specs.append(row_block)
        out_shape.append(jax.ShapeDtypeStruct((T, D), BF16))
    return pl.pallas_call(
        functools.partial(_outproj_kernel, with_next_norm=with_next_norm),
        grid=(T // tm,),
        in_specs=in_specs,
        out_specs=out_specs,
        out_shape=out_shape,
        scratch_shapes=[pltpu.VMEM((K, D), BF16)],
        compiler_params=_compiler_params(("arbitrary",)),
        name="outproj_residual_norm" if with_next_norm else "outproj_residual",
    )(*args)


LOGIT_BOUND = 46.0


def _logits_are_bounded(q_gain, k_gain):
    bound = 1.02 * HEAD_DIM * jnp.max(jnp.abs(q_gain)) * jnp.max(jnp.abs(k_gain))
    return (bound < LOGIT_BOUND).astype(jnp.int32).reshape(1)


def _softmax_t(s_t, r0, mask_t, subtract_max, keep_f32=False):
    last = jnp.where(mask_t, s_t[r0:, :], MASKED)
    first = s_t[:r0, :] if r0 > 0 else None
    if subtract_max:
        m = jnp.max(last, axis=0, keepdims=True)
        if r0 > 0:
            m = jnp.maximum(m, jnp.max(first, axis=0, keepdims=True))
            first = first - m
        last = last - m
    e = jnp.exp2(last) if r0 == 0 else jnp.concatenate([jnp.exp2(first), jnp.exp2(last)], axis=0)
    return e if keep_f32 else e.astype(BF16)


def _fill_vt(vt_scr, v):
    d = v.shape[1]
    vt_scr[:d, :] = v.T
    vt_scr[d:, :] = jnp.ones((vt_scr.shape[0] - d, vt_scr.shape[1]), BF16)


def _silu(g):
    return g * (1.0 / (1.0 + jnp.exp(-g)))


def _diff_attn_kernel(bounded_ref, qt_ref, kt_ref, v_ref, g_ref, lq1_ref, lk1_ref, lq2_ref, lk2_ref,
                      sg_ref, out_ref, vt_scr, k_scr, *, lambda_init):
    S = v_ref.shape[0]
    tq = ATTN_TQ
    dv = 2 * HEAD_DIM
    n_heads = v_ref.shape[1] // dv
    lam = (jnp.exp(jnp.sum(lq1_ref[...] * lk1_ref[...], axis=-1, keepdims=True))
           - jnp.exp(jnp.sum(lq2_ref[...] * lk2_ref[...], axis=-1, keepdims=True))
           + lambda_init)
    key_chunk = lax.broadcasted_iota(jnp.int32, (tq, tq), 0) // CHUNK
    qry_chunk = lax.broadcasted_iota(jnp.int32, (tq, tq), 1) // CHUNK
    mask_t = key_chunk <= qry_chunk
    sub_gain = sg_ref[...] * (1.0 - lambda_init)

    maps = [slice(m * HEAD_DIM, (m + 1) * HEAD_DIM) for m in range(2 * n_heads)]

    def attend(subtract_max):
        for hh in range(n_heads):
            _fill_vt(vt_scr.at[hh], v_ref[:, hh * dv:(hh + 1) * dv])
        k_scr[...] = kt_ref[...].T
        for qi in range(S // tq):
            r0 = qi * tq
            kv = r0 + tq
            s_ts = [_dot(k_scr[:kv, sl], qt_ref[sl, r0:kv]) for sl in maps]
            if not subtract_max:
                e_ts = [_softmax_t(s_t, r0, mask_t, False, keep_f32=True) for s_t in s_ts]
                ws = [1.0 / jnp.sum(e_t, axis=0, keepdims=True) for e_t in e_ts]
                p_ts = [(e_ts[2 * hh] * ws[2 * hh]
                         - e_ts[2 * hh + 1] * (lam * ws[2 * hh + 1])).astype(BF16)
                        for hh in range(n_heads)]
                os_t = [_dot(vt_scr[hh, :dv, :kv], p_t) for hh, p_t in enumerate(p_ts)]
            else:
                p_ts = [_softmax_t(s_t, r0, mask_t, True) for s_t in s_ts]
                os_ = [_dot(vt_scr[m // 2, :, :kv], p_t) for m, p_t in enumerate(p_ts)]
                os_t = []
                for hh in range(n_heads):
                    o1, o2 = os_[2 * hh], os_[2 * hh + 1]
                    w1 = 1.0 / o1[dv:dv + 1, :]
                    w2 = lam * (1.0 / o2[dv:dv + 1, :])
                    os_t.append(o1[:dv, :] * w1 - o2[:dv, :] * w2)
            for hh in range(n_heads):
                o = os_t[hh].T
                o = o * lax.rsqrt(jnp.mean(o * o, axis=-1, keepdims=True) + EPS) * sub_gain
                o = o * _silu(g_ref[r0:kv, hh * dv:(hh + 1) * dv].astype(F32))
                out_ref[r0:kv, hh * dv:(hh + 1) * dv] = o.astype(BF16)

    bounded = bounded_ref[0] != 0
    pl.when(bounded)(lambda: attend(subtract_max=False))
    pl.when(jnp.logical_not(bounded))(lambda: attend(subtract_max=True))


def _diff_attention(bounded, qk_t, vg, B, S, lq1, lk1, lq2, lk2, sub_g, lambda_init):
    T = B * S
    hps = DIFF_HEADS_PER_STEP
    dv = 2 * HEAD_DIM
    width = hps * dv
    heads = (qk_t.shape[0] // 2) // width
    vec = lambda a: a.reshape(1, -1)
    small = lambda n: pl.BlockSpec((1, n), lambda b, h: (0, 0))
    return pl.pallas_call(
        functools.partial(_diff_attn_kernel, lambda_init=lambda_init),
        grid=(B, heads),
        in_specs=[pl.BlockSpec(memory_space=pltpu.SMEM),
                  pl.BlockSpec((width, S), lambda b, h: (h, b)),
                  pl.BlockSpec((width, S), lambda b, h: (heads + h, b)),
                  pl.BlockSpec((S, width), lambda b, h: (b, h)),
                  pl.BlockSpec((S, width), lambda b, h: (b, heads + h)),
                  small(HEAD_DIM), small(HEAD_DIM), small(HEAD_DIM), small(HEAD_DIM),
                  small(dv)],
        out_specs=pl.BlockSpec((S, width), lambda b, h: (b, h)),
        out_shape=jax.ShapeDtypeStruct((T, heads * width), BF16),
        scratch_shapes=[pltpu.VMEM((hps, dv + BF16_ROWS, S), BF16),
                        pltpu.VMEM((S, width), BF16)],
        compiler_params=_compiler_params(("parallel", "parallel")),
        name="diff_attention",
    )(bounded, qk_t, qk_t, vg, vg, vec(lq1), vec(lk1), vec(lq2), vec(lk2), vec(sub_g))


BIAS_PARTS = 3


QUERY_LANE0 = HEAD_DIM // 2


def _cumsum_kernel(logf_ref, kkey_ref, kqry_ref, *, n_heads):
    S = logf_ref.shape[0]
    row = lax.broadcasted_iota(jnp.int32, (CUM_BLK, CUM_BLK), 0)
    col = lax.broadcasted_iota(jnp.int32, (CUM_BLK, CUM_BLK), 1)
    tri = (col <= row).astype(BF16)
    src = lax.broadcasted_iota(jnp.int32, (HEAD_DIM, HEAD_DIM), 0)
    dst = lax.broadcasted_iota(jnp.int32, (HEAD_DIM, HEAD_DIM), 1)
    is_head = src < n_heads
    spread = [jnp.where(is_head & (dst == BIAS_PARTS * src + part), 1.0,
                        jnp.where(is_head & (dst == QUERY_LANE0 + BIAS_PARTS * src + part),
                                  -1.0, 0.0)).astype(BF16)
              for part in range(BIAS_PARTS)]
    lane = lax.broadcasted_iota(jnp.int32, (CUM_BLK, HEAD_DIM), 1)
    key_lanes = lane < QUERY_LANE0
    ones_lanes = (lane >= QUERY_LANE0) & (lane < QUERY_LANE0 + BIAS_PARTS * n_heads)
    def pieces(x):
        out = []
        for _ in range(BIAS_PARTS):
            out.append(x.astype(BF16))
            x = x - out[-1].astype(F32)
        return out

    local = [sum(_dot(tri, piece) for piece in pieces(logf_ref[kb * CUM_BLK:(kb + 1) * CUM_BLK, :]))
             for kb in range(S // CUM_BLK)]
    off = jnp.zeros((1, HEAD_DIM), F32)
    for kb in range(S // CUM_BLK):
        rows = slice(kb * CUM_BLK, (kb + 1) * CUM_BLK)
        c = local[kb] + off
        off = off + local[kb][CUM_BLK - 1:CUM_BLK, :]
        packed = sum(_dot(piece, spread[part])
                     for part, piece in enumerate(pieces(c * (-LOG2E))))
        kkey_ref[rows, :] = jnp.where(key_lanes, packed, jnp.where(ones_lanes, 1.0, 0.0)).astype(BF16)
        kqry_ref[rows, :] = jnp.where(key_lanes, 0.0, packed).astype(BF16)


def _forget_bias(logf, B, S, n_heads):
    T = B * S
    assert QUERY_LANE0 + BIAS_PARTS * n_heads <= HEAD_DIM and BIAS_PARTS * n_heads <= QUERY_LANE0
    table = jax.ShapeDtypeStruct((T, HEAD_DIM), BF16)
    return pl.pallas_call(
        functools.partial(_cumsum_kernel, n_heads=n_heads),
        grid=(B,),
        in_specs=[pl.BlockSpec((S, HEAD_DIM), lambda b: (b, 0))],
        out_specs=[pl.BlockSpec((S, HEAD_DIM), lambda b: (b, 0))] * 2,
        out_shape=[table, table],
        compiler_params=_compiler_params(("parallel",)),
        name="forget_prefix_sum",
    )(logf)


def _fox_attn_kernel(bounded_ref, q_ref, k_ref, v_ref, g_ref, kkey_ref, kqry_ref, out_ref, vt_scr):
    S = q_ref.shape[0]
    tq = ATTN_TQ
    heads_per_step = q_ref.shape[1] // HEAD_DIM
    key = lax.broadcasted_iota(jnp.int32, (tq, tq), 0)
    qry = lax.broadcasted_iota(jnp.int32, (tq, tq), 1)
    mask_t = key <= qry
    lane = lax.broadcasted_iota(jnp.int32, (tq, HEAD_DIM), 1)
    lanes = [slice(hh * HEAD_DIM, (hh + 1) * HEAD_DIM) for hh in range(heads_per_step)]
    heads = [pl.program_id(1) * heads_per_step + hh for hh in range(heads_per_step)]

    def attend(subtract_max):
        for hh, sl in enumerate(lanes):
            _fill_vt(vt_scr.at[hh], v_ref[:, sl])
        for qi in range(S // tq):
            r0 = qi * tq
            kv = r0 + tq
            s_ts = []
            for sl, head in zip(lanes, heads):
                pick = lane // BIAS_PARTS == head
                extra = jnp.where(pick, 1.0, 0.0).astype(BF16)
                if not subtract_max:
                    own = (lane >= QUERY_LANE0) & ((lane - QUERY_LANE0) // BIAS_PARTS == head)
                    extra = jnp.where(own, kqry_ref[r0:kv, :], extra)
                s_ts.append(_dot_nt(jnp.concatenate([k_ref[:kv, sl], kkey_ref[:kv, :]], axis=1),
                                    jnp.concatenate([q_ref[r0:kv, sl], extra], axis=1)))
            p_ts = [_softmax_t(s_t, r0, mask_t, subtract_max) for s_t in s_ts]
            os_ = [_dot(vt_scr[hh, :, :kv], p_t) for hh, p_t in enumerate(p_ts)]
            for sl, o in zip(lanes, os_):
                o = (o[:HEAD_DIM, :] * (1.0 / o[HEAD_DIM:HEAD_DIM + 1, :])).T
                o = o * _silu(g_ref[r0:kv, sl].astype(F32))
                out_ref[r0:kv, sl] = o.astype(BF16)

    bounded = bounded_ref[0] != 0
    pl.when(bounded)(lambda: attend(subtract_max=False))
    pl.when(jnp.logical_not(bounded))(lambda: attend(subtract_max=True))


def _fox_attention(bounded, qk, vg, kkey, kqry, B, S):
    T = B * S
    hps = FOX_HEADS_PER_STEP
    width = hps * HEAD_DIM
    groups = (qk.shape[1] // 2) // width
    table = pl.BlockSpec((S, HEAD_DIM), lambda b, h: (b, 0))
    return pl.pallas_call(
        _fox_attn_kernel,
        grid=(B, groups),
        in_specs=[pl.BlockSpec(memory_space=pltpu.SMEM),
                  pl.BlockSpec((S, width), lambda b, h: (b, h)),
                  pl.BlockSpec((S, width), lambda b, h: (b, groups + h)),
                  pl.BlockSpec((S, width), lambda b, h: (b, h)),
                  pl.BlockSpec((S, width), lambda b, h: (b, groups + h)),
                  table, table],
        out_specs=pl.BlockSpec((S, width), lambda b, h: (b, h)),
        out_shape=jax.ShapeDtypeStruct((T, groups * width), BF16),
        scratch_shapes=[pltpu.VMEM((hps, HEAD_DIM + BF16_ROWS, S), BF16)],
        compiler_params=_compiler_params(("parallel", "parallel")),
        name="fox_attention",
    )(bounded, qk, qk, vg, vg, kkey, kqry)


def kernel(x, positions, a_norm, a_w_in, a_q_norm, a_k_norm, a_lambda_q1, a_lambda_k1,
           a_lambda_q2, a_lambda_k2, a_sub_norm, a_w_out, b_norm, b_w_in, b_f_bias,
           b_q_norm, b_k_norm, b_w_out):
    B, S, D = x.shape
    T = B * S
    width = a_w_out.shape[1]
    heads_per_width = width // HEAD_DIM
    q_scale = HEAD_DIM ** -0.5 * LOG2E
    x2 = x.reshape(T, D)

    def col_gain(qg, kg):
        return jnp.concatenate([jnp.tile(qg * q_scale, heads_per_width),
                                jnp.tile(kg, heads_per_width)]).reshape(1, 2 * width)

    lambda_init0 = 0.8 - 0.6 * math.exp(-0.3 * 0)
    row_gain = jnp.broadcast_to(col_gain(a_q_norm[0], a_k_norm[0]).reshape(2 * width, 1),
                                (2 * width, HEAD_DIM))
    qk_t, h, w0_vg = _qk_proj_t(x2, a_norm[0],
                                _cast_weight(a_w_in[0], 0, 2 * width, transpose_out=True),
                                row_gain, _rope_tables(positions),
                                side_w=a_w_in[0], side_col0=2 * width)
    vg, w1t = _vg_proj(h, w0_vg, 0, 2 * width, side_w=b_w_in[0].T)
    o = _diff_attention(_logits_are_bounded(a_q_norm[0] * q_scale, a_k_norm[0]), qk_t, vg, B, S,
                        a_lambda_q1[0], a_lambda_k1[0], a_lambda_q2[0], a_lambda_k2[0],
                        a_sub_norm[0], lambda_init0)
    x2, h = _outproj(o, a_w_out[0], x2, next_norm_g=b_norm[0])

    n_f = b_w_in.shape[2] - 4 * width
    fb = jnp.pad(b_f_bias[0], (0, HEAD_DIM - n_f)).reshape(1, HEAD_DIM)
    qk = _qk_proj(h, w1t, col_gain(b_q_norm[0], b_k_norm[0]))
    vg, logf = _vg_proj(h, w1t, 2 * width, 2 * width, f_col0=4 * width, f_bias=fb,
                        w_is_transposed=True)
    kkey, kqry = _forget_bias(logf, B, S, n_f)
    o = _fox_attention(_logits_are_bounded(b_q_norm[0] * q_scale, b_k_norm[0]), qk, vg, kkey, kqry,
                       B, S)
    (x2,) = _outproj(o, b_w_out[0], x2)
    return x2.reshape(B, S, D)
```

```python
import functools
import math

import numpy as np
import jax
import jax.numpy as jnp
from jax import lax
from jax.experimental import pallas as pl
from jax.experimental.pallas import tpu as pltpu

F32 = jnp.float32
BF16 = jnp.bfloat16

EPS = 1e-6
CHUNK = 64
ROPE_THETA = 500000.0
HEAD_DIM = 128
BF16_ROWS = 16
ROT_DIM = HEAD_DIM // 4
ROT_HALF = ROT_DIM // 2
LOG2E = math.log2(math.e)
MASKED = -1e30

V7X_VMEM_LIMIT_BYTES = 56 * 1024 * 1024

PROJ_TM = 1024
PROJ_TN = 1024
PROJ_TN_FROM_H = 2048
OUT_TM = 512
ATTN_TQ = 256
CUM_BLK = 256
CAST_TN = 512
SIDE_ROW_ALIGN = BF16_ROWS
DIFF_HEADS_PER_STEP = 2
FOX_HEADS_PER_STEP = 4


def _compiler_params(semantics):
    return pltpu.CompilerParams(dimension_semantics=semantics,
                                vmem_limit_bytes=V7X_VMEM_LIMIT_BYTES)


def _dot(a, b):
    return jnp.dot(a, b, preferred_element_type=F32)


def _dot_nt(a, b):
    return lax.dot_general(a, b, (((1,), (1,)), ((), ())), preferred_element_type=F32)


def _cast_kernel(w_ref, out_ref, *, flip, col_axis, n_valid):
    w = w_ref[...]
    if n_valid is not None:
        col = (pl.program_id(0) * w.shape[col_axis]
               + lax.broadcasted_iota(jnp.int32, w.shape, col_axis))
        w = jnp.where(col < n_valid, w, 0.0)
    out_ref[...] = (w.T if flip else w).astype(BF16)


def _cast_weight(w, col0, n_cols, transpose_out=False, w_is_transposed=False):
    N, K = w.shape if w_is_transposed else w.shape[::-1]
    tc = CAST_TN
    j0 = col0 // tc
    n_cols = pl.cdiv(n_cols, tc) * tc
    n_valid = N - col0 if col0 + n_cols > N else None
    if w_is_transposed:
        in_spec = pl.BlockSpec((tc, K), lambda j: (j0 + j, 0))
    else:
        in_spec = pl.BlockSpec((K, tc), lambda j: (0, j0 + j))
    if transpose_out:
        out_spec = pl.BlockSpec((tc, K), lambda j: (j, 0))
        out_shape = jax.ShapeDtypeStruct((n_cols, K), BF16)
    else:
        out_spec = pl.BlockSpec((K, tc), lambda j: (0, j))
        out_shape = jax.ShapeDtypeStruct((K, n_cols), BF16)
    flip = transpose_out != w_is_transposed
    return pl.pallas_call(
        functools.partial(_cast_kernel, flip=flip, col_axis=0 if w_is_transposed else 1,
                          n_valid=n_valid),
        grid=(n_cols // tc,),
        in_specs=[in_spec],
        out_specs=out_spec,
        out_shape=out_shape,
        compiler_params=_compiler_params(("parallel",)),
        name="cast_weight_flip" if flip else "cast_weight",
    )(w)


def _rope_tables_kernel(pos_ref, invf_ref, cos_ref, sin_ref):
    ang = invf_ref[...] * pos_ref[...]
    cos_ref[...] = jnp.cos(ang)
    sin_ref[...] = jnp.sin(ang)


def _rope_tables(positions):
    T = positions.size
    tb = 2048
    pos = positions.reshape(1, T).astype(F32)
    invf = (ROPE_THETA ** (-np.arange(ROT_HALF, dtype=np.float32) / ROT_HALF)).astype(np.float32)
    invf = jnp.asarray(invf.reshape(ROT_HALF, 1))
    tab = jax.ShapeDtypeStruct((ROT_HALF, T), F32)
    return pl.pallas_call(
        _rope_tables_kernel,
        grid=(T // tb,),
        in_specs=[pl.BlockSpec((1, tb), lambda i: (0, i)),
                  pl.BlockSpec((ROT_HALF, 1), lambda i: (0, 0))],
        out_specs=[pl.BlockSpec((ROT_HALF, tb), lambda i: (0, i))] * 2,
        out_shape=[tab, tab],
        compiler_params=_compiler_params(("parallel",)),
        name="rope_tables",
    )(pos, invf)


def _rmsnorm_to(h_ref, x_ref, ng_ref):
    xf = x_ref[...]
    ms = jnp.mean(xf * xf, axis=-1, keepdims=True)
    h_ref[...] = (xf * lax.rsqrt(ms + EPS) * ng_ref[...]).astype(BF16)


def _snake(i, j, n):
    return jnp.where(i % 2 == 0, j, n - 1 - j)


def _qk_proj_kernel(h_ref, wt_ref, cg_ref, out_ref):
    acc = _dot_nt(h_ref[...], wt_ref[...])
    heads_per_group = out_ref.shape[2] // HEAD_DIM
    for hh in range(acc.shape[1] // HEAD_DIM):
        sl = slice(hh * HEAD_DIM, (hh + 1) * HEAD_DIM)
        a = acc[:, sl]
        r = lax.rsqrt(jnp.mean(a * a, axis=-1, keepdims=True) + EPS)
        lanes = slice((hh % heads_per_group) * HEAD_DIM, (hh % heads_per_group + 1) * HEAD_DIM)
        out_ref[hh // heads_per_group, :, lanes] = (a * r * cg_ref[:, sl]).astype(BF16)


def _qk_proj(h, wt_bf, col_gain, group_width):
    T, D = h.shape
    N = col_gain.shape[1]
    tm, tn = PROJ_TM, PROJ_TN_FROM_H
    col = functools.partial(_snake, n=N // tn)
    gpt = tn // group_width
    return pl.pallas_call(
        _qk_proj_kernel,
        grid=(T // tm, N // tn),
        in_specs=[pl.BlockSpec((tm, D), lambda i, j: (i, 0)),
                  pl.BlockSpec((tn, D), lambda i, j: (col(i, j), 0)),
                  pl.BlockSpec((1, tn), lambda i, j: (0, col(i, j)))],
        out_specs=pl.BlockSpec((gpt, tm, group_width), lambda i, j: (col(i, j), i, 0)),
        out_shape=jax.ShapeDtypeStruct((N // group_width, T, group_width), BF16),
        compiler_params=_compiler_params(("parallel", "arbitrary")),
        name="qk_proj",
    )(h, wt_bf, col_gain)


def _qk_proj_t_kernel(x_ref, ng_ref, wt_ref, g_ref, cos_ref, sin_ref, side_ref, out_ref, h_ref,
                      side_out_ref):
    @pl.when(pl.program_id(1) == 0)
    def _():
        _rmsnorm_to(h_ref, x_ref, ng_ref)

    side_out_ref[...] = side_ref[...].astype(BF16)

    acc = _dot_nt(wt_ref[...], h_ref[...])
    tm = acc.shape[1]
    cos = cos_ref[...]
    sin = sin_ref[...]
    for hh in range(acc.shape[0] // HEAD_DIM):
        rows = slice(hh * HEAD_DIM, (hh + 1) * HEAD_DIM)
        a = acc[rows, :]
        r = lax.rsqrt(jnp.mean(a * a, axis=0, keepdims=True) + EPS)
        gain = jnp.concatenate([g_ref[rows, :]] * (tm // HEAD_DIM), axis=1)
        y = a * r * gain
        y1 = y[:ROT_HALF, :]
        y2 = y[ROT_HALF:ROT_DIM, :]
        y = jnp.concatenate([y1 * cos - y2 * sin, y2 * cos + y1 * sin, y[ROT_DIM:, :]], axis=0)
        out_ref[rows, :] = y.astype(BF16)


def _qk_proj_t(x2, norm_g, wt_bf, row_gain, rope_tabs, side_w, side_col0):
    T, D = x2.shape
    N = row_gain.shape[0]
    tm, tn = PROJ_TM, PROJ_TN
    n_row_steps, n_col_steps = T // tm, N // tn
    assert n_col_steps >= 2
    side_rows, side_cols = side_w.shape[0], side_w.shape[1] - side_col0
    side_blk = side_rows // (n_row_steps * n_col_steps)
    assert side_blk * n_row_steps * n_col_steps == side_rows and side_blk % SIDE_ROW_ALIGN == 0
    assert side_col0 % side_cols == 0
    side_in = pl.BlockSpec((side_blk, side_cols),
                           lambda i, j: (i * n_col_steps + j, side_col0 // side_cols))
    side_out = pl.BlockSpec((side_blk, side_cols), lambda i, j: (i * n_col_steps + j, 0))
    col = functools.partial(_snake, n=n_col_steps)

    def x_block(i, j):
        return (jnp.minimum(i + jnp.minimum(j, 1), n_row_steps - 1), 0)

    return pl.pallas_call(
        _qk_proj_t_kernel,
        grid=(n_row_steps, n_col_steps),
        in_specs=[pl.BlockSpec((tm, D), x_block),
                  pl.BlockSpec((1, D), lambda i, j: (0, 0)),
                  pl.BlockSpec((tn, D), lambda i, j: (col(i, j), 0)),
                  pl.BlockSpec((tn, HEAD_DIM), lambda i, j: (col(i, j), 0)),
                  pl.BlockSpec((ROT_HALF, tm), lambda i, j: (0, i)),
                  pl.BlockSpec((ROT_HALF, tm), lambda i, j: (0, i)),
                  side_in],
        out_specs=[pl.BlockSpec((tn, tm), lambda i, j: (col(i, j), i)),
                   pl.BlockSpec((tm, D), lambda i, j: (i, 0)),
                   side_out],
        out_shape=[jax.ShapeDtypeStruct((N, T), BF16), jax.ShapeDtypeStruct((T, D), BF16),
                   jax.ShapeDtypeStruct((side_rows, side_cols), BF16)],
        compiler_params=_compiler_params(("arbitrary", "arbitrary")),
        name="qk_proj_rope",
    )(x2, norm_g.reshape(1, D), wt_bf, row_gain, *rope_tabs, side_w)


def _vg_proj_kernel(*refs, with_f, n_col_steps, w_is_transposed, side_valid_rows):
    refs = list(refs)
    h_ref, w_ref = refs[:2]
    mm = _dot_nt if w_is_transposed else _dot
    pos = 2
    if with_f:
        wf_ref, fb_ref = refs[pos:pos + 2]
        pos += 2
    if side_valid_rows is not None:
        side_ref = refs[pos]
        pos += 1
    out_ref = refs[pos]
    pos += 1
    if with_f:
        logf_ref = refs[pos]
        pos += 1
        share = h_ref.shape[0] // n_col_steps
        rows = pl.ds(pl.multiple_of(pl.program_id(1) * share, share), share)
        f = mm(h_ref[rows, :], wf_ref[...]) + fb_ref[...]
        logf_ref[rows, :] = jnp.minimum(f, 0.0) - jnp.log(1.0 + jnp.exp(-jnp.abs(f)))
    if side_valid_rows is not None:
        side_out_ref = refs[pos]
        blk = side_ref.shape[0]
        step = pl.program_id(0) * n_col_steps + pl.program_id(1)
        row = step * blk + lax.broadcasted_iota(jnp.int32, side_ref.shape, 0)
        side_out_ref[...] = jnp.where(row < side_valid_rows, side_ref[...], 0.0).astype(BF16)
    acc = mm(h_ref[...], w_ref[...]).astype(BF16)
    if len(out_ref.shape) == 2:
        out_ref[...] = acc
    else:
        gw = out_ref.shape[2]
        for g in range(out_ref.shape[0]):
            out_ref[g] = acc[:, g * gw:(g + 1) * gw]


def _vg_proj(h, w_bf, col0, n_cols, f_col0=None, f_bias=None, w_is_transposed=False, side_w=None,
             group_width=None):
    T, D = h.shape
    tm, tn = PROJ_TM, PROJ_TN_FROM_H
    j0 = col0 // tn
    n_row_steps, n_col_steps = T // tm, n_cols // tn
    with_f = f_col0 is not None
    col = functools.partial(_snake, n=n_col_steps)
    if w_is_transposed:
        w_spec = pl.BlockSpec((tn, D), lambda i, j: (j0 + col(i, j), 0))
    else:
        w_spec = pl.BlockSpec((D, tn), lambda i, j: (0, j0 + col(i, j)))
    in_specs = [pl.BlockSpec((tm, D), lambda i, j: (i, 0)), w_spec]
    args = [h, w_bf]
    if group_width is None:
        out_specs = [pl.BlockSpec((tm, tn), lambda i, j: (i, col(i, j)))]
        out_shape = [jax.ShapeDtypeStruct((T, n_cols), BF16)]
    else:
        out_specs = [pl.BlockSpec((tn // group_width, tm, group_width),
                                  lambda i, j: (col(i, j), i, 0))]
        out_shape = [jax.ShapeDtypeStruct((n_cols // group_width, T, group_width), BF16)]
    if with_f:
        jf = f_col0 // HEAD_DIM
        wf_spec = (pl.BlockSpec((HEAD_DIM, D), lambda i, j: (jf, 0)) if w_is_transposed
                   else pl.BlockSpec((D, HEAD_DIM), lambda i, j: (0, jf)))
        in_specs += [wf_spec, pl.BlockSpec((1, HEAD_DIM), lambda i, j: (0, 0))]
        args += [w_bf, f_bias]
    side_valid_rows = None
    if side_w is not None:
        side_valid_rows = side_w.shape[0]
        n_steps = n_row_steps * n_col_steps
        blk = pl.cdiv(pl.cdiv(side_valid_rows, n_steps), SIDE_ROW_ALIGN) * SIDE_ROW_ALIGN
        last_blk = (side_valid_rows - 1) // blk
        in_specs.append(pl.BlockSpec(
            (blk, D), lambda i, j: (jnp.minimum(i * n_col_steps + j, last_blk), 0)))
        args.append(side_w)
    if with_f:
        out_specs.append(pl.BlockSpec((tm, HEAD_DIM), lambda i, j: (i, 0)))
        out_shape.append(jax.ShapeDtypeStruct((T, HEAD_DIM), F32))
    if side_w is not None:
        out_specs.append(pl.BlockSpec((blk, D), lambda i, j: (i * n_col_steps + j, 0)))
        out_shape.append(jax.ShapeDtypeStruct((blk * n_steps, D), BF16))
    return pl.pallas_call(
        functools.partial(_vg_proj_kernel, with_f=with_f, n_col_steps=n_col_steps,
                          w_is_transposed=w_is_transposed, side_valid_rows=side_valid_rows),
        grid=(n_row_steps, n_col_steps),
        in_specs=in_specs,
        out_specs=out_specs,
        out_shape=out_shape,
        compiler_params=_compiler_params(("parallel", "arbitrary")),
        name="vg_proj_fgate" if with_f else "vg_proj",
    )(*args)


def _outproj_kernel(*refs, with_next_norm):
    if with_next_norm:
        o_ref, w_ref, x_ref, ng_ref, out_ref, h_ref, w_bf = refs
    else:
        o_ref, w_ref, x_ref, out_ref, w_bf = refs

    @pl.when(pl.program_id(0) == 0)
    def _():
        w_bf[...] = w_ref[...].astype(BF16)

    y = x_ref[...] + _dot(o_ref[...], w_bf[...])
    out_ref[...] = y
    if with_next_norm:
        ms = jnp.mean(y * y, axis=-1, keepdims=True)
        h_ref[...] = (y * lax.rsqrt(ms + EPS) * ng_ref[...]).astype(BF16)


def _outproj(o_bf, w, x2, next_norm_g=None):
    T, D = x2.shape
    K = o_bf.shape[1]
    tm = OUT_TM
    with_next_norm = next_norm_g is not None
    row_block = pl.BlockSpec((tm, D), lambda i: (i, 0))
    in_specs = [pl.BlockSpec((tm, K), lambda i: (i, 0)),
                pl.BlockSpec((K, D), lambda i: (0, 0), pipeline_mode=pl.Buffered(1)),
                row_block]
    args = [o_bf, w, x2]
    out_specs = [row_block]
    out_shape = [jax.ShapeDtypeStruct((T, D), F32)]
    if with_next_norm:
        in_specs.append(pl.BlockSpec((1, D), lambda i: (0, 0)))
        args.append(next_norm_g.reshape(1, D))
        out_specs.append(row_block)
        out_shape.append(jax.ShapeDtypeStruct((T, D), BF16))
    return pl.pallas_call(
        functools.partial(_outproj_kernel, with_next_norm=with_next_norm),
        grid=(T // tm,),
        in_specs=in_specs,
        out_specs=out_specs,
        out_shape=out_shape,
        scratch_shapes=[pltpu.VMEM((K, D), BF16)],
        compiler_params=_compiler_params(("arbitrary",)),
        name="outproj_residual_norm" if with_next_norm else "outproj_residual",
    )(*args)


LOGIT_BOUND = 46.0


def _logits_are_bounded(q_gain, k_gain):
    bound = 1.02 * HEAD_DIM * jnp.max(jnp.abs(q_gain)) * jnp.max(jnp.abs(k_gain))
    return (bound < LOGIT_BOUND).astype(jnp.int32).reshape(1)


def _softmax_t(s_t, r0, mask_t, subtract_max, keep_f32=False):
    last = jnp.where(mask_t, s_t[r0:, :], MASKED)
    first = s_t[:r0, :] if r0 > 0 else None
    if subtract_max:
        m = jnp.max(last, axis=0, keepdims=True)
        if r0 > 0:
            m = jnp.maximum(m, jnp.max(first, axis=0, keepdims=True))
            first = first - m
        last = last - m
    e = jnp.exp2(last) if r0 == 0 else jnp.concatenate([jnp.exp2(first), jnp.exp2(last)], axis=0)
    return e if keep_f32 else e.astype(BF16)


def _fill_vt(vt_scr, v):
    d = v.shape[1]
    vt_scr[:d, :] = v.T
    vt_scr[d:, :] = jnp.ones((vt_scr.shape[0] - d, vt_scr.shape[1]), BF16)


def _silu(g):
    return g * (1.0 / (1.0 + jnp.exp(-g)))


def _diff_attn_kernel(bounded_ref, qt_ref, kt_ref, v_ref, g_ref, lq1_ref, lk1_ref, lq2_ref, lk2_ref,
                      sg_ref, out_ref, vt_scr, k_scr, *, lambda_init):
    S = v_ref.shape[0]
    tq = ATTN_TQ
    dv = 2 * HEAD_DIM
    n_heads = v_ref.shape[1] // dv
    lam = (jnp.exp(jnp.sum(lq1_ref[...] * lk1_ref[...], axis=-1, keepdims=True))
           - jnp.exp(jnp.sum(lq2_ref[...] * lk2_ref[...], axis=-1, keepdims=True))
           + lambda_init)
    key_chunk = lax.broadcasted_iota(jnp.int32, (tq, tq), 0) // CHUNK
    qry_chunk = lax.broadcasted_iota(jnp.int32, (tq, tq), 1) // CHUNK
    mask_t = key_chunk <= qry_chunk
    sub_gain = sg_ref[...] * (1.0 - lambda_init)

    maps = [slice(m * HEAD_DIM, (m + 1) * HEAD_DIM) for m in range(2 * n_heads)]

    def attend(subtract_max):
        for hh in range(n_heads):
            _fill_vt(vt_scr.at[hh], v_ref[:, hh * dv:(hh + 1) * dv])
        k_scr[...] = kt_ref[...].T
        for qi in range(S // tq):
            r0 = qi * tq
            kv = r0 + tq
            s_ts = [_dot(k_scr[:kv, sl], qt_ref[sl, r0:kv]) for sl in maps]
            if not subtract_max:
                e_ts = [_softmax_t(s_t, r0, mask_t, False, keep_f32=True) for s_t in s_ts]
                ws = [1.0 / jnp.sum(e_t, axis=0, keepdims=True) for e_t in e_ts]
                p_ts = [(e_ts[2 * hh] * ws[2 * hh]
                         - e_ts[2 * hh + 1] * (lam * ws[2 * hh + 1])).astype(BF16)
                        for hh in range(n_heads)]
                os_t = [_dot(vt_scr[hh, :dv, :kv], p_t) for hh, p_t in enumerate(p_ts)]
            else:
                p_ts = [_softmax_t(s_t, r0, mask_t, True) for s_t in s_ts]
                os_ = [_dot(vt_scr[m // 2, :, :kv], p_t) for m, p_t in enumerate(p_ts)]
                os_t = []
                for hh in range(n_heads):
                    o1, o2 = os_[2 * hh], os_[2 * hh + 1]
                    w1 = 1.0 / o1[dv:dv + 1, :]
                    w2 = lam * (1.0 / o2[dv:dv + 1, :])
                    os_t.append(o1[:dv, :] * w1 - o2[:dv, :] * w2)
            for hh in range(n_heads):
                o = os_t[hh].T
                o = o * lax.rsqrt(jnp.mean(o * o, axis=-1, keepdims=True) + EPS) * sub_gain
                o = o * _silu(g_ref[r0:kv, hh * dv:(hh + 1) * dv].astype(F32))
                out_ref[r0:kv, hh * dv:(hh + 1) * dv] = o.astype(BF16)

    bounded = bounded_ref[0] != 0
    pl.when(bounded)(lambda: attend(subtract_max=False))
    pl.when(jnp.logical_not(bounded))(lambda: attend(subtract_max=True))


def _diff_attention(bounded, qk_t, vg, B, S, lq1, lk1, lq2, lk2, sub_g, lambda_init):
    T = B * S
    hps = DIFF_HEADS_PER_STEP
    dv = 2 * HEAD_DIM
    width = hps * dv
    heads = (qk_t.shape[0] // 2) // width
    vec = lambda a: a.reshape(1, -1)
    small = lambda n: pl.BlockSpec((1, n), lambda b, h: (0, 0))
    return pl.pallas_call(
        functools.partial(_diff_attn_kernel, lambda_init=lambda_init),
        grid=(B, heads),
        in_specs=[pl.BlockSpec(memory_space=pltpu.SMEM),
                  pl.BlockSpec((width, S), lambda b, h: (h, b)),
                  pl.BlockSpec((width, S), lambda b, h: (heads + h, b)),
                  pl.BlockSpec((S, width), lambda b, h: (b, h)),
                  pl.BlockSpec((S, width), lambda b, h: (b, heads + h)),
                  small(HEAD_DIM), small(HEAD_DIM), small(HEAD_DIM), small(HEAD_DIM),
                  small(dv)],
        out_specs=pl.BlockSpec((S, width), lambda b, h: (b, h)),
        out_shape=jax.ShapeDtypeStruct((T, heads * width), BF16),
        scratch_shapes=[pltpu.VMEM((hps, dv + BF16_ROWS, S), BF16),
                        pltpu.VMEM((S, width), BF16)],
        compiler_params=_compiler_params(("parallel", "parallel")),
        name="diff_attention",
    )(bounded, qk_t, qk_t, vg, vg, vec(lq1), vec(lk1), vec(lq2), vec(lk2), vec(sub_g))


BIAS_PARTS = 3


QUERY_LANE0 = HEAD_DIM // 2


def _cumsum_kernel(logf_ref, kkey_ref, kqry_ref, *, n_heads):
    S = logf_ref.shape[0]
    row = lax.broadcasted_iota(jnp.int32, (CUM_BLK, CUM_BLK), 0)
    col = lax.broadcasted_iota(jnp.int32, (CUM_BLK, CUM_BLK), 1)
    tri = (col <= row).astype(BF16)
    src = lax.broadcasted_iota(jnp.int32, (HEAD_DIM, HEAD_DIM), 0)
    dst = lax.broadcasted_iota(jnp.int32, (HEAD_DIM, HEAD_DIM), 1)
    is_head = src < n_heads
    spread = [jnp.where(is_head & (dst == BIAS_PARTS * src + part), 1.0,
                        jnp.where(is_head & (dst == QUERY_LANE0 + BIAS_PARTS * src + part),
                                  -1.0, 0.0)).astype(BF16)
              for part in range(BIAS_PARTS)]
    lane = lax.broadcasted_iota(jnp.int32, (CUM_BLK, HEAD_DIM), 1)
    key_lanes = lane < QUERY_LANE0
    ones_lanes = (lane >= QUERY_LANE0) & (lane < QUERY_LANE0 + BIAS_PARTS * n_heads)
    def pieces(x):
        out = []
        for _ in range(BIAS_PARTS):
            out.append(x.astype(BF16))
            x = x - out[-1].astype(F32)
        return out

    local = [sum(_dot(tri, piece) for piece in pieces(logf_ref[kb * CUM_BLK:(kb + 1) * CUM_BLK, :]))
             for kb in range(S // CUM_BLK)]
    off = jnp.zeros((1, HEAD_DIM), F32)
    for kb in range(S // CUM_BLK):
        rows = slice(kb * CUM_BLK, (kb + 1) * CUM_BLK)
        c = local[kb] + off
        off = off + local[kb][CUM_BLK - 1:CUM_BLK, :]
        packed = sum(_dot(piece, spread[part])
                     for part, piece in enumerate(pieces(c * (-LOG2E))))
        kkey_ref[rows, :] = jnp.where(key_lanes, packed, jnp.where(ones_lanes, 1.0, 0.0)).astype(BF16)
        kqry_ref[rows, :] = jnp.where(key_lanes, 0.0, packed).astype(BF16)


def _forget_bias(logf, B, S, n_heads):
    T = B * S
    assert QUERY_LANE0 + BIAS_PARTS * n_heads <= HEAD_DIM and BIAS_PARTS * n_heads <= QUERY_LANE0
    table = jax.ShapeDtypeStruct((T, HEAD_DIM), BF16)
    return pl.pallas_call(
        functools.partial(_cumsum_kernel, n_heads=n_heads),
        grid=(B,),
        in_specs=[pl.BlockSpec((S, HEAD_DIM), lambda b: (b, 0))],
        out_specs=[pl.BlockSpec((S, HEAD_DIM), lambda b: (b, 0))] * 2,
        out_shape=[table, table],
        compiler_params=_compiler_params(("parallel",)),
        name="forget_prefix_sum",
    )(logf)


def _fox_attn_kernel(bounded_ref, q_ref, k_ref, v_ref, g_ref, kkey_ref, kqry_ref, out_ref, vt_scr):
    S = q_ref.shape[0]
    tq = ATTN_TQ
    heads_per_step = q_ref.shape[1] // HEAD_DIM
    key = lax.broadcasted_iota(jnp.int32, (tq, tq), 0)
    qry = lax.broadcasted_iota(jnp.int32, (tq, tq), 1)
    mask_t = key <= qry
    lane = lax.broadcasted_iota(jnp.int32, (tq, HEAD_DIM), 1)
    lanes = [slice(hh * HEAD_DIM, (hh + 1) * HEAD_DIM) for hh in range(heads_per_step)]
    heads = [pl.program_id(1) * heads_per_step + hh for hh in range(heads_per_step)]

    def attend(subtract_max):
        for hh, sl in enumerate(lanes):
            _fill_vt(vt_scr.at[hh], v_ref[:, sl])
        for qi in range(S // tq):
            r0 = qi * tq
            kv = r0 + tq
            s_ts = []
            for sl, head in zip(lanes, heads):
                pick = lane // BIAS_PARTS == head
                extra = jnp.where(pick, 1.0, 0.0).astype(BF16)
                if not subtract_max:
                    own = (lane >= QUERY_LANE0) & ((lane - QUERY_LANE0) // BIAS_PARTS == head)
                    extra = jnp.where(own, kqry_ref[r0:kv, :], extra)
                s_ts.append(_dot_nt(jnp.concatenate([k_ref[:kv, sl], kkey_ref[:kv, :]], axis=1),
                                    jnp.concatenate([q_ref[r0:kv, sl], extra], axis=1)))
            p_ts = [_softmax_t(s_t, r0, mask_t, subtract_max) for s_t in s_ts]
            os_ = [_dot(vt_scr[hh, :, :kv], p_t) for hh, p_t in enumerate(p_ts)]
            for sl, o in zip(lanes, os_):
                o = (o[:HEAD_DIM, :] * (1.0 / o[HEAD_DIM:HEAD_DIM + 1, :])).T
                o = o * _silu(g_ref[r0:kv, sl].astype(F32))
                out_ref[r0:kv, sl] = o.astype(BF16)

    bounded = bounded_ref[0] != 0
    pl.when(bounded)(lambda: attend(subtract_max=False))
    pl.when(jnp.logical_not(bounded))(lambda: attend(subtract_max=True))


def _fox_attention(bounded, qk, vg, kkey, kqry, B, S):
    T = B * S
    hps = FOX_HEADS_PER_STEP
    width = hps * HEAD_DIM
    assert qk.shape[2] == width and vg.shape[2] == width
    groups = qk.shape[0] // 2
    table = pl.BlockSpec((S, HEAD_DIM), lambda b, h: (b, 0))
    return pl.pallas_call(
        _fox_attn_kernel,
        grid=(B, groups),
        in_specs=[pl.BlockSpec(memory_space=pltpu.SMEM),
                  pl.BlockSpec((None, S, width), lambda b, h: (h, b, 0)),
                  pl.BlockSpec((None, S, width), lambda b, h: (groups + h, b, 0)),
                  pl.BlockSpec((None, S, width), lambda b, h: (h, b, 0)),
                  pl.BlockSpec((None, S, width), lambda b, h: (groups + h, b, 0)),
                  table, table],
        out_specs=pl.BlockSpec((S, width), lambda b, h: (b, h)),
        out_shape=jax.ShapeDtypeStruct((T, groups * width), BF16),
        scratch_shapes=[pltpu.VMEM((hps, HEAD_DIM + BF16_ROWS, S), BF16)],
        compiler_params=_compiler_params(("parallel", "parallel")),
        name="fox_attention",
    )(bounded, qk, qk, vg, vg, kkey, kqry)


def kernel(x, positions, a_norm, a_w_in, a_q_norm, a_k_norm, a_lambda_q1, a_lambda_k1,
           a_lambda_q2, a_lambda_k2, a_sub_norm, a_w_out, b_norm, b_w_in, b_f_bias,
           b_q_norm, b_k_norm, b_w_out):
    B, S, D = x.shape
    T = B * S
    width = a_w_out.shape[1]
    heads_per_width = width // HEAD_DIM
    q_scale = HEAD_DIM ** -0.5 * LOG2E
    x2 = x.reshape(T, D)

    def col_gain(qg, kg):
        return jnp.concatenate([jnp.tile(qg * q_scale, heads_per_width),
                                jnp.tile(kg, heads_per_width)]).reshape(1, 2 * width)

    lambda_init0 = 0.8 - 0.6 * math.exp(-0.3 * 0)
    row_gain = jnp.broadcast_to(col_gain(a_q_norm[0], a_k_norm[0]).reshape(2 * width, 1),
                                (2 * width, HEAD_DIM))
    qk_t, h, w0_vg = _qk_proj_t(x2, a_norm[0],
                                _cast_weight(a_w_in[0], 0, 2 * width, transpose_out=True),
                                row_gain, _rope_tables(positions),
                                side_w=a_w_in[0], side_col0=2 * width)
    vg, w1t = _vg_proj(h, w0_vg, 0, 2 * width, side_w=b_w_in[0].T)
    o = _diff_attention(_logits_are_bounded(a_q_norm[0] * q_scale, a_k_norm[0]), qk_t, vg, B, S,
                        a_lambda_q1[0], a_lambda_k1[0], a_lambda_q2[0], a_lambda_k2[0],
                        a_sub_norm[0], lambda_init0)
    x2, h = _outproj(o, a_w_out[0], x2, next_norm_g=b_norm[0])

    n_f = b_w_in.shape[2] - 4 * width
    fb = jnp.pad(b_f_bias[0], (0, HEAD_DIM - n_f)).reshape(1, HEAD_DIM)
    fox_group = FOX_HEADS_PER_STEP * HEAD_DIM
    qk = _qk_proj(h, w1t, col_gain(b_q_norm[0], b_k_norm[0]), fox_group)
    vg, logf = _vg_proj(h, w1t, 2 * width, 2 * width, f_col0=4 * width, f_bias=fb,
                        w_is_transposed=True, group_width=fox_group)
    kkey, kqry = _forget_bias(logf, B, S, n_f)
    o = _fox_attention(_logits_are_bounded(b_q_norm[0] * q_scale, b_k_norm[0]), qk, vg, kkey, kqry,
                       B, S)
    (x2,) = _outproj(o, b_w_out[0], x2)
    return x2.reshape(B, S, D)
```

```python
import functools
import math

import numpy as np
import jax
import jax.numpy as jnp
from jax import lax
from jax.experimental import pallas as pl
from jax.experimental.pallas import tpu as pltpu

F32 = jnp.float32
BF16 = jnp.bfloat16

EPS = 1e-6
CHUNK = 64
ROPE_THETA = 500000.0
HEAD_DIM = 128
BF16_ROWS = 16
ROT_DIM = HEAD_DIM // 4
ROT_HALF = ROT_DIM // 2
LOG2E = math.log2(math.e)
MASKED = -1e30

V7X_VMEM_LIMIT_BYTES = 56 * 1024 * 1024

PROJ_TM = 1024
PROJ_TN = 1024
PROJ_TN_FROM_H = 2048
OUT_TM = 512
RESIDUAL_RING = 3
ATTN_TQ = 256
CUM_BLK = 256
CAST_TN = 512
SIDE_ROW_ALIGN = BF16_ROWS
DIFF_HEADS_PER_STEP = 2
FOX_HEADS_PER_STEP = 4


def _compiler_params(semantics):
    return pltpu.CompilerParams(dimension_semantics=semantics,
                                vmem_limit_bytes=V7X_VMEM_LIMIT_BYTES)


def _dot(a, b):
    return jnp.dot(a, b, preferred_element_type=F32)


def _dot_nt(a, b):
    return lax.dot_general(a, b, (((1,), (1,)), ((), ())), preferred_element_type=F32)


def _cast_kernel(w_ref, out_ref, *, flip, col_axis, n_valid):
    w = w_ref[...]
    if n_valid is not None:
        col = (pl.program_id(0) * w.shape[col_axis]
               + lax.broadcasted_iota(jnp.int32, w.shape, col_axis))
        w = jnp.where(col < n_valid, w, 0.0)
    out_ref[...] = (w.T if flip else w).astype(BF16)


def _cast_weight(w, col0, n_cols, transpose_out=False, w_is_transposed=False):
    N, K = w.shape if w_is_transposed else w.shape[::-1]
    tc = CAST_TN
    j0 = col0 // tc
    n_cols = pl.cdiv(n_cols, tc) * tc
    n_valid = N - col0 if col0 + n_cols > N else None
    if w_is_transposed:
        in_spec = pl.BlockSpec((tc, K), lambda j: (j0 + j, 0))
    else:
        in_spec = pl.BlockSpec((K, tc), lambda j: (0, j0 + j))
    if transpose_out:
        out_spec = pl.BlockSpec((tc, K), lambda j: (j, 0))
        out_shape = jax.ShapeDtypeStruct((n_cols, K), BF16)
    else:
        out_spec = pl.BlockSpec((K, tc), lambda j: (0, j))
        out_shape = jax.ShapeDtypeStruct((K, n_cols), BF16)
    flip = transpose_out != w_is_transposed
    return pl.pallas_call(
        functools.partial(_cast_kernel, flip=flip, col_axis=0 if w_is_transposed else 1,
                          n_valid=n_valid),
        grid=(n_cols // tc,),
        in_specs=[in_spec],
        out_specs=out_spec,
        out_shape=out_shape,
        compiler_params=_compiler_params(("parallel",)),
        name="cast_weight_flip" if flip else "cast_weight",
    )(w)


def _rope_tables_kernel(pos_ref, invf_ref, cos_ref, sin_ref):
    ang = invf_ref[...] * pos_ref[...]
    cos_ref[...] = jnp.cos(ang)
    sin_ref[...] = jnp.sin(ang)


def _rope_tables(positions):
    T = positions.size
    tb = 2048
    pos = positions.reshape(1, T).astype(F32)
    invf = (ROPE_THETA ** (-np.arange(ROT_HALF, dtype=np.float32) / ROT_HALF)).astype(np.float32)
    invf = jnp.asarray(invf.reshape(ROT_HALF, 1))
    tab = jax.ShapeDtypeStruct((ROT_HALF, T), F32)
    return pl.pallas_call(
        _rope_tables_kernel,
        grid=(T // tb,),
        in_specs=[pl.BlockSpec((1, tb), lambda i: (0, i)),
                  pl.BlockSpec((ROT_HALF, 1), lambda i: (0, 0))],
        out_specs=[pl.BlockSpec((ROT_HALF, tb), lambda i: (0, i))] * 2,
        out_shape=[tab, tab],
        compiler_params=_compiler_params(("parallel",)),
        name="rope_tables",
    )(pos, invf)


def _rmsnorm_to(h_ref, x_ref, ng_ref):
    xf = x_ref[...]
    ms = jnp.mean(xf * xf, axis=-1, keepdims=True)
    h_ref[...] = (xf * lax.rsqrt(ms + EPS) * ng_ref[...]).astype(BF16)


def _qk_proj_kernel(h_ref, wt_ref, cg_ref, out_ref):
    acc = _dot_nt(h_ref[...], wt_ref[...])
    for hh in range(acc.shape[1] // HEAD_DIM):
        sl = slice(hh * HEAD_DIM, (hh + 1) * HEAD_DIM)
        a = acc[:, sl]
        r = lax.rsqrt(jnp.mean(a * a, axis=-1, keepdims=True) + EPS)
        out_ref[:, sl] = (a * r * cg_ref[:, sl]).astype(BF16)


def _qk_proj(h, wt_bf, col_gain):
    T, D = h.shape
    N = col_gain.shape[1]
    tm, tn = PROJ_TM, PROJ_TN_FROM_H
    return pl.pallas_call(
        _qk_proj_kernel,
        grid=(T // tm, N // tn),
        in_specs=[pl.BlockSpec((tm, D), lambda i, j: (i, 0)),
                  pl.BlockSpec((tn, D), lambda i, j: (j, 0)),
                  pl.BlockSpec((1, tn), lambda i, j: (0, j))],
        out_specs=pl.BlockSpec((tm, tn), lambda i, j: (i, j)),
        out_shape=jax.ShapeDtypeStruct((T, N), BF16),
        compiler_params=_compiler_params(("parallel", "arbitrary")),
        name="qk_proj",
    )(h, wt_bf, col_gain)


def _qk_proj_t_kernel(x_ref, ng_ref, wt_ref, g_ref, cos_ref, sin_ref, side_ref, out_ref, h_ref,
                      side_out_ref):
    @pl.when(pl.program_id(1) == 0)
    def _():
        _rmsnorm_to(h_ref, x_ref, ng_ref)

    side_out_ref[...] = side_ref[...].astype(BF16)

    acc = _dot_nt(wt_ref[...], h_ref[...])
    tm = acc.shape[1]
    cos = cos_ref[...]
    sin = sin_ref[...]
    for hh in range(acc.shape[0] // HEAD_DIM):
        rows = slice(hh * HEAD_DIM, (hh + 1) * HEAD_DIM)
        a = acc[rows, :]
        r = lax.rsqrt(jnp.mean(a * a, axis=0, keepdims=True) + EPS)
        gain = jnp.concatenate([g_ref[rows, :]] * (tm // HEAD_DIM), axis=1)
        y = a * r * gain
        y1 = y[:ROT_HALF, :]
        y2 = y[ROT_HALF:ROT_DIM, :]
        y = jnp.concatenate([y1 * cos - y2 * sin, y2 * cos + y1 * sin, y[ROT_DIM:, :]], axis=0)
        out_ref[rows, :] = y.astype(BF16)


def _qk_proj_t(x2, norm_g, wt_bf, row_gain, rope_tabs, side_w, side_col0):
    T, D = x2.shape
    N = row_gain.shape[0]
    tm, tn = PROJ_TM, PROJ_TN
    n_row_steps, n_col_steps = T // tm, N // tn
    assert n_col_steps >= 2
    side_rows, side_cols = side_w.shape[0], side_w.shape[1] - side_col0
    side_blk = side_rows // (n_row_steps * n_col_steps)
    assert side_blk * n_row_steps * n_col_steps == side_rows and side_blk % SIDE_ROW_ALIGN == 0
    assert side_col0 % side_cols == 0
    side_in = pl.BlockSpec((side_blk, side_cols),
                           lambda i, j: (i * n_col_steps + j, side_col0 // side_cols))
    side_out = pl.BlockSpec((side_blk, side_cols), lambda i, j: (i * n_col_steps + j, 0))

    def x_block(i, j):
        return (jnp.minimum(i + jnp.minimum(j, 1), n_row_steps - 1), 0)

    return pl.pallas_call(
        _qk_proj_t_kernel,
        grid=(n_row_steps, n_col_steps),
        in_specs=[pl.BlockSpec((tm, D), x_block),
                  pl.BlockSpec((1, D), lambda i, j: (0, 0)),
                  pl.BlockSpec((tn, D), lambda i, j: (j, 0)),
                  pl.BlockSpec((tn, HEAD_DIM), lambda i, j: (j, 0)),
                  pl.BlockSpec((ROT_HALF, tm), lambda i, j: (0, i)),
                  pl.BlockSpec((ROT_HALF, tm), lambda i, j: (0, i)),
                  side_in],
        out_specs=[pl.BlockSpec((tn, tm), lambda i, j: (j, i)),
                   pl.BlockSpec((tm, D), lambda i, j: (i, 0)),
                   side_out],
        out_shape=[jax.ShapeDtypeStruct((N, T), BF16), jax.ShapeDtypeStruct((T, D), BF16),
                   jax.ShapeDtypeStruct((side_rows, side_cols), BF16)],
        compiler_params=_compiler_params(("arbitrary", "arbitrary")),
        name="qk_proj_rope",
    )(x2, norm_g.reshape(1, D), wt_bf, row_gain, *rope_tabs, side_w)


def _vg_proj_kernel(*refs, with_f, n_col_steps, w_is_transposed, side_valid_rows):
    refs = list(refs)
    h_ref, w_ref = refs[:2]
    mm = _dot_nt if w_is_transposed else _dot
    pos = 2
    if with_f:
        wf_ref, fb_ref = refs[pos:pos + 2]
        pos += 2
    if side_valid_rows is not None:
        side_ref = refs[pos]
        pos += 1
    out_ref = refs[pos]
    pos += 1
    if with_f:
        logf_ref = refs[pos]
        pos += 1
        share = h_ref.shape[0] // n_col_steps
        rows = pl.ds(pl.multiple_of(pl.program_id(1) * share, share), share)
        f = mm(h_ref[rows, :], wf_ref[...]) + fb_ref[...]
        logf_ref[rows, :] = jnp.minimum(f, 0.0) - jnp.log(1.0 + jnp.exp(-jnp.abs(f)))
    if side_valid_rows is not None:
        side_out_ref = refs[pos]
        blk = side_ref.shape[0]
        step = pl.program_id(0) * n_col_steps + pl.program_id(1)
        row = step * blk + lax.broadcasted_iota(jnp.int32, side_ref.shape, 0)
        side_out_ref[...] = jnp.where(row < side_valid_rows, side_ref[...], 0.0).astype(BF16)
    out_ref[...] = mm(h_ref[...], w_ref[...]).astype(BF16)


def _vg_proj(h, w_bf, col0, n_cols, f_col0=None, f_bias=None, w_is_transposed=False, side_w=None):
    T, D = h.shape
    tm, tn = PROJ_TM, PROJ_TN_FROM_H
    j0 = col0 // tn
    n_row_steps, n_col_steps = T // tm, n_cols // tn
    with_f = f_col0 is not None
    if w_is_transposed:
        w_spec = pl.BlockSpec((tn, D), lambda i, j: (j0 + j, 0))
    else:
        w_spec = pl.BlockSpec((D, tn), lambda i, j: (0, j0 + j))
    in_specs = [pl.BlockSpec((tm, D), lambda i, j: (i, 0)), w_spec]
    args = [h, w_bf]
    out_specs = [pl.BlockSpec((tm, tn), lambda i, j: (i, j))]
    out_shape = [jax.ShapeDtypeStruct((T, n_cols), BF16)]
    if with_f:
        jf = f_col0 // HEAD_DIM
        wf_spec = (pl.BlockSpec((HEAD_DIM, D), lambda i, j: (jf, 0)) if w_is_transposed
                   else pl.BlockSpec((D, HEAD_DIM), lambda i, j: (0, jf)))
        in_specs += [wf_spec, pl.BlockSpec((1, HEAD_DIM), lambda i, j: (0, 0))]
        args += [w_bf, f_bias]
    side_valid_rows = None
    if side_w is not None:
        side_valid_rows = side_w.shape[0]
        n_steps = n_row_steps * n_col_steps
        blk = pl.cdiv(pl.cdiv(side_valid_rows, n_steps), SIDE_ROW_ALIGN) * SIDE_ROW_ALIGN
        last_blk = (side_valid_rows - 1) // blk
        in_specs.append(pl.BlockSpec(
            (blk, D), lambda i, j: (jnp.minimum(i * n_col_steps + j, last_blk), 0)))
        args.append(side_w)
    if with_f:
        out_specs.append(pl.BlockSpec((tm, HEAD_DIM), lambda i, j: (i, 0)))
        out_shape.append(jax.ShapeDtypeStruct((T, HEAD_DIM), F32))
    if side_w is not None:
        out_specs.append(pl.BlockSpec((blk, D), lambda i, j: (i * n_col_steps + j, 0)))
        out_shape.append(jax.ShapeDtypeStruct((blk * n_steps, D), BF16))
    return pl.pallas_call(
        functools.partial(_vg_proj_kernel, with_f=with_f, n_col_steps=n_col_steps,
                          w_is_transposed=w_is_transposed, side_valid_rows=side_valid_rows),
        grid=(n_row_steps, n_col_steps),
        in_specs=in_specs,
        out_specs=out_specs,
        out_shape=out_shape,
        compiler_params=_compiler_params(("parallel", "arbitrary")),
        name="vg_proj_fgate" if with_f else "vg_proj",
    )(*args)


def _outproj_kernel(*refs, with_next_norm, n_steps):
    if with_next_norm:
        o_ref, w_ref, x_hbm, ng_ref, out_ref, h_ref, w_bf, x_ring, x_sem = refs
    else:
        o_ref, w_ref, x_hbm, out_ref, w_bf, x_ring, x_sem = refs
    i = pl.program_id(0)
    tm = x_ring.shape[1]

    def x_copy(step):
        slot = step % RESIDUAL_RING
        return pltpu.make_async_copy(x_hbm.at[pl.ds(pl.multiple_of(step * tm, tm), tm), :],
                                     x_ring.at[slot], x_sem.at[slot])

    @pl.when(i == 0)
    def _():
        w_bf[...] = w_ref[...].astype(BF16)
        for ahead in range(RESIDUAL_RING - 1):
            x_copy(ahead).start()

    @pl.when(i + RESIDUAL_RING - 1 < n_steps)
    def _():
        x_copy(i + RESIDUAL_RING - 1).start()

    x_copy(i).wait()
    y = x_ring[i % RESIDUAL_RING] + _dot(o_ref[...], w_bf[...])
    out_ref[...] = y
    if with_next_norm:
        ms = jnp.mean(y * y, axis=-1, keepdims=True)
        h_ref[...] = (y * lax.rsqrt(ms + EPS) * ng_ref[...]).astype(BF16)


def _outproj(o_bf, w, x2, next_norm_g=None):
    T, D = x2.shape
    K = o_bf.shape[1]
    tm = OUT_TM
    with_next_norm = next_norm_g is not None
    n_steps = T // tm
    assert n_steps >= RESIDUAL_RING
    row_block = pl.BlockSpec((tm, D), lambda i: (i, 0))
    in_specs = [pl.BlockSpec((tm, K), lambda i: (i, 0)),
                pl.BlockSpec((K, D), lambda i: (0, 0), pipeline_mode=pl.Buffered(1)),
                pl.BlockSpec(memory_space=pl.ANY)]
    args = [o_bf, w, x2]
    out_specs = [row_block]
    out_shape = [jax.ShapeDtypeStruct((T, D), F32)]
    if with_next_norm:
        in_specs.append(pl.BlockSpec((1, D), lambda i: (0, 0)))
        args.append(next_norm_g.reshape(1, D))
        out_specs.append(row_block)
        out_shape.append(jax.ShapeDtypeStruct((T, D), BF16))
    return pl.pallas_call(
        functools.partial(_outproj_kernel, with_next_norm=with_next_norm, n_steps=n_steps),
        grid=(n_steps,),
        in_specs=in_specs,
        out_specs=out_specs,
        out_shape=out_shape,
        scratch_shapes=[pltpu.VMEM((K, D), BF16),
                        pltpu.VMEM((RESIDUAL_RING, tm, D), F32),
                        pltpu.SemaphoreType.DMA((RESIDUAL_RING,))],
        compiler_params=_compiler_params(("arbitrary",)),
        name="outproj_residual_norm" if with_next_norm else "outproj_residual",
    )(*args)


LOGIT_BOUND = 46.0


def _logits_are_bounded(q_gain, k_gain):
    bound = 1.02 * HEAD_DIM * jnp.max(jnp.abs(q_gain)) * jnp.max(jnp.abs(k_gain))
    return (bound < LOGIT_BOUND).astype(jnp.int32).reshape(1)


def _softmax_t(s_t, r0, mask_t, subtract_max, keep_f32=False):
    last = jnp.where(mask_t, s_t[r0:, :], MASKED)
    first = s_t[:r0, :] if r0 > 0 else None
    if subtract_max:
        m = jnp.max(last, axis=0, keepdims=True)
        if r0 > 0:
            m = jnp.maximum(m, jnp.max(first, axis=0, keepdims=True))
            first = first - m
        last = last - m
    e = jnp.exp2(last) if r0 == 0 else jnp.concatenate([jnp.exp2(first), jnp.exp2(last)], axis=0)
    return e if keep_f32 else e.astype(BF16)


def _fill_vt(vt_scr, v):
    d = v.shape[1]
    vt_scr[:d, :] = v.T
    vt_scr[d:, :] = jnp.ones((vt_scr.shape[0] - d, vt_scr.shape[1]), BF16)


def _silu(g):
    return g * (1.0 / (1.0 + jnp.exp(-g)))


def _diff_attn_kernel(bounded_ref, qt_ref, kt_ref, v_ref, g_ref, lq1_ref, lk1_ref, lq2_ref, lk2_ref,
                      sg_ref, out_ref, vt_scr, k_scr, *, lambda_init):
    S = v_ref.shape[0]
    tq = ATTN_TQ
    dv = 2 * HEAD_DIM
    n_heads = v_ref.shape[1] // dv
    lam = (jnp.exp(jnp.sum(lq1_ref[...] * lk1_ref[...], axis=-1, keepdims=True))
           - jnp.exp(jnp.sum(lq2_ref[...] * lk2_ref[...], axis=-1, keepdims=True))
           + lambda_init)
    key_chunk = lax.broadcasted_iota(jnp.int32, (tq, tq), 0) // CHUNK
    qry_chunk = lax.broadcasted_iota(jnp.int32, (tq, tq), 1) // CHUNK
    mask_t = key_chunk <= qry_chunk
    sub_gain = sg_ref[...] * (1.0 - lambda_init)

    maps = [slice(m * HEAD_DIM, (m + 1) * HEAD_DIM) for m in range(2 * n_heads)]

    def attend(subtract_max):
        for hh in range(n_heads):
            _fill_vt(vt_scr.at[hh], v_ref[:, hh * dv:(hh + 1) * dv])
        k_scr[...] = kt_ref[...].T
        for qi in range(S // tq):
            r0 = qi * tq
            kv = r0 + tq
            s_ts = [_dot(k_scr[:kv, sl], qt_ref[sl, r0:kv]) for sl in maps]
            if not subtract_max:
                e_ts = [_softmax_t(s_t, r0, mask_t, False, keep_f32=True) for s_t in s_ts]
                ws = [1.0 / jnp.sum(e_t, axis=0, keepdims=True) for e_t in e_ts]
                p_ts = [(e_ts[2 * hh] * ws[2 * hh]
                         - e_ts[2 * hh + 1] * (lam * ws[2 * hh + 1])).astype(BF16)
                        for hh in range(n_heads)]
                os_t = [_dot(vt_scr[hh, :dv, :kv], p_t) for hh, p_t in enumerate(p_ts)]
            else:
                p_ts = [_softmax_t(s_t, r0, mask_t, True) for s_t in s_ts]
                os_ = [_dot(vt_scr[m // 2, :, :kv], p_t) for m, p_t in enumerate(p_ts)]
                os_t = []
                for hh in range(n_heads):
                    o1, o2 = os_[2 * hh], os_[2 * hh + 1]
                    w1 = 1.0 / o1[dv:dv + 1, :]
                    w2 = lam * (1.0 / o2[dv:dv + 1, :])
                    os_t.append(o1[:dv, :] * w1 - o2[:dv, :] * w2)
            for hh in range(n_heads):
                o = os_t[hh].T
                o = o * lax.rsqrt(jnp.mean(o * o, axis=-1, keepdims=True) + EPS) * sub_gain
                o = o * _silu(g_ref[r0:kv, hh * dv:(hh + 1) * dv].astype(F32))
                out_ref[r0:kv, hh * dv:(hh + 1) * dv] = o.astype(BF16)

    bounded = bounded_ref[0] != 0
    pl.when(bounded)(lambda: attend(subtract_max=False))
    pl.when(jnp.logical_not(bounded))(lambda: attend(subtract_max=True))


def _diff_attention(bounded, qk_t, vg, B, S, lq1, lk1, lq2, lk2, sub_g, lambda_init):
    T = B * S
    hps = DIFF_HEADS_PER_STEP
    dv = 2 * HEAD_DIM
    width = hps * dv
    heads = (qk_t.shape[0] // 2) // width
    vec = lambda a: a.reshape(1, -1)
    small = lambda n: pl.BlockSpec((1, n), lambda b, h: (0, 0))
    return pl.pallas_call(
        functools.partial(_diff_attn_kernel, lambda_init=lambda_init),
        grid=(B, heads),
        in_specs=[pl.BlockSpec(memory_space=pltpu.SMEM),
                  pl.BlockSpec((width, S), lambda b, h: (h, b)),
                  pl.BlockSpec((width, S), lambda b, h: (heads + h, b)),
                  pl.BlockSpec((S, width), lambda b, h: (b, h)),
                  pl.BlockSpec((S, width), lambda b, h: (b, heads + h)),
                  small(HEAD_DIM), small(HEAD_DIM), small(HEAD_DIM), small(HEAD_DIM),
                  small(dv)],
        out_specs=pl.BlockSpec((S, width), lambda b, h: (b, h)),
        out_shape=jax.ShapeDtypeStruct((T, heads * width), BF16),
        scratch_shapes=[pltpu.VMEM((hps, dv + BF16_ROWS, S), BF16),
                        pltpu.VMEM((S, width), BF16)],
        compiler_params=_compiler_params(("parallel", "parallel")),
        name="diff_attention",
    )(bounded, qk_t, qk_t, vg, vg, vec(lq1), vec(lk1), vec(lq2), vec(lk2), vec(sub_g))


BIAS_PARTS = 3


QUERY_LANE0 = HEAD_DIM // 2


def _cumsum_kernel(logf_ref, kkey_ref, kqry_ref, *, n_heads):
    S = logf_ref.shape[0]
    row = lax.broadcasted_iota(jnp.int32, (CUM_BLK, CUM_BLK), 0)
    col = lax.broadcasted_iota(jnp.int32, (CUM_BLK, CUM_BLK), 1)
    tri = (col <= row).astype(BF16)
    src = lax.broadcasted_iota(jnp.int32, (HEAD_DIM, HEAD_DIM), 0)
    dst = lax.broadcasted_iota(jnp.int32, (HEAD_DIM, HEAD_DIM), 1)
    is_head = src < n_heads
    spread = [jnp.where(is_head & (dst == BIAS_PARTS * src + part), 1.0,
                        jnp.where(is_head & (dst == QUERY_LANE0 + BIAS_PARTS * src + part),
                                  -1.0, 0.0)).astype(BF16)
              for part in range(BIAS_PARTS)]
    lane = lax.broadcasted_iota(jnp.int32, (CUM_BLK, HEAD_DIM), 1)
    key_lanes = lane < QUERY_LANE0
    ones_lanes = (lane >= QUERY_LANE0) & (lane < QUERY_LANE0 + BIAS_PARTS * n_heads)
    def pieces(x):
        out = []
        for _ in range(BIAS_PARTS):
            out.append(x.astype(BF16))
            x = x - out[-1].astype(F32)
        return out

    local = [sum(_dot(tri, piece) for piece in pieces(logf_ref[kb * CUM_BLK:(kb + 1) * CUM_BLK, :]))
             for kb in range(S // CUM_BLK)]
    off = jnp.zeros((1, HEAD_DIM), F32)
    for kb in range(S // CUM_BLK):
        rows = slice(kb * CUM_BLK, (kb + 1) * CUM_BLK)
        c = local[kb] + off
        off = off + local[kb][CUM_BLK - 1:CUM_BLK, :]
        packed = sum(_dot(piece, spread[part])
                     for part, piece in enumerate(pieces(c * (-LOG2E))))
        kkey_ref[rows, :] = jnp.where(key_lanes, packed, jnp.where(ones_lanes, 1.0, 0.0)).astype(BF16)
        kqry_ref[rows, :] = jnp.where(key_lanes, 0.0, packed).astype(BF16)


def _forget_bias(logf, B, S, n_heads):
    T = B * S
    assert QUERY_LANE0 + BIAS_PARTS * n_heads <= HEAD_DIM and BIAS_PARTS * n_heads <= QUERY_LANE0
    table = jax.ShapeDtypeStruct((T, HEAD_DIM), BF16)
    return pl.pallas_call(
        functools.partial(_cumsum_kernel, n_heads=n_heads),
        grid=(B,),
        in_specs=[pl.BlockSpec((S, HEAD_DIM), lambda b: (b, 0))],
        out_specs=[pl.BlockSpec((S, HEAD_DIM), lambda b: (b, 0))] * 2,
        out_shape=[table, table],
        compiler_params=_compiler_params(("parallel",)),
        name="forget_prefix_sum",
    )(logf)


def _fox_attn_kernel(bounded_ref, q_ref, k_ref, v_ref, g_ref, kkey_ref, kqry_ref, out_ref, vt_scr):
    S = q_ref.shape[0]
    tq = ATTN_TQ
    heads_per_step = q_ref.shape[1] // HEAD_DIM
    key = lax.broadcasted_iota(jnp.int32, (tq, tq), 0)
    qry = lax.broadcasted_iota(jnp.int32, (tq, tq), 1)
    mask_t = key <= qry
    lane = lax.broadcasted_iota(jnp.int32, (tq, HEAD_DIM), 1)
    lanes = [slice(hh * HEAD_DIM, (hh + 1) * HEAD_DIM) for hh in range(heads_per_step)]
    heads = [pl.program_id(1) * heads_per_step + hh for hh in range(heads_per_step)]

    def attend(subtract_max):
        for hh, sl in enumerate(lanes):
            _fill_vt(vt_scr.at[hh], v_ref[:, sl])
        for qi in range(S // tq):
            r0 = qi * tq
            kv = r0 + tq
            s_ts = []
            for sl, head in zip(lanes, heads):
                pick = lane // BIAS_PARTS == head
                extra = jnp.where(pick, 1.0, 0.0).astype(BF16)
                if not subtract_max:
                    own = (lane >= QUERY_LANE0) & ((lane - QUERY_LANE0) // BIAS_PARTS == head)
                    extra = jnp.where(own, kqry_ref[r0:kv, :], extra)
                s_ts.append(_dot_nt(jnp.concatenate([k_ref[:kv, sl], kkey_ref[:kv, :]], axis=1),
                                    jnp.concatenate([q_ref[r0:kv, sl], extra], axis=1)))
            p_ts = [_softmax_t(s_t, r0, mask_t, subtract_max) for s_t in s_ts]
            os_ = [_dot(vt_scr[hh, :, :kv], p_t) for hh, p_t in enumerate(p_ts)]
            for sl, o in zip(lanes, os_):
                o = (o[:HEAD_DIM, :] * (1.0 / o[HEAD_DIM:HEAD_DIM + 1, :])).T
                o = o * _silu(g_ref[r0:kv, sl].astype(F32))
                out_ref[r0:kv, sl] = o.astype(BF16)

    bounded = bounded_ref[0] != 0
    pl.when(bounded)(lambda: attend(subtract_max=False))
    pl.when(jnp.logical_not(bounded))(lambda: attend(subtract_max=True))


def _fox_attention(bounded, qk, vg, kkey, kqry, B, S):
    T = B * S
    hps = FOX_HEADS_PER_STEP
    width = hps * HEAD_DIM
    groups = (qk.shape[1] // 2) // width
    table = pl.BlockSpec((S, HEAD_DIM), lambda b, h: (b, 0))
    return pl.pallas_call(
        _fox_attn_kernel,
        grid=(B, groups),
        in_specs=[pl.BlockSpec(memory_space=pltpu.SMEM),
                  pl.BlockSpec((S, width), lambda b, h: (b, h)),
                  pl.BlockSpec((S, width), lambda b, h: (b, groups + h)),
                  pl.BlockSpec((S, width), lambda b, h: (b, h)),
                  pl.BlockSpec((S, width), lambda b, h: (b, groups + h)),
                  table, table],
        out_specs=pl.BlockSpec((S, width), lambda b, h: (b, h)),
        out_shape=jax.ShapeDtypeStruct((T, groups * width), BF16),
        scratch_shapes=[pltpu.VMEM((hps, HEAD_DIM + BF16_ROWS, S), BF16)],
        compiler_params=_compiler_params(("parallel", "parallel")),
        name="fox_attention",
    )(bounded, qk, qk, vg, vg, kkey, kqry)


def kernel(x, positions, a_norm, a_w_in, a_q_norm, a_k_norm, a_lambda_q1, a_lambda_k1,
           a_lambda_q2, a_lambda_k2, a_sub_norm, a_w_out, b_norm, b_w_in, b_f_bias,
           b_q_norm, b_k_norm, b_w_out):
    B, S, D = x.shape
    T = B * S
    width = a_w_out.shape[1]
    heads_per_width = width // HEAD_DIM
    q_scale = HEAD_DIM ** -0.5 * LOG2E
    x2 = x.reshape(T, D)

    def col_gain(qg, kg):
        return jnp.concatenate([jnp.tile(qg * q_scale, heads_per_width),
                                jnp.tile(kg, heads_per_width)]).reshape(1, 2 * width)

    lambda_init0 = 0.8 - 0.6 * math.exp(-0.3 * 0)
    row_gain = jnp.broadcast_to(col_gain(a_q_norm[0], a_k_norm[0]).reshape(2 * width, 1),
                                (2 * width, HEAD_DIM))
    qk_t, h, w0_vg = _qk_proj_t(x2, a_norm[0],
                                _cast_weight(a_w_in[0], 0, 2 * width, transpose_out=True),
                                row_gain, _rope_tables(positions),
                                side_w=a_w_in[0], side_col0=2 * width)
    vg, w1t = _vg_proj(h, w0_vg, 0, 2 * width, side_w=b_w_in[0].T)
    o = _diff_attention(_logits_are_bounded(a_q_norm[0] * q_scale, a_k_norm[0]), qk_t, vg, B, S,
                        a_lambda_q1[0], a_lambda_k1[0], a_lambda_q2[0], a_lambda_k2[0],
                        a_sub_norm[0], lambda_init0)
    x2, h = _outproj(o, a_w_out[0], x2, next_norm_g=b_norm[0])

    n_f = b_w_in.shape[2] - 4 * width
    fb = jnp.pad(b_f_bias[0], (0, HEAD_DIM - n_f)).reshape(1, HEAD_DIM)
    qk = _qk_proj(h, w1t, col_gain(b_q_norm[0], b_k_norm[0]))
    vg, logf = _vg_proj(h, w1t, 2 * width, 2 * width, f_col0=4 * width, f_bias=fb,
                        w_is_transposed=True)
    kkey, kqry = _forget_bias(logf, B, S, n_f)
    o = _fox_attention(_logits_are_bounded(b_q_norm[0] * q_scale, b_k_norm[0]), qk, vg, kkey, kqry,
                       B, S)
    (x2,) = _outproj(o, b_w_out[0], x2)
    return x2.reshape(B, S, D)
```

```python
import functools
import math

import numpy as np
import jax
import jax.numpy as jnp
from jax import lax
from jax.experimental import pallas as pl
from jax.experimental.pallas import tpu as pltpu

F32 = jnp.float32
BF16 = jnp.bfloat16

EPS = 1e-6
CHUNK = 64
ROPE_THETA = 500000.0
HEAD_DIM = 128
BF16_ROWS = 16
ROT_DIM = HEAD_DIM // 4
ROT_HALF = ROT_DIM // 2
LOG2E = math.log2(math.e)
MASKED = -1e30

V7X_VMEM_LIMIT_BYTES = 56 * 1024 * 1024

PROJ_TM = 1024
PROJ_TN = 1024
PROJ_TN_FROM_H = 2048
OUT_TM = 512
ATTN_TQ = 256
CUM_BLK = 256
CAST_TN = 512
SIDE_ROW_ALIGN = BF16_ROWS
DIFF_HEADS_PER_STEP = 2
FOX_HEADS_PER_STEP = 4


def _compiler_params(semantics):
    return pltpu.CompilerParams(dimension_semantics=semantics,
                                vmem_limit_bytes=V7X_VMEM_LIMIT_BYTES)


def _dot(a, b):
    return jnp.dot(a, b, preferred_element_type=F32)


def _dot_nt(a, b):
    return lax.dot_general(a, b, (((1,), (1,)), ((), ())), preferred_element_type=F32)


def _cast_kernel(w_ref, out_ref, *, flip, col_axis, n_valid):
    w = w_ref[...]
    if n_valid is not None:
        col = (pl.program_id(0) * w.shape[col_axis]
               + lax.broadcasted_iota(jnp.int32, w.shape, col_axis))
        w = jnp.where(col < n_valid, w, 0.0)
    out_ref[...] = (w.T if flip else w).astype(BF16)


def _cast_weight(w, col0, n_cols, transpose_out=False, w_is_transposed=False):
    N, K = w.shape if w_is_transposed else w.shape[::-1]
    tc = CAST_TN
    j0 = col0 // tc
    n_cols = pl.cdiv(n_cols, tc) * tc
    n_valid = N - col0 if col0 + n_cols > N else None
    if w_is_transposed:
        in_spec = pl.BlockSpec((tc, K), lambda j: (j0 + j, 0))
    else:
        in_spec = pl.BlockSpec((K, tc), lambda j: (0, j0 + j))
    if transpose_out:
        out_spec = pl.BlockSpec((tc, K), lambda j: (j, 0))
        out_shape = jax.ShapeDtypeStruct((n_cols, K), BF16)
    else:
        out_spec = pl.BlockSpec((K, tc), lambda j: (0, j))
        out_shape = jax.ShapeDtypeStruct((K, n_cols), BF16)
    flip = transpose_out != w_is_transposed
    return pl.pallas_call(
        functools.partial(_cast_kernel, flip=flip, col_axis=0 if w_is_transposed else 1,
                          n_valid=n_valid),
        grid=(n_cols // tc,),
        in_specs=[in_spec],
        out_specs=out_spec,
        out_shape=out_shape,
        compiler_params=_compiler_params(("parallel",)),
        name="cast_weight_flip" if flip else "cast_weight",
    )(w)


def _rope_tables_kernel(pos_ref, invf_ref, cos_ref, sin_ref):
    ang = invf_ref[...] * pos_ref[...]
    cos_ref[...] = jnp.cos(ang)
    sin_ref[...] = jnp.sin(ang)


def _rope_tables(positions):
    T = positions.size
    tb = 2048
    pos = positions.reshape(1, T).astype(F32)
    invf = (ROPE_THETA ** (-np.arange(ROT_HALF, dtype=np.float32) / ROT_HALF)).astype(np.float32)
    invf = jnp.asarray(invf.reshape(ROT_HALF, 1))
    tab = jax.ShapeDtypeStruct((ROT_HALF, T), F32)
    return pl.pallas_call(
        _rope_tables_kernel,
        grid=(T // tb,),
        in_specs=[pl.BlockSpec((1, tb), lambda i: (0, i)),
                  pl.BlockSpec((ROT_HALF, 1), lambda i: (0, 0))],
        out_specs=[pl.BlockSpec((ROT_HALF, tb), lambda i: (0, i))] * 2,
        out_shape=[tab, tab],
        compiler_params=_compiler_params(("parallel",)),
        name="rope_tables",
    )(pos, invf)


def _rmsnorm_to(h_ref, x_ref, ng_ref):
    xf = x_ref[...]
    ms = jnp.mean(xf * xf, axis=-1, keepdims=True)
    h_ref[...] = (xf * lax.rsqrt(ms + EPS) * ng_ref[...]).astype(BF16)


def _qk_proj_kernel(h_ref, wt_ref, cg_ref, out_ref):
    acc = _dot_nt(h_ref[...], wt_ref[...])
    for hh in range(acc.shape[1] // HEAD_DIM):
        sl = slice(hh * HEAD_DIM, (hh + 1) * HEAD_DIM)
        a = acc[:, sl]
        r = lax.rsqrt(jnp.mean(a * a, axis=-1, keepdims=True) + EPS)
        out_ref[:, sl] = (a * r * cg_ref[:, sl]).astype(BF16)


def _qk_proj(h, wt_bf, col_gain):
    T, D = h.shape
    N = col_gain.shape[1]
    tm, tn = PROJ_TM, PROJ_TN_FROM_H
    return pl.pallas_call(
        _qk_proj_kernel,
        grid=(T // tm, N // tn),
        in_specs=[pl.BlockSpec((tm, D), lambda i, j: (i, 0)),
                  pl.BlockSpec((tn, D), lambda i, j: (j, 0)),
                  pl.BlockSpec((1, tn), lambda i, j: (0, j))],
        out_specs=pl.BlockSpec((tm, tn), lambda i, j: (i, j)),
        out_shape=jax.ShapeDtypeStruct((T, N), BF16),
        compiler_params=_compiler_params(("parallel", "arbitrary")),
        name="qk_proj",
    )(h, wt_bf, col_gain)


def _qkvg_proj_kernel(h_ref, wt_ref, cg_ref, wf_ref, fb_ref, qk_ref, vg_ref, logf_ref, *, n_qk_steps):
    j = pl.program_id(1)

    @pl.when(j < n_qk_steps)
    def _():
        _qk_proj_kernel(h_ref, wt_ref, cg_ref, qk_ref)

    @pl.when(j >= n_qk_steps)
    def _():
        share = h_ref.shape[0] // n_qk_steps
        rows = pl.ds(pl.multiple_of((j - n_qk_steps) * share, share), share)
        f = _dot_nt(h_ref[rows, :], wf_ref[...]) + fb_ref[...]
        logf_ref[rows, :] = jnp.minimum(f, 0.0) - jnp.log(1.0 + jnp.exp(-jnp.abs(f)))
        vg_ref[...] = _dot_nt(h_ref[...], wt_ref[...]).astype(BF16)


def _qkvg_proj(h, wt_bf, col_gain, f_row0, f_bias):
    T, D = h.shape
    N = col_gain.shape[1]
    tm, tn = PROJ_TM, PROJ_TN_FROM_H
    nq = N // tn
    assert f_row0 == 2 * N
    return pl.pallas_call(
        functools.partial(_qkvg_proj_kernel, n_qk_steps=nq),
        grid=(T // tm, 2 * nq),
        in_specs=[pl.BlockSpec((tm, D), lambda i, j: (i, 0)),
                  pl.BlockSpec((tn, D), lambda i, j: (j, 0)),
                  pl.BlockSpec((1, tn), lambda i, j: (0, jnp.minimum(j, nq - 1))),
                  pl.BlockSpec((HEAD_DIM, D), lambda i, j: (f_row0 // HEAD_DIM, 0)),
                  pl.BlockSpec((1, HEAD_DIM), lambda i, j: (0, 0))],
        out_specs=[pl.BlockSpec((tm, tn), lambda i, j: (i, jnp.minimum(j, nq - 1))),
                   pl.BlockSpec((tm, tn), lambda i, j: (i, jnp.maximum(j - nq, 0))),
                   pl.BlockSpec((tm, HEAD_DIM), lambda i, j: (i, 0))],
        out_shape=[jax.ShapeDtypeStruct((T, N), BF16), jax.ShapeDtypeStruct((T, N), BF16),
                   jax.ShapeDtypeStruct((T, HEAD_DIM), F32)],
        compiler_params=_compiler_params(("parallel", "arbitrary")),
        name="qkvg_proj_fgate",
    )(h, wt_bf, col_gain, wt_bf, f_bias)


def _qk_proj_t_kernel(x_ref, ng_ref, wt_ref, g_ref, cos_ref, sin_ref, side_ref, out_ref, h_ref,
                      side_out_ref):
    @pl.when(pl.program_id(1) == 0)
    def _():
        _rmsnorm_to(h_ref, x_ref, ng_ref)

    side_out_ref[...] = side_ref[...].astype(BF16)

    acc = _dot_nt(wt_ref[...], h_ref[...])
    tm = acc.shape[1]
    cos = cos_ref[...]
    sin = sin_ref[...]
    for hh in range(acc.shape[0] // HEAD_DIM):
        rows = slice(hh * HEAD_DIM, (hh + 1) * HEAD_DIM)
        a = acc[rows, :]
        r = lax.rsqrt(jnp.mean(a * a, axis=0, keepdims=True) + EPS)
        gain = jnp.concatenate([g_ref[rows, :]] * (tm // HEAD_DIM), axis=1)
        y = a * r * gain
        y1 = y[:ROT_HALF, :]
        y2 = y[ROT_HALF:ROT_DIM, :]
        y = jnp.concatenate([y1 * cos - y2 * sin, y2 * cos + y1 * sin, y[ROT_DIM:, :]], axis=0)
        out_ref[rows, :] = y.astype(BF16)


def _qk_proj_t(x2, norm_g, wt_bf, row_gain, rope_tabs, side_w, side_col0):
    T, D = x2.shape
    N = row_gain.shape[0]
    tm, tn = PROJ_TM, PROJ_TN
    n_row_steps, n_col_steps = T // tm, N // tn
    assert n_col_steps >= 2
    side_rows, side_cols = side_w.shape[0], side_w.shape[1] - side_col0
    side_blk = side_rows // (n_row_steps * n_col_steps)
    assert side_blk * n_row_steps * n_col_steps == side_rows and side_blk % SIDE_ROW_ALIGN == 0
    assert side_col0 % side_cols == 0
    side_in = pl.BlockSpec((side_blk, side_cols),
                           lambda i, j: (i * n_col_steps + j, side_col0 // side_cols))
    side_out = pl.BlockSpec((side_blk, side_cols), lambda i, j: (i * n_col_steps + j, 0))

    def x_block(i, j):
        return (jnp.minimum(i + jnp.minimum(j, 1), n_row_steps - 1), 0)

    return pl.pallas_call(
        _qk_proj_t_kernel,
        grid=(n_row_steps, n_col_steps),
        in_specs=[pl.BlockSpec((tm, D), x_block),
                  pl.BlockSpec((1, D), lambda i, j: (0, 0)),
                  pl.BlockSpec((tn, D), lambda i, j: (j, 0)),
                  pl.BlockSpec((tn, HEAD_DIM), lambda i, j: (j, 0)),
                  pl.BlockSpec((ROT_HALF, tm), lambda i, j: (0, i)),
                  pl.BlockSpec((ROT_HALF, tm), lambda i, j: (0, i)),
                  side_in],
        out_specs=[pl.BlockSpec((tn, tm), lambda i, j: (j, i)),
                   pl.BlockSpec((tm, D), lambda i, j: (i, 0)),
                   side_out],
        out_shape=[jax.ShapeDtypeStruct((N, T), BF16), jax.ShapeDtypeStruct((T, D), BF16),
                   jax.ShapeDtypeStruct((side_rows, side_cols), BF16)],
        compiler_params=_compiler_params(("arbitrary", "arbitrary")),
        name="qk_proj_rope",
    )(x2, norm_g.reshape(1, D), wt_bf, row_gain, *rope_tabs, side_w)


def _vg_proj_kernel(*refs, with_f, n_col_steps, w_is_transposed, side_valid_rows):
    refs = list(refs)
    h_ref, w_ref = refs[:2]
    mm = _dot_nt if w_is_transposed else _dot
    pos = 2
    if with_f:
        wf_ref, fb_ref = refs[pos:pos + 2]
        pos += 2
    if side_valid_rows is not None:
        side_ref = refs[pos]
        pos += 1
    out_ref = refs[pos]
    pos += 1
    if with_f:
        logf_ref = refs[pos]
        pos += 1
        share = h_ref.shape[0] // n_col_steps
        rows = pl.ds(pl.multiple_of(pl.program_id(1) * share, share), share)
        f = mm(h_ref[rows, :], wf_ref[...]) + fb_ref[...]
        logf_ref[rows, :] = jnp.minimum(f, 0.0) - jnp.log(1.0 + jnp.exp(-jnp.abs(f)))
    if side_valid_rows is not None:
        side_out_ref = refs[pos]
        blk = side_ref.shape[0]
        step = pl.program_id(0) * n_col_steps + pl.program_id(1)
        row = step * blk + lax.broadcasted_iota(jnp.int32, side_ref.shape, 0)
        side_out_ref[...] = jnp.where(row < side_valid_rows, side_ref[...], 0.0).astype(BF16)
    out_ref[...] = mm(h_ref[...], w_ref[...]).astype(BF16)


def _vg_proj(h, w_bf, col0, n_cols, f_col0=None, f_bias=None, w_is_transposed=False, side_w=None):
    T, D = h.shape
    tm, tn = PROJ_TM, PROJ_TN_FROM_H
    j0 = col0 // tn
    n_row_steps, n_col_steps = T // tm, n_cols // tn
    with_f = f_col0 is not None
    if w_is_transposed:
        w_spec = pl.BlockSpec((tn, D), lambda i, j: (j0 + j, 0))
    else:
        w_spec = pl.BlockSpec((D, tn), lambda i, j: (0, j0 + j))
    in_specs = [pl.BlockSpec((tm, D), lambda i, j: (i, 0)), w_spec]
    args = [h, w_bf]
    out_specs = [pl.BlockSpec((tm, tn), lambda i, j: (i, j))]
    out_shape = [jax.ShapeDtypeStruct((T, n_cols), BF16)]
    if with_f:
        jf = f_col0 // HEAD_DIM
        wf_spec = (pl.BlockSpec((HEAD_DIM, D), lambda i, j: (jf, 0)) if w_is_transposed
                   else pl.BlockSpec((D, HEAD_DIM), lambda i, j: (0, jf)))
        in_specs += [wf_spec, pl.BlockSpec((1, HEAD_DIM), lambda i, j: (0, 0))]
        args += [w_bf, f_bias]
    side_valid_rows = None
    if side_w is not None:
        side_valid_rows = side_w.shape[0]
        n_steps = n_row_steps * n_col_steps
        blk = pl.cdiv(pl.cdiv(side_valid_rows, n_steps), SIDE_ROW_ALIGN) * SIDE_ROW_ALIGN
        last_blk = (side_valid_rows - 1) // blk
        in_specs.append(pl.BlockSpec(
            (blk, D), lambda i, j: (jnp.minimum(i * n_col_steps + j, last_blk), 0)))
        args.append(side_w)
    if with_f:
        out_specs.append(pl.BlockSpec((tm, HEAD_DIM), lambda i, j: (i, 0)))
        out_shape.append(jax.ShapeDtypeStruct((T, HEAD_DIM), F32))
    if side_w is not None:
        out_specs.append(pl.BlockSpec((blk, D), lambda i, j: (i * n_col_steps + j, 0)))
        out_shape.append(jax.ShapeDtypeStruct((blk * n_steps, D), BF16))
    return pl.pallas_call(
        functools.partial(_vg_proj_kernel, with_f=with_f, n_col_steps=n_col_steps,
                          w_is_transposed=w_is_transposed, side_valid_rows=side_valid_rows),
        grid=(n_row_steps, n_col_steps),
        in_specs=in_specs,
        out_specs=out_specs,
        out_shape=out_shape,
        compiler_params=_compiler_params(("parallel", "arbitrary")),
        name="vg_proj_fgate" if with_f else "vg_proj",
    )(*args)


def _outproj_kernel(*refs, with_next_norm):
    if with_next_norm:
        o_ref, w_ref, x_ref, ng_ref, out_ref, h_ref, w_bf = refs
    else:
        o_ref, w_ref, x_ref, out_ref, w_bf = refs

    @pl.when(pl.program_id(0) == 0)
    def _():
        w_bf[...] = w_ref[...].astype(BF16)

    y = x_ref[...] + _dot(o_ref[...], w_bf[...])
    out_ref[...] = y
    if with_next_norm:
        ms = jnp.mean(y * y, axis=-1, keepdims=True)
        h_ref[...] = (y * lax.rsqrt(ms + EPS) * ng_ref[...]).astype(BF16)


def _outproj(o_bf, w, x2, next_norm_g=None):
    T, D = x2.shape
    K = o_bf.shape[1]
    tm = OUT_TM
    with_next_norm = next_norm_g is not None
    row_block = pl.BlockSpec((tm, D), lambda i: (i, 0))
    in_specs = [pl.BlockSpec((tm, K), lambda i: (i, 0)),
                pl.BlockSpec((K, D), lambda i: (0, 0), pipeline_mode=pl.Buffered(1)),
                row_block]
    args = [o_bf, w, x2]
    out_specs = [row_block]
    out_shape = [jax.ShapeDtypeStruct((T, D), F32)]
    if with_next_norm:
        in_specs.append(pl.BlockSpec((1, D), lambda i: (0, 0)))
        args.append(next_norm_g.reshape(1, D))
        out_specs.append(row_block)
        out_shape.append(jax.ShapeDtypeStruct((T, D), BF16))
    return pl.pallas_call(
        functools.partial(_outproj_kernel, with_next_norm=with_next_norm),
        grid=(T // tm,),
        in_specs=in_specs,
        out_specs=out_specs,
        out_shape=out_shape,
        scratch_shapes=[pltpu.VMEM((K, D), BF16)],
        compiler_params=_compiler_params(("arbitrary",)),
        name="outproj_residual_norm" if with_next_norm else "outproj_residual",
    )(*args)


LOGIT_BOUND = 46.0


def _logits_are_bounded(q_gain, k_gain):
    bound = 1.02 * HEAD_DIM * jnp.max(jnp.abs(q_gain)) * jnp.max(jnp.abs(k_gain))
    return (bound < LOGIT_BOUND).astype(jnp.int32).reshape(1)


def _softmax_t(s_t, r0, mask_t, subtract_max, keep_f32=False):
    last = jnp.where(mask_t, s_t[r0:, :], MASKED)
    first = s_t[:r0, :] if r0 > 0 else None
    if subtract_max:
        m = jnp.max(last, axis=0, keepdims=True)
        if r0 > 0:
            m = jnp.maximum(m, jnp.max(first, axis=0, keepdims=True))
            first = first - m
        last = last - m
    e = jnp.exp2(last) if r0 == 0 else jnp.concatenate([jnp.exp2(first), jnp.exp2(last)], axis=0)
    return e if keep_f32 else e.astype(BF16)


def _fill_vt(vt_scr, v):
    d = v.shape[1]
    vt_scr[:d, :] = v.T
    vt_scr[d:, :] = jnp.ones((vt_scr.shape[0] - d, vt_scr.shape[1]), BF16)


def _silu(g):
    return g * (1.0 / (1.0 + jnp.exp(-g)))


def _diff_attn_kernel(bounded_ref, qt_ref, kt_ref, v_ref, g_ref, lq1_ref, lk1_ref, lq2_ref, lk2_ref,
                      sg_ref, out_ref, vt_scr, k_scr, *, lambda_init):
    S = v_ref.shape[0]
    tq = ATTN_TQ
    dv = 2 * HEAD_DIM
    n_heads = v_ref.shape[1] // dv
    lam = (jnp.exp(jnp.sum(lq1_ref[...] * lk1_ref[...], axis=-1, keepdims=True))
           - jnp.exp(jnp.sum(lq2_ref[...] * lk2_ref[...], axis=-1, keepdims=True))
           + lambda_init)
    key_chunk = lax.broadcasted_iota(jnp.int32, (tq, tq), 0) // CHUNK
    qry_chunk = lax.broadcasted_iota(jnp.int32, (tq, tq), 1) // CHUNK
    mask_t = key_chunk <= qry_chunk
    sub_gain = sg_ref[...] * (1.0 - lambda_init)

    maps = [slice(m * HEAD_DIM, (m + 1) * HEAD_DIM) for m in range(2 * n_heads)]

    def attend(subtract_max):
        for hh in range(n_heads):
            _fill_vt(vt_scr.at[hh], v_ref[:, hh * dv:(hh + 1) * dv])
        k_scr[...] = kt_ref[...].T
        for qi in range(S // tq):
            r0 = qi * tq
            kv = r0 + tq
            s_ts = [_dot(k_scr[:kv, sl], qt_ref[sl, r0:kv]) for sl in maps]
            if not subtract_max:
                e_ts = [_softmax_t(s_t, r0, mask_t, False, keep_f32=True) for s_t in s_ts]
                ws = [1.0 / jnp.sum(e_t, axis=0, keepdims=True) for e_t in e_ts]
                p_ts = [(e_ts[2 * hh] * ws[2 * hh]
                         - e_ts[2 * hh + 1] * (lam * ws[2 * hh + 1])).astype(BF16)
                        for hh in range(n_heads)]
                os_t = [_dot(vt_scr[hh, :dv, :kv], p_t) for hh, p_t in enumerate(p_ts)]
            else:
                p_ts = [_softmax_t(s_t, r0, mask_t, True) for s_t in s_ts]
                os_ = [_dot(vt_scr[m // 2, :, :kv], p_t) for m, p_t in enumerate(p_ts)]
                os_t = []
                for hh in range(n_heads):
                    o1, o2 = os_[2 * hh], os_[2 * hh + 1]
                    w1 = 1.0 / o1[dv:dv + 1, :]
                    w2 = lam * (1.0 / o2[dv:dv + 1, :])
                    os_t.append(o1[:dv, :] * w1 - o2[:dv, :] * w2)
            for hh in range(n_heads):
                o = os_t[hh].T
                o = o * lax.rsqrt(jnp.mean(o * o, axis=-1, keepdims=True) + EPS) * sub_gain
                o = o * _silu(g_ref[r0:kv, hh * dv:(hh + 1) * dv].astype(F32))
                out_ref[r0:kv, hh * dv:(hh + 1) * dv] = o.astype(BF16)

    bounded = bounded_ref[0] != 0
    pl.when(bounded)(lambda: attend(subtract_max=False))
    pl.when(jnp.logical_not(bounded))(lambda: attend(subtract_max=True))


def _diff_attention(bounded, qk_t, vg, B, S, lq1, lk1, lq2, lk2, sub_g, lambda_init):
    T = B * S
    hps = DIFF_HEADS_PER_STEP
    dv = 2 * HEAD_DIM
    width = hps * dv
    heads = (qk_t.shape[0] // 2) // width
    vec = lambda a: a.reshape(1, -1)
    small = lambda n: pl.BlockSpec((1, n), lambda b, h: (0, 0))
    return pl.pallas_call(
        functools.partial(_diff_attn_kernel, lambda_init=lambda_init),
        grid=(B, heads),
        in_specs=[pl.BlockSpec(memory_space=pltpu.SMEM),
                  pl.BlockSpec((width, S), lambda b, h: (h, b)),
                  pl.BlockSpec((width, S), lambda b, h: (heads + h, b)),
                  pl.BlockSpec((S, width), lambda b, h: (b, h)),
                  pl.BlockSpec((S, width), lambda b, h: (b, heads + h)),
                  small(HEAD_DIM), small(HEAD_DIM), small(HEAD_DIM), small(HEAD_DIM),
                  small(dv)],
        out_specs=pl.BlockSpec((S, width), lambda b, h: (b, h)),
        out_shape=jax.ShapeDtypeStruct((T, heads * width), BF16),
        scratch_shapes=[pltpu.VMEM((hps, dv + BF16_ROWS, S), BF16),
                        pltpu.VMEM((S, width), BF16)],
        compiler_params=_compiler_params(("parallel", "parallel")),
        name="diff_attention",
    )(bounded, qk_t, qk_t, vg, vg, vec(lq1), vec(lk1), vec(lq2), vec(lk2), vec(sub_g))


BIAS_PARTS = 3


QUERY_LANE0 = HEAD_DIM // 2


def _cumsum_kernel(logf_ref, kkey_ref, kqry_ref, *, n_heads):
    S = logf_ref.shape[0]
    row = lax.broadcasted_iota(jnp.int32, (CUM_BLK, CUM_BLK), 0)
    col = lax.broadcasted_iota(jnp.int32, (CUM_BLK, CUM_BLK), 1)
    tri = (col <= row).astype(BF16)
    src = lax.broadcasted_iota(jnp.int32, (HEAD_DIM, HEAD_DIM), 0)
    dst = lax.broadcasted_iota(jnp.int32, (HEAD_DIM, HEAD_DIM), 1)
    is_head = src < n_heads
    spread = [jnp.where(is_head & (dst == BIAS_PARTS * src + part), 1.0,
                        jnp.where(is_head & (dst == QUERY_LANE0 + BIAS_PARTS * src + part),
                                  -1.0, 0.0)).astype(BF16)
              for part in range(BIAS_PARTS)]
    lane = lax.broadcasted_iota(jnp.int32, (CUM_BLK, HEAD_DIM), 1)
    key_lanes = lane < QUERY_LANE0
    ones_lanes = (lane >= QUERY_LANE0) & (lane < QUERY_LANE0 + BIAS_PARTS * n_heads)
    def pieces(x):
        out = []
        for _ in range(BIAS_PARTS):
            out.append(x.astype(BF16))
            x = x - out[-1].astype(F32)
        return out

    local = [sum(_dot(tri, piece) for piece in pieces(logf_ref[kb * CUM_BLK:(kb + 1) * CUM_BLK, :]))
             for kb in range(S // CUM_BLK)]
    off = jnp.zeros((1, HEAD_DIM), F32)
    for kb in range(S // CUM_BLK):
        rows = slice(kb * CUM_BLK, (kb + 1) * CUM_BLK)
        c = local[kb] + off
        off = off + local[kb][CUM_BLK - 1:CUM_BLK, :]
        packed = sum(_dot(piece, spread[part])
                     for part, piece in enumerate(pieces(c * (-LOG2E))))
        kkey_ref[rows, :] = jnp.where(key_lanes, packed, jnp.where(ones_lanes, 1.0, 0.0)).astype(BF16)
        kqry_ref[rows, :] = jnp.where(key_lanes, 0.0, packed).astype(BF16)


def _forget_bias(logf, B, S, n_heads):
    T = B * S
    assert QUERY_LANE0 + BIAS_PARTS * n_heads <= HEAD_DIM and BIAS_PARTS * n_heads <= QUERY_LANE0
    table = jax.ShapeDtypeStruct((T, HEAD_DIM), BF16)
    return pl.pallas_call(
        functools.partial(_cumsum_kernel, n_heads=n_heads),
        grid=(B,),
        in_specs=[pl.BlockSpec((S, HEAD_DIM), lambda b: (b, 0))],
        out_specs=[pl.BlockSpec((S, HEAD_DIM), lambda b: (b, 0))] * 2,
        out_shape=[table, table],
        compiler_params=_compiler_params(("parallel",)),
        name="forget_prefix_sum",
    )(logf)


def _fox_attn_kernel(bounded_ref, q_ref, k_ref, v_ref, g_ref, kkey_ref, kqry_ref, out_ref, vt_scr):
    S = q_ref.shape[0]
    tq = ATTN_TQ
    heads_per_step = q_ref.shape[1] // HEAD_DIM
    key = lax.broadcasted_iota(jnp.int32, (tq, tq), 0)
    qry = lax.broadcasted_iota(jnp.int32, (tq, tq), 1)
    mask_t = key <= qry
    lane = lax.broadcasted_iota(jnp.int32, (tq, HEAD_DIM), 1)
    lanes = [slice(hh * HEAD_DIM, (hh + 1) * HEAD_DIM) for hh in range(heads_per_step)]
    heads = [pl.program_id(1) * heads_per_step + hh for hh in range(heads_per_step)]

    def attend(subtract_max):
        for hh, sl in enumerate(lanes):
            _fill_vt(vt_scr.at[hh], v_ref[:, sl])
        for qi in range(S // tq):
            r0 = qi * tq
            kv = r0 + tq
            s_ts = []
            for sl, head in zip(lanes, heads):
                pick = lane // BIAS_PARTS == head
                extra = jnp.where(pick, 1.0, 0.0).astype(BF16)
                if not subtract_max:
                    own = (lane >= QUERY_LANE0) & ((lane - QUERY_LANE0) // BIAS_PARTS == head)
                    extra = jnp.where(own, kqry_ref[r0:kv, :], extra)
                s_ts.append(_dot_nt(jnp.concatenate([k_ref[:kv, sl], kkey_ref[:kv, :]], axis=1),
                                    jnp.concatenate([q_ref[r0:kv, sl], extra], axis=1)))
            p_ts = [_softmax_t(s_t, r0, mask_t, subtract_max) for s_t in s_ts]
            os_ = [_dot(vt_scr[hh, :, :kv], p_t) for hh, p_t in enumerate(p_ts)]
            for sl, o in zip(lanes, os_):
                o = (o[:HEAD_DIM, :] * (1.0 / o[HEAD_DIM:HEAD_DIM + 1, :])).T
                o = o * _silu(g_ref[r0:kv, sl].astype(F32))
                out_ref[r0:kv, sl] = o.astype(BF16)

    bounded = bounded_ref[0] != 0
    pl.when(bounded)(lambda: attend(subtract_max=False))
    pl.when(jnp.logical_not(bounded))(lambda: attend(subtract_max=True))


def _fox_attention(bounded, qk, vg, kkey, kqry, B, S):
    T = B * S
    hps = FOX_HEADS_PER_STEP
    width = hps * HEAD_DIM
    groups = (qk.shape[1] // 2) // width
    table = pl.BlockSpec((S, HEAD_DIM), lambda b, h: (b, 0))
    return pl.pallas_call(
        _fox_attn_kernel,
        grid=(B, groups),
        in_specs=[pl.BlockSpec(memory_space=pltpu.SMEM),
                  pl.BlockSpec((S, width), lambda b, h: (b, h)),
                  pl.BlockSpec((S, width), lambda b, h: (b, groups + h)),
                  pl.BlockSpec((S, width), lambda b, h: (b, h)),
                  pl.BlockSpec((S, width), lambda b, h: (b, groups + h)),
                  table, table],
        out_specs=pl.BlockSpec((S, width), lambda b, h: (b, h)),
        out_shape=jax.ShapeDtypeStruct((T, groups * width), BF16),
        scratch_shapes=[pltpu.VMEM((hps, HEAD_DIM + BF16_ROWS, S), BF16)],
        compiler_params=_compiler_params(("parallel", "parallel")),
        name="fox_attention",
    )(bounded, qk, qk, vg, vg, kkey, kqry)


def kernel(x, positions, a_norm, a_w_in, a_q_norm, a_k_norm, a_lambda_q1, a_lambda_k1,
           a_lambda_q2, a_lambda_k2, a_sub_norm, a_w_out, b_norm, b_w_in, b_f_bias,
           b_q_norm, b_k_norm, b_w_out):
    B, S, D = x.shape
    T = B * S
    width = a_w_out.shape[1]
    heads_per_width = width // HEAD_DIM
    q_scale = HEAD_DIM ** -0.5 * LOG2E
    x2 = x.reshape(T, D)

    def col_gain(qg, kg):
        return jnp.concatenate([jnp.tile(qg * q_scale, heads_per_width),
                                jnp.tile(kg, heads_per_width)]).reshape(1, 2 * width)

    lambda_init0 = 0.8 - 0.6 * math.exp(-0.3 * 0)
    row_gain = jnp.broadcast_to(col_gain(a_q_norm[0], a_k_norm[0]).reshape(2 * width, 1),
                                (2 * width, HEAD_DIM))
    qk_t, h, w0_vg = _qk_proj_t(x2, a_norm[0],
                                _cast_weight(a_w_in[0], 0, 2 * width, transpose_out=True),
                                row_gain, _rope_tables(positions),
                                side_w=a_w_in[0], side_col0=2 * width)
    vg, w1t = _vg_proj(h, w0_vg, 0, 2 * width, side_w=b_w_in[0].T)
    o = _diff_attention(_logits_are_bounded(a_q_norm[0] * q_scale, a_k_norm[0]), qk_t, vg, B, S,
                        a_lambda_q1[0], a_lambda_k1[0], a_lambda_q2[0], a_lambda_k2[0],
                        a_sub_norm[0], lambda_init0)
    x2, h = _outproj(o, a_w_out[0], x2, next_norm_g=b_norm[0])

    n_f = b_w_in.shape[2] - 4 * width
    fb = jnp.pad(b_f_bias[0], (0, HEAD_DIM - n_f)).reshape(1, HEAD_DIM)
    qk, vg, logf = _qkvg_proj(h, w1t, col_gain(b_q_norm[0], b_k_norm[0]), 4 * width, fb)
    kkey, kqry = _forget_bias(logf, B, S, n_f)
    o = _fox_attention(_logits_are_bounded(b_q_norm[0] * q_scale, b_k_norm[0]), qk, vg, kkey, kqry,
                       B, S)
    (x2,) = _outproj(o, b_w_out[0], x2)
    return x2.reshape(B, S, D)
```

```python
import functools
import math

import numpy as np
import jax
import jax.numpy as jnp
from jax import lax
from jax.experimental import pallas as pl
from jax.experimental.pallas import tpu as pltpu

F32 = jnp.float32
BF16 = jnp.bfloat16

EPS = 1e-6
CHUNK = 64
ROPE_THETA = 500000.0
HEAD_DIM = 128
BF16_ROWS = 16
ROT_DIM = HEAD_DIM // 4
ROT_HALF = ROT_DIM // 2
LOG2E = math.log2(math.e)
MASKED = -1e30

V7X_VMEM_LIMIT_BYTES = 56 * 1024 * 1024

PROJ_TM = 1024
PROJ_TN = 1024
PROJ_TN_FROM_H = 2048
OUT_TM = 512
ATTN_TQ = 256
CUM_BLK = 256
CAST_TN = 512
SIDE_ROW_ALIGN = BF16_ROWS
DIFF_HEADS_PER_STEP = 2
FOX_HEADS_PER_STEP = 4


def _compiler_params(semantics):
    return pltpu.CompilerParams(dimension_semantics=semantics,
                                vmem_limit_bytes=V7X_VMEM_LIMIT_BYTES)


def _dot(a, b):
    return jnp.dot(a, b, preferred_element_type=F32)


def _dot_nt(a, b):
    return lax.dot_general(a, b, (((1,), (1,)), ((), ())), preferred_element_type=F32)


def _cast_kernel(w_ref, out_ref, *, flip, col_axis, n_valid):
    w = w_ref[...]
    if n_valid is not None:
        col = (pl.program_id(0) * w.shape[col_axis]
               + lax.broadcasted_iota(jnp.int32, w.shape, col_axis))
        w = jnp.where(col < n_valid, w, 0.0)
    out_ref[...] = (w.T if flip else w).astype(BF16)


def _cast_weight(w, col0, n_cols, transpose_out=False, w_is_transposed=False):
    N, K = w.shape if w_is_transposed else w.shape[::-1]
    tc = CAST_TN
    j0 = col0 // tc
    n_cols = pl.cdiv(n_cols, tc) * tc
    n_valid = N - col0 if col0 + n_cols > N else None
    if w_is_transposed:
        in_spec = pl.BlockSpec((tc, K), lambda j: (j0 + j, 0))
    else:
        in_spec = pl.BlockSpec((K, tc), lambda j: (0, j0 + j))
    if transpose_out:
        out_spec = pl.BlockSpec((tc, K), lambda j: (j, 0))
        out_shape = jax.ShapeDtypeStruct((n_cols, K), BF16)
    else:
        out_spec = pl.BlockSpec((K, tc), lambda j: (0, j))
        out_shape = jax.ShapeDtypeStruct((K, n_cols), BF16)
    flip = transpose_out != w_is_transposed
    return pl.pallas_call(
        functools.partial(_cast_kernel, flip=flip, col_axis=0 if w_is_transposed else 1,
                          n_valid=n_valid),
        grid=(n_cols // tc,),
        in_specs=[in_spec],
        out_specs=out_spec,
        out_shape=out_shape,
        compiler_params=_compiler_params(("parallel",)),
        name="cast_weight_flip" if flip else "cast_weight",
    )(w)


def _rope_tables_kernel(pos_ref, invf_ref, cos_ref, sin_ref):
    ang = invf_ref[...] * pos_ref[...]
    cos_ref[...] = jnp.cos(ang)
    sin_ref[...] = jnp.sin(ang)


def _rope_tables(positions):
    T = positions.size
    tb = 2048
    pos = positions.reshape(1, T).astype(F32)
    invf = (ROPE_THETA ** (-np.arange(ROT_HALF, dtype=np.float32) / ROT_HALF)).astype(np.float32)
    invf = jnp.asarray(invf.reshape(ROT_HALF, 1))
    tab = jax.ShapeDtypeStruct((ROT_HALF, T), F32)
    return pl.pallas_call(
        _rope_tables_kernel,
        grid=(T // tb,),
        in_specs=[pl.BlockSpec((1, tb), lambda i: (0, i)),
                  pl.BlockSpec((ROT_HALF, 1), lambda i: (0, 0))],
        out_specs=[pl.BlockSpec((ROT_HALF, tb), lambda i: (0, i))] * 2,
        out_shape=[tab, tab],
        compiler_params=_compiler_params(("parallel",)),
        name="rope_tables",
    )(pos, invf)


def _rmsnorm_to(h_ref, x_ref, ng_ref):
    xf = x_ref[...]
    ms = jnp.mean(xf * xf, axis=-1, keepdims=True)
    h_ref[...] = (xf * lax.rsqrt(ms + EPS) * ng_ref[...]).astype(BF16)


def _qk_proj_kernel(h_ref, wt_ref, cg_ref, out_ref):
    acc = _dot_nt(h_ref[...], wt_ref[...])
    for hh in range(acc.shape[1] // HEAD_DIM):
        sl = slice(hh * HEAD_DIM, (hh + 1) * HEAD_DIM)
        a = acc[:, sl]
        r = lax.rsqrt(jnp.mean(a * a, axis=-1, keepdims=True) + EPS)
        out_ref[:, sl] = (a * r * cg_ref[:, sl]).astype(BF16)


def _qk_proj(h, wt_bf, col_gain):
    T, D = h.shape
    N = col_gain.shape[1]
    tm, tn = PROJ_TM, PROJ_TN_FROM_H
    return pl.pallas_call(
        _qk_proj_kernel,
        grid=(T // tm, N // tn),
        in_specs=[pl.BlockSpec((tm, D), lambda i, j: (i, 0)),
                  pl.BlockSpec((tn, D), lambda i, j: (j, 0)),
                  pl.BlockSpec((1, tn), lambda i, j: (0, j))],
        out_specs=pl.BlockSpec((tm, tn), lambda i, j: (i, j)),
        out_shape=jax.ShapeDtypeStruct((T, N), BF16),
        compiler_params=_compiler_params(("parallel", "arbitrary")),
        name="qk_proj",
    )(h, wt_bf, col_gain)


def _qk_proj_t_kernel(x_ref, ng_ref, wt_ref, g_ref, cos_ref, sin_ref, side_ref, out_ref, h_ref,
                      side_out_ref):
    @pl.when(pl.program_id(1) == 0)
    def _():
        _rmsnorm_to(h_ref, x_ref, ng_ref)

    side_out_ref[...] = side_ref[...].astype(BF16)

    acc = _dot_nt(wt_ref[...], h_ref[...])
    tm = acc.shape[1]
    cos = cos_ref[...]
    sin = sin_ref[...]
    for hh in range(acc.shape[0] // HEAD_DIM):
        rows = slice(hh * HEAD_DIM, (hh + 1) * HEAD_DIM)
        a = acc[rows, :]
        r = lax.rsqrt(jnp.mean(a * a, axis=0, keepdims=True) + EPS)
        gain = jnp.concatenate([g_ref[rows, :]] * (tm // HEAD_DIM), axis=1)
        y = a * r * gain
        y1 = y[:ROT_HALF, :]
        y2 = y[ROT_HALF:ROT_DIM, :]
        y = jnp.concatenate([y1 * cos - y2 * sin, y2 * cos + y1 * sin, y[ROT_DIM:, :]], axis=0)
        out_ref[rows, :] = y.astype(BF16)


def _qk_proj_t(x2, norm_g, wt_bf, row_gain, rope_tabs, side_w, side_col0):
    T, D = x2.shape
    N = row_gain.shape[0]
    tm, tn = PROJ_TM, PROJ_TN
    n_row_steps, n_col_steps = T // tm, N // tn
    assert n_col_steps >= 2
    side_rows, side_cols = side_w.shape[0], side_w.shape[1] - side_col0
    side_blk = side_rows // (n_row_steps * n_col_steps)
    assert side_blk * n_row_steps * n_col_steps == side_rows and side_blk % SIDE_ROW_ALIGN == 0
    assert side_col0 % side_cols == 0
    side_in = pl.BlockSpec((side_blk, side_cols),
                           lambda i, j: (i * n_col_steps + j, side_col0 // side_cols))
    side_out = pl.BlockSpec((side_blk, side_cols), lambda i, j: (i * n_col_steps + j, 0))

    def x_block(i, j):
        return (jnp.minimum(i + jnp.minimum(j, 1), n_row_steps - 1), 0)

    return pl.pallas_call(
        _qk_proj_t_kernel,
        grid=(n_row_steps, n_col_steps),
        in_specs=[pl.BlockSpec((tm, D), x_block),
                  pl.BlockSpec((1, D), lambda i, j: (0, 0)),
                  pl.BlockSpec((tn, D), lambda i, j: (j, 0)),
                  pl.BlockSpec((tn, HEAD_DIM), lambda i, j: (j, 0)),
                  pl.BlockSpec((ROT_HALF, tm), lambda i, j: (0, i)),
                  pl.BlockSpec((ROT_HALF, tm), lambda i, j: (0, i)),
                  side_in],
        out_specs=[pl.BlockSpec((tn, tm), lambda i, j: (j, i)),
                   pl.BlockSpec((tm, D), lambda i, j: (i, 0)),
                   side_out],
        out_shape=[jax.ShapeDtypeStruct((N, T), BF16), jax.ShapeDtypeStruct((T, D), BF16),
                   jax.ShapeDtypeStruct((side_rows, side_cols), BF16)],
        compiler_params=_compiler_params(("arbitrary", "arbitrary")),
        name="qk_proj_rope",
    )(x2, norm_g.reshape(1, D), wt_bf, row_gain, *rope_tabs, side_w)


def _vg_proj_kernel(*refs, with_f, n_col_steps, w_is_transposed, side_valid_rows):
    refs = list(refs)
    h_ref, w_ref = refs[:2]
    mm = _dot_nt if w_is_transposed else _dot
    pos = 2
    if with_f:
        wf_ref, fb_ref = refs[pos:pos + 2]
        pos += 2
    if side_valid_rows is not None:
        side_ref = refs[pos]
        pos += 1
    out_ref = refs[pos]
    pos += 1
    if with_f:
        logf_ref = refs[pos]
        pos += 1
        share = h_ref.shape[0] // n_col_steps
        rows = pl.ds(pl.multiple_of(pl.program_id(1) * share, share), share)
        f = mm(h_ref[rows, :], wf_ref[...]) + fb_ref[...]
        logf_ref[rows, :] = jnp.minimum(f, 0.0) - jnp.log(1.0 + jnp.exp(-jnp.abs(f)))
    if side_valid_rows is not None:
        side_out_ref = refs[pos]
        blk = side_ref.shape[0]
        step = pl.program_id(0) * n_col_steps + pl.program_id(1)
        row = step * blk + lax.broadcasted_iota(jnp.int32, side_ref.shape, 0)
        side_out_ref[...] = jnp.where(row < side_valid_rows, side_ref[...], 0.0).astype(BF16)
    out_ref[...] = mm(h_ref[...], w_ref[...]).astype(BF16)


def _vg_proj(h, w_bf, col0, n_cols, f_col0=None, f_bias=None, w_is_transposed=False, side_w=None):
    T, D = h.shape
    tm, tn = PROJ_TM, PROJ_TN_FROM_H
    j0 = col0 // tn
    n_row_steps, n_col_steps = T // tm, n_cols // tn
    with_f = f_col0 is not None
    if w_is_transposed:
        w_spec = pl.BlockSpec((tn, D), lambda i, j: (j0 + j, 0))
    else:
        w_spec = pl.BlockSpec((D, tn), lambda i, j: (0, j0 + j))
    in_specs = [pl.BlockSpec((tm, D), lambda i, j: (i, 0)), w_spec]
    args = [h, w_bf]
    out_specs = [pl.BlockSpec((tm, tn), lambda i, j: (i, j))]
    out_shape = [jax.ShapeDtypeStruct((T, n_cols), BF16)]
    if with_f:
        jf = f_col0 // HEAD_DIM
        wf_spec = (pl.BlockSpec((HEAD_DIM, D), lambda i, j: (jf, 0)) if w_is_transposed
                   else pl.BlockSpec((D, HEAD_DIM), lambda i, j: (0, jf)))
        in_specs += [wf_spec, pl.BlockSpec((1, HEAD_DIM), lambda i, j: (0, 0))]
        args += [w_bf, f_bias]
    side_valid_rows = None
    if side_w is not None:
        side_valid_rows = side_w.shape[0]
        n_steps = n_row_steps * n_col_steps
        blk = pl.cdiv(pl.cdiv(side_valid_rows, n_steps), SIDE_ROW_ALIGN) * SIDE_ROW_ALIGN
        last_blk = (side_valid_rows - 1) // blk
        in_specs.append(pl.BlockSpec(
            (blk, D), lambda i, j: (jnp.minimum(i * n_col_steps + j, last_blk), 0)))
        args.append(side_w)
    if with_f:
        out_specs.append(pl.BlockSpec((tm, HEAD_DIM), lambda i, j: (i, 0)))
        out_shape.append(jax.ShapeDtypeStruct((T, HEAD_DIM), F32))
    if side_w is not None:
        out_specs.append(pl.BlockSpec((blk, D), lambda i, j: (i * n_col_steps + j, 0)))
        out_shape.append(jax.ShapeDtypeStruct((blk * n_steps, D), BF16))
    return pl.pallas_call(
        functools.partial(_vg_proj_kernel, with_f=with_f, n_col_steps=n_col_steps,
                          w_is_transposed=w_is_transposed, side_valid_rows=side_valid_rows),
        grid=(n_row_steps, n_col_steps),
        in_specs=in_specs,
        out_specs=out_specs,
        out_shape=out_shape,
        compiler_params=_compiler_params(("parallel", "arbitrary")),
        name="vg_proj_fgate" if with_f else "vg_proj",
    )(*args)


def _outproj_kernel(*refs, with_next_norm):
    if with_next_norm:
        o_ref, w_ref, x_ref, ng_ref, out_ref, h_ref, w_bf = refs
    else:
        o_ref, w_ref, x_ref, out_ref, w_bf = refs

    @pl.when(pl.program_id(0) == 0)
    def _():
        w_bf[...] = w_ref[...].astype(BF16)

    y = x_ref[...] + _dot(o_ref[...], w_bf[...])
    out_ref[...] = y
    if with_next_norm:
        ms = jnp.mean(y * y, axis=-1, keepdims=True)
        h_ref[...] = (y * lax.rsqrt(ms + EPS) * ng_ref[...]).astype(BF16)


def _outproj(o_bf, w, x2, next_norm_g=None):
    T, D = x2.shape
    K = o_bf.shape[1]
    tm = OUT_TM
    with_next_norm = next_norm_g is not None
    row_block = pl.BlockSpec((tm, D), lambda i: (i, 0))
    in_specs = [pl.BlockSpec((tm, K), lambda i: (i, 0)),
                pl.BlockSpec((K, D), lambda i: (0, 0), pipeline_mode=pl.Buffered(1)),
                row_block]
    args = [o_bf, w, x2]
    out_specs = [row_block]
    out_shape = [jax.ShapeDtypeStruct((T, D), F32)]
    if with_next_norm:
        in_specs.append(pl.BlockSpec((1, D), lambda i: (0, 0)))
        args.append(next_norm_g.reshape(1, D))
        out_specs.append(row_block)
        out_shape.append(jax.ShapeDtypeStruct((T, D), BF16))
    return pl.pallas_call(
        functools.partial(_outproj_kernel, with_next_norm=with_next_norm),
        grid=(T // tm,),
        in_specs=in_specs,
        out_specs=out_specs,
        out_shape=out_shape,
        scratch_shapes=[pltpu.VMEM((K, D), BF16)],
        compiler_params=_compiler_params(("arbitrary",)),
        name="outproj_residual_norm" if with_next_norm else "outproj_residual",
    )(*args)


LOGIT_BOUND = 46.0


def _logits_are_bounded(q_gain, k_gain):
    bound = 1.02 * HEAD_DIM * jnp.max(jnp.abs(q_gain)) * jnp.max(jnp.abs(k_gain))
    return (bound < LOGIT_BOUND).astype(jnp.int32).reshape(1)


def _softmax_t(s_t, r0, mask_t, subtract_max, keep_f32=False):
    last = jnp.where(mask_t, s_t[r0:, :], MASKED)
    first = s_t[:r0, :] if r0 > 0 else None
    if subtract_max:
        m = jnp.max(last, axis=0, keepdims=True)
        if r0 > 0:
            m = jnp.maximum(m, jnp.max(first, axis=0, keepdims=True))
            first = first - m
        last = last - m
    e = jnp.exp2(last) if r0 == 0 else jnp.concatenate([jnp.exp2(first), jnp.exp2(last)], axis=0)
    return e if keep_f32 else e.astype(BF16)


def _fill_vt(vt_scr, v):
    d = v.shape[1]
    vt_scr[:d, :] = v.T
    vt_scr[d:, :] = jnp.ones((vt_scr.shape[0] - d, vt_scr.shape[1]), BF16)


def _silu(g):
    return g * (1.0 / (1.0 + jnp.exp(-g)))


def _diff_attn_kernel(bounded_ref, qt_ref, kt_ref, v_ref, g_ref, lq1_ref, lk1_ref, lq2_ref, lk2_ref,
                      sg_ref, out_ref, vt_scr, k_scr, *, lambda_init):
    S = v_ref.shape[0]
    tq = ATTN_TQ
    dv = 2 * HEAD_DIM
    n_heads = v_ref.shape[1] // dv
    lam = (jnp.exp(jnp.sum(lq1_ref[...] * lk1_ref[...], axis=-1, keepdims=True))
           - jnp.exp(jnp.sum(lq2_ref[...] * lk2_ref[...], axis=-1, keepdims=True))
           + lambda_init)
    key_chunk = lax.broadcasted_iota(jnp.int32, (tq, tq), 0) // CHUNK
    qry_chunk = lax.broadcasted_iota(jnp.int32, (tq, tq), 1) // CHUNK
    mask_t = key_chunk <= qry_chunk
    sub_gain = sg_ref[...] * (1.0 - lambda_init)

    maps = [slice(m * HEAD_DIM, (m + 1) * HEAD_DIM) for m in range(2 * n_heads)]

    def attend(subtract_max):
        for hh in range(n_heads):
            _fill_vt(vt_scr.at[hh], v_ref[:, hh * dv:(hh + 1) * dv])
        k_scr[...] = kt_ref[...].T
        for qi in range(S // tq):
            r0 = qi * tq
            kv = r0 + tq
            s_ts = [_dot(k_scr[:kv, sl], qt_ref[sl, r0:kv]) for sl in maps]
            if not subtract_max:
                e_ts = [_softmax_t(s_t, r0, mask_t, False, keep_f32=True) for s_t in s_ts]
                ls = [jnp.sum(e_t, axis=0, keepdims=True) for e_t in e_ts]
                p_ts = [(e_ts[2 * hh]
                         - e_ts[2 * hh + 1] * (lam * ls[2 * hh] * (1.0 / ls[2 * hh + 1]))).astype(BF16)
                        for hh in range(n_heads)]
                os_t = [_dot(vt_scr[hh, :dv, :kv], p_t) * (1.0 / ls[2 * hh])
                        for hh, p_t in enumerate(p_ts)]
            else:
                p_ts = [_softmax_t(s_t, r0, mask_t, True) for s_t in s_ts]
                os_ = [_dot(vt_scr[m // 2, :, :kv], p_t) for m, p_t in enumerate(p_ts)]
                os_t = []
                for hh in range(n_heads):
                    o1, o2 = os_[2 * hh], os_[2 * hh + 1]
                    w1 = 1.0 / o1[dv:dv + 1, :]
                    w2 = lam * (1.0 / o2[dv:dv + 1, :])
                    os_t.append(o1[:dv, :] * w1 - o2[:dv, :] * w2)
            for hh in range(n_heads):
                o = os_t[hh].T
                o = o * lax.rsqrt(jnp.mean(o * o, axis=-1, keepdims=True) + EPS) * sub_gain
                o = o * _silu(g_ref[r0:kv, hh * dv:(hh + 1) * dv].astype(F32))
                out_ref[r0:kv, hh * dv:(hh + 1) * dv] = o.astype(BF16)

    bounded = bounded_ref[0] != 0
    pl.when(bounded)(lambda: attend(subtract_max=False))
    pl.when(jnp.logical_not(bounded))(lambda: attend(subtract_max=True))


def _diff_attention(bounded, qk_t, vg, B, S, lq1, lk1, lq2, lk2, sub_g, lambda_init):
    T = B * S
    hps = DIFF_HEADS_PER_STEP
    dv = 2 * HEAD_DIM
    width = hps * dv
    heads = (qk_t.shape[0] // 2) // width
    vec = lambda a: a.reshape(1, -1)
    small = lambda n: pl.BlockSpec((1, n), lambda b, h: (0, 0))
    return pl.pallas_call(
        functools.partial(_diff_attn_kernel, lambda_init=lambda_init),
        grid=(B, heads),
        in_specs=[pl.BlockSpec(memory_space=pltpu.SMEM),
                  pl.BlockSpec((width, S), lambda b, h: (h, b)),
                  pl.BlockSpec((width, S), lambda b, h: (heads + h, b)),
                  pl.BlockSpec((S, width), lambda b, h: (b, h)),
                  pl.BlockSpec((S, width), lambda b, h: (b, heads + h)),
                  small(HEAD_DIM), small(HEAD_DIM), small(HEAD_DIM), small(HEAD_DIM),
                  small(dv)],
        out_specs=pl.BlockSpec((S, width), lambda b, h: (b, h)),
        out_shape=jax.ShapeDtypeStruct((T, heads * width), BF16),
        scratch_shapes=[pltpu.VMEM((hps, dv + BF16_ROWS, S), BF16),
                        pltpu.VMEM((S, width), BF16)],
        compiler_params=_compiler_params(("parallel", "parallel")),
        name="diff_attention",
    )(bounded, qk_t, qk_t, vg, vg, vec(lq1), vec(lk1), vec(lq2), vec(lk2), vec(sub_g))


BIAS_PARTS = 3


QUERY_LANE0 = HEAD_DIM // 2


def _cumsum_kernel(logf_ref, kkey_ref, kqry_ref, *, n_heads):
    S = logf_ref.shape[0]
    row = lax.broadcasted_iota(jnp.int32, (CUM_BLK, CUM_BLK), 0)
    col = lax.broadcasted_iota(jnp.int32, (CUM_BLK, CUM_BLK), 1)
    tri = (col <= row).astype(BF16)
    src = lax.broadcasted_iota(jnp.int32, (HEAD_DIM, HEAD_DIM), 0)
    dst = lax.broadcasted_iota(jnp.int32, (HEAD_DIM, HEAD_DIM), 1)
    is_head = src < n_heads
    spread = [jnp.where(is_head & (dst == BIAS_PARTS * src + part), 1.0,
                        jnp.where(is_head & (dst == QUERY_LANE0 + BIAS_PARTS * src + part),
                                  -1.0, 0.0)).astype(BF16)
              for part in range(BIAS_PARTS)]
    lane = lax.broadcasted_iota(jnp.int32, (CUM_BLK, HEAD_DIM), 1)
    key_lanes = lane < QUERY_LANE0
    ones_lanes = (lane >= QUERY_LANE0) & (lane < QUERY_LANE0 + BIAS_PARTS * n_heads)
    def pieces(x):
        out = []
        for _ in range(BIAS_PARTS):
            out.append(x.astype(BF16))
            x = x - out[-1].astype(F32)
        return out

    local = [sum(_dot(tri, piece) for piece in pieces(logf_ref[kb * CUM_BLK:(kb + 1) * CUM_BLK, :]))
             for kb in range(S // CUM_BLK)]
    off = jnp.zeros((1, HEAD_DIM), F32)
    for kb in range(S // CUM_BLK):
        rows = slice(kb * CUM_BLK, (kb + 1) * CUM_BLK)
        c = local[kb] + off
        off = off + local[kb][CUM_BLK - 1:CUM_BLK, :]
        packed = sum(_dot(piece, spread[part])
                     for part, piece in enumerate(pieces(c * (-LOG2E))))
        kkey_ref[rows, :] = jnp.where(key_lanes, packed, jnp.where(ones_lanes, 1.0, 0.0)).astype(BF16)
        kqry_ref[rows, :] = jnp.where(key_lanes, 0.0, packed).astype(BF16)


def _forget_bias(logf, B, S, n_heads):
    T = B * S
    assert QUERY_LANE0 + BIAS_PARTS * n_heads <= HEAD_DIM and BIAS_PARTS * n_heads <= QUERY_LANE0
    table = jax.ShapeDtypeStruct((T, HEAD_DIM), BF16)
    return pl.pallas_call(
        functools.partial(_cumsum_kernel, n_heads=n_heads),
        grid=(B,),
        in_specs=[pl.BlockSpec((S, HEAD_DIM), lambda b: (b, 0))],
        out_specs=[pl.BlockSpec((S, HEAD_DIM), lambda b: (b, 0))] * 2,
        out_shape=[table, table],
        compiler_params=_compiler_params(("parallel",)),
        name="forget_prefix_sum",
    )(logf)


def _fox_attn_kernel(bounded_ref, q_ref, k_ref, v_ref, g_ref, kkey_ref, kqry_ref, out_ref, vt_scr):
    S = q_ref.shape[0]
    tq = ATTN_TQ
    heads_per_step = q_ref.shape[1] // HEAD_DIM
    key = lax.broadcasted_iota(jnp.int32, (tq, tq), 0)
    qry = lax.broadcasted_iota(jnp.int32, (tq, tq), 1)
    mask_t = key <= qry
    lane = lax.broadcasted_iota(jnp.int32, (tq, HEAD_DIM), 1)
    lanes = [slice(hh * HEAD_DIM, (hh + 1) * HEAD_DIM) for hh in range(heads_per_step)]
    heads = [pl.program_id(1) * heads_per_step + hh for hh in range(heads_per_step)]

    def attend(subtract_max):
        for hh, sl in enumerate(lanes):
            _fill_vt(vt_scr.at[hh], v_ref[:, sl])
        for qi in range(S // tq):
            r0 = qi * tq
            kv = r0 + tq
            s_ts = []
            for sl, head in zip(lanes, heads):
                pick = lane // BIAS_PARTS == head
                extra = jnp.where(pick, 1.0, 0.0).astype(BF16)
                if not subtract_max:
                    own = (lane >= QUERY_LANE0) & ((lane - QUERY_LANE0) // BIAS_PARTS == head)
                    extra = jnp.where(own, kqry_ref[r0:kv, :], extra)
                s_ts.append(_dot_nt(jnp.concatenate([k_ref[:kv, sl], kkey_ref[:kv, :]], axis=1),
                                    jnp.concatenate([q_ref[r0:kv, sl], extra], axis=1)))
            p_ts = [_softmax_t(s_t, r0, mask_t, subtract_max) for s_t in s_ts]
            os_ = [_dot(vt_scr[hh, :, :kv], p_t) for hh, p_t in enumerate(p_ts)]
            for sl, o in zip(lanes, os_):
                o = (o[:HEAD_DIM, :] * (1.0 / o[HEAD_DIM:HEAD_DIM + 1, :])).T
                o = o * _silu(g_ref[r0:kv, sl].astype(F32))
                out_ref[r0:kv, sl] = o.astype(BF16)

    bounded = bounded_ref[0] != 0
    pl.when(bounded)(lambda: attend(subtract_max=False))
    pl.when(jnp.logical_not(bounded))(lambda: attend(subtract_max=True))


def _fox_attention(bounded, qk, vg, kkey, kqry, B, S):
    T = B * S
    hps = FOX_HEADS_PER_STEP
    width = hps * HEAD_DIM
    groups = (qk.shape[1] // 2) // width
    table = pl.BlockSpec((S, HEAD_DIM), lambda b, h: (b, 0))
    return pl.pallas_call(
        _fox_attn_kernel,
        grid=(B, groups),
        in_specs=[pl.BlockSpec(memory_space=pltpu.SMEM),
                  pl.BlockSpec((S, width), lambda b, h: (b, h)),
                  pl.BlockSpec((S, width), lambda b, h: (b, groups + h)),
                  pl.BlockSpec((S, width), lambda b, h: (b, h)),
                  pl.BlockSpec((S, width), lambda b, h: (b, groups + h)),
                  table, table],
        out_specs=pl.BlockSpec((S, width), lambda b, h: (b, h)),
        out_shape=jax.ShapeDtypeStruct((T, groups * width), BF16),
        scratch_shapes=[pltpu.VMEM((hps, HEAD_DIM + BF16_ROWS, S), BF16)],
        compiler_params=_compiler_params(("parallel", "parallel")),
        name="fox_attention",
    )(bounded, qk, qk, vg, vg, kkey, kqry)


def kernel(x, positions, a_norm, a_w_in, a_q_norm, a_k_norm, a_lambda_q1, a_lambda_k1,
           a_lambda_q2, a_lambda_k2, a_sub_norm, a_w_out, b_norm, b_w_in, b_f_bias,
           b_q_norm, b_k_norm, b_w_out):
    B, S, D = x.shape
    T = B * S
    width = a_w_out.shape[1]
    heads_per_width = width // HEAD_DIM
    q_scale = HEAD_DIM ** -0.5 * LOG2E
    x2 = x.reshape(T, D)

    def col_gain(qg, kg):
        return jnp.concatenate([jnp.tile(qg * q_scale, heads_per_width),
                                jnp.tile(kg, heads_per_width)]).reshape(1, 2 * width)

    lambda_init0 = 0.8 - 0.6 * math.exp(-0.3 * 0)
    row_gain = jnp.broadcast_to(col_gain(a_q_norm[0], a_k_norm[0]).reshape(2 * width, 1),
                                (2 * width, HEAD_DIM))
    qk_t, h, w0_vg = _qk_proj_t(x2, a_norm[0],
                                _cast_weight(a_w_in[0], 0, 2 * width, transpose_out=True),
                                row_gain, _rope_tables(positions),
                                side_w=a_w_in[0], side_col0=2 * width)
    vg, w1t = _vg_proj(h, w0_vg, 0, 2 * width, side_w=b_w_in[0].T)
    o = _diff_attention(_logits_are_bounded(a_q_norm[0] * q_scale, a_k_norm[0]), qk_t, vg, B, S,
                        a_lambda_q1[0], a_lambda_k1[0], a_lambda_q2[0], a_lambda_k2[0],
                        a_sub_norm[0], lambda_init0)
    x2, h = _outproj(o, a_w_out[0], x2, next_norm_g=b_norm[0])

    n_f = b_w_in.shape[2] - 4 * width
    fb = jnp.pad(b_f_bias[0], (0, HEAD_DIM - n_f)).reshape(1, HEAD_DIM)
    qk = _qk_proj(h, w1t, col_gain(b_q_norm[0], b_k_norm[0]))
    vg, logf = _vg_proj(h, w1t, 2 * width, 2 * width, f_col0=4 * width, f_bias=fb,
                        w_is_transposed=True)
    kkey, kqry = _forget_bias(logf, B, S, n_f)
    o = _fox_attention(_logits_are_bounded(b_q_norm[0] * q_scale, b_k_norm[0]), qk, vg, kkey, kqry,
                       B, S)
    (x2,) = _outproj(o, b_w_out[0], x2)
    return x2.reshape(B, S, D)
```
